```python
import jax, jax.numpy as jnp
from jax import lax
import numpy as np

D_MODEL = 2048
BATCH = 4
SEQ = 4096
DEPTH = 4

HEAD_DIM = 128
FOX_HEADS = D_MODEL // (2 * HEAD_DIM)
SWA_Q_HEADS = D_MODEL // (2 * HEAD_DIM)
SWA_KV_HEADS = 2
SWA_GROUP = SWA_Q_HEADS // SWA_KV_HEADS
SWA_WINDOW = 128
Q_BLOCK = 128
ROPE_THETA = 10000.0
MLSTM_HEADS = 8
MLSTM_QK_DIM = D_MODEL // (2 * MLSTM_HEADS)
MLSTM_V_DIM = D_MODEL // MLSTM_HEADS
MLSTM_CHUNK = 64
D_FF = 5632
CONV_WIDTH = 3
LN_EPS = 1e-5
DEEPNORM_ALPHA = (2 * DEPTH) ** 0.25
DEEPNORM_BETA = (8 * DEPTH) ** -0.25
FORGET_BIAS_INIT = 3.0
N_EVEN = (DEPTH + 1) // 2
N_ODD = DEPTH // 2

FOX_DIM = FOX_HEADS * HEAD_DIM
SWA_Q_DIM = SWA_Q_HEADS * HEAD_DIM
SWA_KV_DIM = SWA_KV_HEADS * HEAD_DIM
ATTN_SPLIT_SIZES = (FOX_DIM, FOX_DIM, FOX_DIM, FOX_HEADS, SWA_Q_DIM, SWA_KV_DIM, SWA_KV_DIM)
ATTN_IN_DIM = 3 * FOX_DIM + FOX_HEADS + SWA_Q_DIM + 2 * SWA_KV_DIM
FOX_F_OFF = 3 * FOX_DIM
ATTN_OUT_DIM = FOX_DIM + SWA_Q_DIM

MLSTM_QK_TOT = MLSTM_HEADS * MLSTM_QK_DIM
MLSTM_V_TOT = MLSTM_HEADS * MLSTM_V_DIM
MLSTM_SPLIT_SIZES = (MLSTM_QK_TOT, MLSTM_QK_TOT, MLSTM_V_TOT, MLSTM_V_TOT, MLSTM_HEADS, MLSTM_HEADS)
MLSTM_IN_DIM = 2 * MLSTM_QK_TOT + 2 * MLSTM_V_TOT + 2 * MLSTM_HEADS
MLSTM_F_OFF = 2 * MLSTM_QK_TOT + 2 * MLSTM_V_TOT + MLSTM_HEADS

kernel_name = "fox_swa_sink_mlstm_convffn_deepnorm_hybrid"


def split_columns(a, sizes):
    out, off = [], 0
    for s in sizes:
        out.append(a[..., off:off + s])
        off += s
    return out


def layer_norm(x, g, b):
    xf = x.astype(jnp.float32)
    mu = jnp.mean(xf, axis=-1, keepdims=True)
    var = jnp.mean(jnp.square(xf - mu), axis=-1, keepdims=True)
    y = (xf - mu) * lax.rsqrt(var + LN_EPS)
    return (y * g.astype(jnp.float32) + b.astype(jnp.float32)).astype(x.dtype)


def apply_rope(x):
    S = x.shape[1]
    half = HEAD_DIM // 2
    inv_freq = jnp.power(ROPE_THETA, -jnp.arange(half, dtype=jnp.float32) * (2.0 / HEAD_DIM))
    ang = jnp.arange(S, dtype=jnp.float32)[:, None] * inv_freq[None, :]
    cos = jnp.cos(ang)[None, :, None, :]
    sin = jnp.sin(ang)[None, :, None, :]
    xf = x.astype(jnp.float32)
    x1, x2 = xf[..., :half], xf[..., half:]
    return jnp.concatenate([x1 * cos - x2 * sin, x2 * cos + x1 * sin], axis=-1).astype(x.dtype)


def fox_attention(q, k, v, log_f):
    B, S, H, d = q.shape
    nb = S // Q_BLOCK
    scale = d ** -0.5
    c = jnp.cumsum(log_f, axis=1)
    c_k = jnp.transpose(c, (0, 2, 1))
    q_blocks = jnp.transpose(q.reshape(B, nb, Q_BLOCK, H, d), (1, 0, 2, 3, 4))
    c_blocks = jnp.transpose(c.reshape(B, nb, Q_BLOCK, H), (1, 0, 3, 2))
    k_pos = jnp.arange(S)

    def one_block(args):
        qi, cqi, i = args
        s = jnp.einsum('bqhd,bkhd->bhqk', qi, k, preferred_element_type=jnp.float32) * scale
        s = s + (cqi[..., :, None] - c_k[:, :, None, :])
        q_pos = i * Q_BLOCK + jnp.arange(Q_BLOCK)
        causal = k_pos[None, :] <= q_pos[:, None]
        s = jnp.where(causal[None, None], s, -jnp.inf)
        p = jax.nn.softmax(s, axis=-1)
        o = jnp.einsum('bhqk,bkhd->bqhd', p.astype(v.dtype), v, preferred_element_type=jnp.float32)
        return o.astype(q.dtype)

    out = lax.map(one_block, (q_blocks, c_blocks, jnp.arange(nb)))
    return jnp.transpose(out, (1, 0, 2, 3, 4)).reshape(B, S, H * d)


def swa_sink_attention(q, k, v, sinks):
    B, S, Hq, d = q.shape
    W = SWA_WINDOW
    nb = S // W
    scale = d ** -0.5
    qb = q.reshape(B, nb, W, SWA_KV_HEADS, SWA_GROUP, d)
    kb = k.reshape(B, nb, W, SWA_KV_HEADS, d)
    vb = v.reshape(B, nb, W, SWA_KV_HEADS, d)
    k_prev = jnp.concatenate([jnp.zeros_like(kb[:, :1]), kb[:, :-1]], axis=1)
    v_prev = jnp.concatenate([jnp.zeros_like(vb[:, :1]), vb[:, :-1]], axis=1)
    k_win = jnp.concatenate([k_prev, kb], axis=2)
    v_win = jnp.concatenate([v_prev, vb], axis=2)
    s = jnp.einsum('bnqkgd,bnskd->bnkgqs', qb, k_win, preferred_element_type=jnp.float32) * scale
    blk = jnp.arange(nb)[:, None, None]
    q_pos = blk * W + jnp.arange(W)[None, :, None]
    k_pos = (blk - 1) * W + jnp.arange(2 * W)[None, None, :]
    valid = (k_pos <= q_pos) & (k_pos > q_pos - W) & (k_pos >= 0)
    s = jnp.where(valid[None, :, None, None], s, -jnp.inf)
    sink = sinks.astype(jnp.float32).reshape(SWA_KV_HEADS, SWA_GROUP)[None, None, :, :, None, None]
    m = jnp.maximum(jnp.max(s, axis=-1, keepdims=True), sink)
    p = jnp.exp(s - m)
    denom = jnp.sum(p, axis=-1, keepdims=True) + jnp.exp(sink - m)
    o = jnp.einsum('bnkgqs,bnskd->bnqkgd', (p / denom).astype(v.dtype), v_win,
                   preferred_element_type=jnp.float32)
    return o.astype(q.dtype).reshape(B, S, Hq * d)


def mlstm_chunkwise(q, k, v, i_pre, f_pre):
    B, S, H, dk = q.shape
    dv = v.shape[-1]
    L = MLSTM_CHUNK
    nc = S // L
    qf = q.astype(jnp.float32)
    kf = k.astype(jnp.float32) * (dk ** -0.5)
    vf = v.astype(jnp.float32)
    log_f = jax.nn.log_sigmoid(f_pre)

    def to_chunks(a):
        return jnp.transpose(a.reshape(B, nc, L, H, a.shape[-1]), (1, 0, 3, 2, 4))

    def gate_chunks(a):
        return jnp.transpose(a.reshape(B, nc, L, H), (1, 0, 3, 2))

    qc, kc, vc = to_chunks(qf), to_chunks(kf), to_chunks(vf)
    ic = gate_chunks(i_pre)
    b = jnp.cumsum(gate_chunks(log_f), axis=-1)
    b_last = b[..., -1]
    tri = jnp.arange(L)[:, None] >= jnp.arange(L)[None, :]
    d_log = jnp.where(tri, b[..., :, None] - b[..., None, :] + ic[..., None, :], -jnp.inf)
    m_intra = jnp.max(d_log, axis=-1)
    g = b_last[..., None] - b + ic

    def step(carry, inp):
        Cs, ns, m = carry
        qx, kx, vx, bx, dx, mi, gx, bl = inp
        m_inter = bx + m[..., None]
        m_t = jnp.maximum(m_inter, mi)
        inter = jnp.exp(m_inter - m_t)
        w = jnp.exp(dx - m_t[..., None])
        sm = w * jnp.einsum('bhtd,bhsd->bhts', qx, kx)
        num = inter[..., None] * jnp.einsum('bhvd,bhtd->bhtv', Cs, qx) + jnp.einsum('bhts,bhsv->bhtv', sm, vx)
        dot = inter * jnp.einsum('bhd,bhtd->bht', ns, qx) + jnp.sum(sm, axis=-1)
        den = jnp.maximum(jnp.abs(dot), jnp.exp(-m_t))
        h = num / den[..., None]
        m_new = jnp.maximum(bl + m, jnp.max(gx, axis=-1))
        decay = jnp.exp(bl + m - m_new)
        wg = jnp.exp(gx - m_new[..., None])
        Cs_new = decay[..., None, None] * Cs + jnp.einsum('bhsv,bhsd->bhvd', vx * wg[..., None], kx)
        ns_new = decay[..., None] * ns + jnp.einsum('bhs,bhsd->bhd', wg, kx)
        return (Cs_new, ns_new, m_new), h

    init = (jnp.zeros((B, H, dv, dk), jnp.float32), jnp.zeros((B, H, dk), jnp.float32),
            jnp.zeros((B, H), jnp.float32))
    _, hs = lax.scan(step, init, (qc, kc, vc, b, d_log, m_intra, g, b_last))
    return jnp.transpose(hs, (1, 0, 3, 2, 4)).reshape(B, S, H * dv)


def attention_mixer(x, w_in, b_in, sinks, w_out):
    B, S, _ = x.shape
    proj = jnp.einsum('bsd,de->bse', x, w_in) + b_in
    fq, fk, fv, ff, sq, sk, sv = split_columns(proj, ATTN_SPLIT_SIZES)
    fox_out = fox_attention(fq.reshape(B, S, FOX_HEADS, HEAD_DIM), fk.reshape(B, S, FOX_HEADS, HEAD_DIM),
                            fv.reshape(B, S, FOX_HEADS, HEAD_DIM),
                            jax.nn.log_sigmoid(ff.astype(jnp.float32)))
    swa_out = swa_sink_attention(apply_rope(sq.reshape(B, S, SWA_Q_HEADS, HEAD_DIM)),
                                 apply_rope(sk.reshape(B, S, SWA_KV_HEADS, HEAD_DIM)),
                                 sv.reshape(B, S, SWA_KV_HEADS, HEAD_DIM), sinks)
    heads = jnp.concatenate([fox_out, swa_out], axis=-1)
    return jnp.einsum('bse,ed->bsd', heads, w_out)


def mlstm_mixer(x, w_in, b_in, w_out):
    B, S, _ = x.shape
    proj = jnp.einsum('bsd,de->bse', x, w_in) + b_in
    q, k, v, o, ig, fg = split_columns(proj, MLSTM_SPLIT_SIZES)
    h = mlstm_chunkwise(q.reshape(B, S, MLSTM_HEADS, MLSTM_QK_DIM), k.reshape(B, S, MLSTM_HEADS, MLSTM_QK_DIM),
                        v.reshape(B, S, MLSTM_HEADS, MLSTM_V_DIM), ig.astype(jnp.float32), fg.astype(jnp.float32))
    h = (jax.nn.sigmoid(o.astype(jnp.float32)) * h).astype(x.dtype)
    return jnp.einsum('bse,ed->bsd', h, w_out)


def conv_ffn(x, w_up, conv_w, conv_b, w_down):
    S = x.shape[1]
    u = jnp.einsum('bsd,df->bsf', x, w_up)
    u_pad = jnp.pad(u, ((0, 0), (CONV_WIDTH - 1, 0), (0, 0)))
    c = conv_b + sum(conv_w[j] * u_pad[:, j:j + S] for j in range(CONV_WIDTH))
    gate, val = c[..., :D_FF], c[..., D_FF:]
    return jnp.einsum('bsf,fd->bsd', jax.nn.silu(gate) * val, w_down)


def setup_inputs(seed: int = 0) -> dict:
    key = jax.random.key(seed)
    ks = jax.random.split(key, 20)
    f32 = jnp.float32

    def nrm(k, shape, scale):
        return jax.random.normal(k, shape, f32) * scale

    x = nrm(ks[0], (BATCH, SEQ, D_MODEL), 1.0)
    attn_w_in = nrm(ks[1], (N_EVEN, D_MODEL, ATTN_IN_DIM), D_MODEL ** -0.5)
    attn_b_in = nrm(ks[2], (N_EVEN, ATTN_IN_DIM), 0.02).at[:, FOX_F_OFF:FOX_F_OFF + FOX_HEADS].add(FORGET_BIAS_INIT)
    attn_sinks = nrm(ks[3], (N_EVEN, SWA_Q_HEADS), 0.5)
    attn_w_out = nrm(ks[4], (N_EVEN, ATTN_OUT_DIM, D_MODEL), ATTN_OUT_DIM ** -0.5 * DEEPNORM_BETA)
    mlstm_w_in = nrm(ks[5], (N_ODD, D_MODEL, MLSTM_IN_DIM), D_MODEL ** -0.5)
    mlstm_b_in = nrm(ks[6], (N_ODD, MLSTM_IN_DIM), 0.02).at[:, MLSTM_F_OFF:MLSTM_F_OFF + MLSTM_HEADS].add(FORGET_BIAS_INIT)
    mlstm_w_out = nrm(ks[7], (N_ODD, MLSTM_V_TOT, D_MODEL), MLSTM_V_TOT ** -0.5 * DEEPNORM_BETA)
    ffn_w_up = nrm(ks[8], (DEPTH, D_MODEL, 2 * D_FF), D_MODEL ** -0.5)
    ffn_conv_w = nrm(ks[9], (DEPTH, CONV_WIDTH, 2 * D_FF), CONV_WIDTH ** -0.5)
    ffn_conv_b = nrm(ks[10], (DEPTH, 2 * D_FF), 0.02)
    ffn_w_down = nrm(ks[11], (DEPTH, D_FF, D_MODEL), D_FF ** -0.5 * DEEPNORM_BETA)
    ln1_g = 1.0 + nrm(ks[12], (DEPTH, D_MODEL), 0.02)
    ln1_b = nrm(ks[13], (DEPTH, D_MODEL), 0.02)
    ln2_g = 1.0 + nrm(ks[14], (DEPTH, D_MODEL), 0.02)
    ln2_b = nrm(ks[15], (DEPTH, D_MODEL), 0.02)
    return {"x": x, "attn_w_in": attn_w_in, "attn_b_in": attn_b_in, "attn_sinks": attn_sinks,
            "attn_w_out": attn_w_out, "mlstm_w_in": mlstm_w_in, "mlstm_b_in": mlstm_b_in,
            "mlstm_w_out": mlstm_w_out, "ffn_w_up": ffn_w_up, "ffn_conv_w": ffn_conv_w,
            "ffn_conv_b": ffn_conv_b, "ffn_w_down": ffn_w_down, "ln1_g": ln1_g, "ln1_b": ln1_b,
            "ln2_g": ln2_g, "ln2_b": ln2_b}


def reference(x, attn_w_in, attn_b_in, attn_sinks, attn_w_out, mlstm_w_in, mlstm_b_in, mlstm_w_out,
              ffn_w_up, ffn_conv_w, ffn_conv_b, ffn_w_down, ln1_g, ln1_b, ln2_g, ln2_b):
    for layer in range(DEPTH):
        j = layer // 2
        if layer % 2 == 0:
            y = attention_mixer(x, attn_w_in[j], attn_b_in[j], attn_sinks[j], attn_w_out[j])
        else:
            y = mlstm_mixer(x, mlstm_w_in[j], mlstm_b_in[j], mlstm_w_out[j])
        x = layer_norm(DEEPNORM_ALPHA * x + y, ln1_g[layer], ln1_b[layer])
        y = conv_ffn(x, ffn_w_up[layer], ffn_conv_w[layer], ffn_conv_b[layer], ffn_w_down[layer])
        x = layer_norm(DEEPNORM_ALPHA * x + y, ln2_g[layer], ln2_b[layer])
    return x
```

```python
import functools

import jax
import jax.numpy as jnp
from jax import lax
from jax.experimental import pallas as pl
from jax.experimental.pallas import tpu as pltpu

F32 = jnp.float32
BF16 = jnp.bfloat16

HEAD_DIM = 128
SWA_KV_HEADS = 2
SWA_WINDOW = 128
ROPE_THETA = 10000.0
MLSTM_HEADS = 8
CONV_WIDTH = 3
LN_EPS = 1e-5

LANES = 128
SUBLANES = 8
GATE_ROWS = 16
HEAD_ROWS = 16
VMEM_LIMIT_BYTES = 56 * 1024 * 1024

PROJ_BM = 1024
PROJ_BN = 1536
OUT_BM = 512
FFN_BM = 512
FFN_BF = 512
FOX_T = 512
SWA_T = 512
CUM_T = 512
MLSTM_L = 256

NT_DIMS = (((1,), (1,)), ((), ()))
TN_DIMS = (((0,), (0,)), ((), ()))


def _tile(n, pref, unit):
    t = min(pref, n)
    while n % t or t % unit:
        t -= unit
    assert t > 0, (n, pref, unit)
    return t


def _params(sem):
    return pltpu.CompilerParams(dimension_semantics=sem, vmem_limit_bytes=VMEM_LIMIT_BYTES)


def _log_sigmoid(x):
    return jnp.minimum(x, 0.0) - jnp.log(1.0 + jnp.exp(-jnp.abs(x)))


def _sigmoid(x):
    return 1.0 / (1.0 + jnp.exp(-x))


def _split3(x):
    h1 = x.astype(BF16)
    r1 = x - h1.astype(F32)
    h2 = r1.astype(BF16)
    h3 = (r1 - h2.astype(F32)).astype(BF16)
    return h1, h2, h3


def _cumsum_rows(tri, x):
    return sum(jnp.dot(tri, h, preferred_element_type=F32) for h in _split3(x))


def _cumsum_lanes(x, tri_t):
    return sum(jnp.dot(h, tri_t, preferred_element_type=F32) for h in _split3(x))


def _layer_norm(z, g, b):
    mu = jnp.mean(z, axis=-1, keepdims=True)
    zc = z - mu
    var = jnp.mean(zc * zc, axis=-1, keepdims=True)
    return zc * lax.rsqrt(var + LN_EPS) * g + b


def _proj_kernel(x_ref, w_ref, b_ref, wg_ref, bg_ref, wgt_ref, bgt_ref, o_ref, g_ref, gt_ref, xb_ref):
    @pl.when(pl.program_id(1) == 0)
    def _():
        xb = x_ref[...].astype(BF16)
        xb_ref[...] = xb
        g_ref[...] = jnp.dot(xb, wg_ref[...], preferred_element_type=F32) + bg_ref[...]
        gt_ref[...] = lax.dot_general(wgt_ref[...], xb, NT_DIMS, preferred_element_type=F32) + bgt_ref[...]

    acc = jnp.dot(xb_ref[...], w_ref[...], preferred_element_type=F32)
    o_ref[...] = (acc + b_ref[...]).astype(o_ref.dtype)


def _in_projection(x, w, b, wg, bg, wgt, bgt):
    m, d = x.shape
    n = w.shape[1]
    bm = _tile(m, PROJ_BM, LANES)
    bn = _tile(n, PROJ_BN, LANES)
    return pl.pallas_call(
        _proj_kernel,
        grid=(m // bm, n // bn),
        in_specs=[
            pl.BlockSpec((bm, d), lambda i, j: (i, 0)),
            pl.BlockSpec((d, bn), lambda i, j: (0, j)),
            pl.BlockSpec((1, bn), lambda i, j: (0, j)),
            pl.BlockSpec((d, LANES), lambda i, j: (0, 0)),
            pl.BlockSpec((1, LANES), lambda i, j: (0, 0)),
            pl.BlockSpec((GATE_ROWS, d), lambda i, j: (0, 0)),
            pl.BlockSpec((GATE_ROWS, 1), lambda i, j: (0, 0)),
        ],
        out_specs=[
            pl.BlockSpec((bm, bn), lambda i, j: (i, j)),
            pl.BlockSpec((bm, LANES), lambda i, j: (i, 0)),
            pl.BlockSpec((GATE_ROWS, bm), lambda i, j: (0, i)),
        ],
        out_shape=[
            jax.ShapeDtypeStruct((m, n), BF16),
            jax.ShapeDtypeStruct((m, LANES), F32),
            jax.ShapeDtypeStruct((GATE_ROWS, m), F32),
        ],
        scratch_shapes=[pltpu.VMEM((bm, d), BF16)],
        compiler_params=_params(("arbitrary", "arbitrary")),
        name="in_projection",
    )(x, w, b, wg, bg, wgt, bgt)


def _fox_gate_kernel(g_ref, gt_ref, ccol_ref, crow_ref, carry_col_ref, carry_row_ref):
    @pl.when(pl.program_id(1) == 0)
    def _():
        carry_col_ref[...] = jnp.zeros_like(carry_col_ref)
        carry_row_ref[...] = jnp.zeros_like(carry_row_ref)

    t = g_ref.shape[0]
    row = lax.broadcasted_iota(jnp.int32, (t, t), 0)
    col = lax.broadcasted_iota(jnp.int32, (t, t), 1)
    tri = (col <= row).astype(BF16)
    tri_t = (row <= col).astype(BF16)
    c_col = _cumsum_rows(tri, _log_sigmoid(g_ref[...])) + carry_col_ref[...]
    c_row = _cumsum_lanes(_log_sigmoid(gt_ref[...]), tri_t) + carry_row_ref[...]
    ccol_ref[...] = c_col
    crow_ref[...] = c_row
    carry_col_ref[...] = c_col[t - 1:t, :]
    carry_row_ref[...] = c_row[:, t - 1:t]


def _fox_gate_cumsum(g, gt, batch, seq):
    t = _tile(seq, CUM_T, LANES)
    ns = seq // t
    return pl.pallas_call(
        _fox_gate_kernel,
        grid=(batch, ns),
        in_specs=[
            pl.BlockSpec((t, LANES), lambda b, s: (b * ns + s, 0)),
            pl.BlockSpec((GATE_ROWS, t), lambda b, s: (0, b * ns + s)),
        ],
        out_specs=[
            pl.BlockSpec((t, LANES), lambda b, s: (b * ns + s, 0)),
            pl.BlockSpec((None, GATE_ROWS, t), lambda b, s: (b, 0, s)),
        ],
        out_shape=[
            jax.ShapeDtypeStruct((batch * seq, LANES), F32),
            jax.ShapeDtypeStruct((batch, GATE_ROWS, seq), F32),
        ],
        scratch_shapes=[pltpu.VMEM((1, LANES), F32), pltpu.VMEM((GATE_ROWS, 1), F32)],
        compiler_params=_params(("arbitrary", "arbitrary")),
        name="fox_gate_cumsum",
    )(g, gt)


def _fox_kernel(q_ref, k_ref, v_ref, ccol_ref, crow_ref, o_ref, *, scale):
    head = pl.program_id(1)
    qi = pl.program_id(2)
    t = q_ref.shape[0]
    q = q_ref[...]
    lane = lax.broadcasted_iota(jnp.int32, (t, LANES), 1)
    cq = jnp.sum(jnp.where(lane == head, ccol_ref[...], 0.0), axis=-1, keepdims=True)

    def tile(ki, carry, diagonal):
        m, l, acc = carry
        start = pl.multiple_of(ki * t, t)
        k = k_ref[pl.ds(start, t), :]
        v = v_ref[pl.ds(start, t), :]
        s = lax.dot_general(q, k, NT_DIMS, preferred_element_type=F32) * scale
        s = s + (cq - crow_ref[ki])
        if diagonal:
            row = lax.broadcasted_iota(jnp.int32, (t, t), 0)
            col = lax.broadcasted_iota(jnp.int32, (t, t), 1)
            s = jnp.where(col <= row, s, -jnp.inf)
        m_new = jnp.maximum(m, jnp.max(s, axis=-1, keepdims=True))
        alpha = jnp.exp(m - m_new)
        p = jnp.exp(s - m_new)
        l = alpha * l + jnp.sum(p, axis=-1, keepdims=True)
        acc = alpha * acc + jnp.dot(p.astype(BF16), v, preferred_element_type=F32)
        return m_new, l, acc

    init = (jnp.full((t, 1), -jnp.inf, F32), jnp.zeros((t, 1), F32), jnp.zeros((t, HEAD_DIM), F32))
    carry = lax.fori_loop(0, qi, lambda ki, c: tile(ki, c, False), init)
    _, l, acc = tile(qi, carry, True)
    o_ref[...] = (acc / l).astype(o_ref.dtype)


def _fox_attention(proj, c_col, c_row, batch, seq, heads):
    t = _tile(seq, FOX_T, LANES)
    nq = seq // t
    c_row = c_row.reshape(batch, GATE_ROWS, nq, 1, t)
    return pl.pallas_call(
        functools.partial(_fox_kernel, scale=HEAD_DIM ** -0.5),
        grid=(batch, heads, nq),
        in_specs=[
            pl.BlockSpec((t, HEAD_DIM), lambda b, h, i: (b * nq + i, h)),
            pl.BlockSpec((seq, HEAD_DIM), lambda b, h, i: (b, heads + h)),
            pl.BlockSpec((seq, HEAD_DIM), lambda b, h, i: (b, 2 * heads + h)),
            pl.BlockSpec((t, LANES), lambda b, h, i: (b * nq + i, 0)),
            pl.BlockSpec((None, None, nq, 1, t), lambda b, h, i: (b, h, 0, 0, 0)),
        ],
        out_specs=pl.BlockSpec((t, HEAD_DIM), lambda b, h, i: (b * nq + i, h)),
        out_shape=jax.ShapeDtypeStruct((batch * seq, heads * HEAD_DIM), BF16),
        compiler_params=_params(("arbitrary", "arbitrary", "arbitrary")),
        name="fox_attention",
    )(proj, proj, proj, c_col, c_row)


def _rope(x, cos, sin_signed):
    xf = x.astype(F32)
    return xf * cos + pltpu.roll(xf, HEAD_DIM // 2, 1) * sin_signed


def _swa_kernel(sink_ref, q_ref, k_ref, v_ref, kp_ref, vp_ref, cos_ref, sin_ref, cosp_ref, sinp_ref, o_ref,
                *, scale, group):
    w = SWA_WINDOW
    first_visible = jnp.where(pl.program_id(1) == 0, w, 0)
    nblk = q_ref.shape[0] // w
    cos, sin = cos_ref[...], sin_ref[...]
    cosp, sinp = cosp_ref[...], sinp_ref[...]
    row = lax.broadcasted_iota(jnp.int32, (w, 2 * w), 0)
    col = lax.broadcasted_iota(jnp.int32, (w, 2 * w), 1)
    valid = jnp.logical_and(col > row, col - w <= row)
    valid_first = jnp.logical_and(valid, col >= first_visible)
    for kv in range(SWA_KV_HEADS):
        ksl = slice(kv * HEAD_DIM, (kv + 1) * HEAD_DIM)
        k_all = jnp.concatenate([_rope(kp_ref[:, ksl], cosp, sinp), _rope(k_ref[:, ksl], cos, sin)],
                                axis=0).astype(BF16)
        v_all = jnp.concatenate([vp_ref[:, ksl], v_ref[:, ksl]], axis=0)
        for g in range(group):
            hq = kv * group + g
            qsl = slice(hq * HEAD_DIM, (hq + 1) * HEAD_DIM)
            q = _rope(q_ref[:, qsl], cos, sin).astype(BF16)
            sink = sink_ref[hq]
            for blk in range(nblk):
                qb = q[blk * w:(blk + 1) * w, :]
                kb = k_all[blk * w:(blk + 2) * w, :]
                vb = v_all[blk * w:(blk + 2) * w, :]
                s = lax.dot_general(qb, kb, NT_DIMS, preferred_element_type=F32) * scale
                s = jnp.where(valid_first if blk == 0 else valid, s, -jnp.inf)
                m = jnp.maximum(jnp.max(s, axis=-1, keepdims=True), sink)
                p = jnp.exp(s - m)
                denom = jnp.sum(p, axis=-1, keepdims=True) + jnp.exp(sink - m)
                o = jnp.dot((p / denom).astype(BF16), vb, preferred_element_type=F32)
                o_ref[blk * w:(blk + 1) * w, qsl] = o.astype(o_ref.dtype)


def _swa_attention(proj, sinks, cos, sin_signed, batch, seq, q_col, k_col, v_col, q_heads):
    t = _tile(seq, SWA_T, SWA_WINDOW)
    nt = seq // t
    per = t // SWA_WINDOW
    qw = q_heads * HEAD_DIM
    kvw = SWA_KV_HEADS * HEAD_DIM
    assert q_col % qw == 0 and k_col % kvw == 0 and v_col % kvw == 0

    def prev(b, i):
        return jnp.maximum((b * nt + i) * per - 1, 0)

    def prev_pos(i):
        return jnp.maximum(i * per - 1, 0)

    return pl.pallas_call(
        functools.partial(_swa_kernel, scale=HEAD_DIM ** -0.5, group=q_heads // SWA_KV_HEADS),
        grid=(batch, nt),
        in_specs=[
            pl.BlockSpec(memory_space=pltpu.SMEM),
            pl.BlockSpec((t, qw), lambda b, i: (b * nt + i, q_col // qw)),
            pl.BlockSpec((t, kvw), lambda b, i: (b * nt + i, k_col // kvw)),
            pl.BlockSpec((t, kvw), lambda b, i: (b * nt + i, v_col // kvw)),
            pl.BlockSpec((SWA_WINDOW, kvw), lambda b, i: (prev(b, i), k_col // kvw)),
            pl.BlockSpec((SWA_WINDOW, kvw), lambda b, i: (prev(b, i), v_col // kvw)),
            pl.BlockSpec((t, HEAD_DIM), lambda b, i: (i, 0)),
            pl.BlockSpec((t, HEAD_DIM), lambda b, i: (i, 0)),
            pl.BlockSpec((SWA_WINDOW, HEAD_DIM), lambda b, i: (prev_pos(i), 0)),
            pl.BlockSpec((SWA_WINDOW, HEAD_DIM), lambda b, i: (prev_pos(i), 0)),
        ],
        out_specs=pl.BlockSpec((t, qw), lambda b, i: (b * nt + i, 0)),
        out_shape=jax.ShapeDtypeStruct((batch * seq, qw), BF16),
        compiler_params=_params(("arbitrary", "arbitrary")),
        name="swa_attention",
    )(sinks, proj, proj, proj, proj, proj, cos, sin_signed, cos, sin_signed)


def _mlstm_kernel(q_ref, k_ref, v_ref, og_ref, g_ref, gt_ref, o_ref, state_ref, m_ref, *, dk, dv):
    heads = MLSTM_HEADS

    @pl.when(pl.program_id(1) == 0)
    def _():
        state_ref[...] = jnp.zeros_like(state_ref)
        m_ref[...] = jnp.zeros_like(m_ref)

    n = q_ref.shape[0]
    g = g_ref[...]
    gt = gt_ref[...]
    row = lax.broadcasted_iota(jnp.int32, (n, n), 0)
    col = lax.broadcasted_iota(jnp.int32, (n, n), 1)
    causal = col <= row
    b_col_all = _cumsum_rows(causal.astype(BF16), _log_sigmoid(g))
    b_row_all = _cumsum_lanes(_log_sigmoid(gt), (row <= col).astype(BF16))
    unit = (lax.broadcasted_iota(jnp.int32, (n, LANES), 1) == 0).astype(BF16)

    for h in range(heads):
        i_col = g[:, h:h + 1]
        b_col = b_col_all[:, heads + h:heads + h + 1]
        i_row = gt[h:h + 1, :]
        b_row = b_row_all[heads + h:heads + h + 1, :]
        b_last = b_row[:, n - 1:n]
        m_prev = m_ref[h]

        d = jnp.where(causal, (b_col - b_row) + i_row, -jnp.inf)
        m_inter = b_col + m_prev
        m_t = jnp.maximum(m_inter, jnp.max(d, axis=-1, keepdims=True))
        inter = jnp.exp(m_inter - m_t)
        wgt = jnp.exp(d - m_t)

        qh = q_ref[:, h * dk:(h + 1) * dk]
        kh = (k_ref[:, h * dk:(h + 1) * dk].astype(F32) * (dk ** -0.5)).astype(BF16)
        v_ext = jnp.concatenate([v_ref[:, h * dv:(h + 1) * dv], unit], axis=1)
        state = state_ref[h]

        sm = wgt * lax.dot_general(qh, kh, NT_DIMS, preferred_element_type=F32)
        tot = inter * jnp.dot(qh, state.astype(BF16), preferred_element_type=F32)
        tot = tot + jnp.dot(sm.astype(BF16), v_ext, preferred_element_type=F32)
        den = jnp.maximum(jnp.abs(tot[:, dv:dv + 1]), jnp.exp(-m_t))
        hid = tot[:, :dv] / den
        gate = _sigmoid(og_ref[:, h * dv:(h + 1) * dv].astype(F32))
        o_ref[:, h * dv:(h + 1) * dv] = (gate * hid).astype(o_ref.dtype)

        m_new = jnp.maximum(b_last + m_prev, jnp.max((b_last - b_row) + i_row, axis=-1, keepdims=True))
        decay = jnp.exp(b_last + m_prev - m_new)
        w_end = jnp.exp(((b_last - b_col) + i_col) - m_new)
        vw = (v_ext.astype(F32) * w_end).astype(BF16)
        state_ref[h] = decay * state + lax.dot_general(kh, vw, TN_DIMS, preferred_element_type=F32)
        m_ref[h] = m_new


def _mlstm(proj, g, gt, batch, seq, dk, dv):
    heads = MLSTM_HEADS
    n = _tile(seq, MLSTM_L, LANES)
    nc = seq // n
    qk, vw = heads * dk, heads * dv
    assert vw % qk == 0
    r = vw // qk
    return pl.pallas_call(
        functools.partial(_mlstm_kernel, dk=dk, dv=dv),
        grid=(batch, nc),
        in_specs=[
            pl.BlockSpec((n, qk), lambda b, c: (b * nc + c, 0)),
            pl.BlockSpec((n, qk), lambda b, c: (b * nc + c, 1)),
            pl.BlockSpec((n, vw), lambda b, c: (b * nc + c, 2 // r)),
            pl.BlockSpec((n, vw), lambda b, c: (b * nc + c, 2 // r + 1)),
            pl.BlockSpec((n, LANES), lambda b, c: (b * nc + c, 0)),
            pl.BlockSpec((GATE_ROWS, n), lambda b, c: (0, b * nc + c)),
        ],
        out_specs=pl.BlockSpec((n, vw), lambda b, c: (b * nc + c, 0)),
        out_shape=jax.ShapeDtypeStruct((batch * seq, vw), BF16),
        scratch_shapes=[pltpu.VMEM((heads, dk, dv + LANES), F32), pltpu.VMEM((heads, 1, 1), F32)],
        compiler_params=_params(("arbitrary", "arbitrary")),
        name="mlstm",
    )(proj, proj, proj, proj, g, gt)


def _outproj_ln_kernel(*refs, alpha, n_in):
    h_refs, w_refs = refs[:n_in], refs[n_in:2 * n_in]
    x_ref, g_ref, b_ref, o_ref = refs[2 * n_in:]
    y = sum(jnp.dot(h[...], w[...], preferred_element_type=F32) for h, w in zip(h_refs, w_refs))
    o_ref[...] = _layer_norm(alpha * x_ref[...] + y, g_ref[...], b_ref[...])


def _out_projection_ln(hs, ws, x, ln_g, ln_b, alpha):
    m, d = x.shape
    bm = _tile(m, OUT_BM, LANES)
    in_specs = [pl.BlockSpec((bm, h.shape[1]), lambda i: (i, 0)) for h in hs]
    in_specs += [pl.BlockSpec(w.shape, lambda i: (0, 0)) for w in ws]
    in_specs += [pl.BlockSpec((bm, d), lambda i: (i, 0)),
                 pl.BlockSpec((1, d), lambda i: (0, 0)),
                 pl.BlockSpec((1, d), lambda i: (0, 0))]
    return pl.pallas_call(
        functools.partial(_outproj_ln_kernel, alpha=alpha, n_in=len(hs)),
        grid=(m // bm,),
        in_specs=in_specs,
        out_specs=pl.BlockSpec((bm, d), lambda i: (i, 0)),
        out_shape=jax.ShapeDtypeStruct((m, d), F32),
        compiler_params=_params(("arbitrary",)),
        name="out_projection_ln",
    )(*hs, *ws, x, ln_g, ln_b)


def _causal_conv(u, prev, w_ref, b_ref):
    w0, w1, w2 = w_ref[0:1, :], w_ref[1:2, :], w_ref[2:3, :]
    body = b_ref[...] + w0 * pltpu.roll(u, 2, 0) + w1 * pltpu.roll(u, 1, 0) + w2 * u
    top = jnp.concatenate([prev, u[:HEAD_ROWS, :]], axis=0)
    head = b_ref[...] + w0 * pltpu.roll(top, 2, 0) + w1 * pltpu.roll(top, 1, 0) + w2 * top
    return body, head[SUBLANES:, :]


def _ffn_kernel(x_ref, wg_ref, wv_ref, cwg_ref, cwv_ref, cbg_ref, cbv_ref, wd_ref, lg_ref, lb_ref, o_ref,
                xb_ref, acc_ref, h_ref, pg_ref, pv_ref, *, alpha, tiles_per_seq):
    i = pl.program_id(0)
    j = pl.program_id(1)
    bm = x_ref.shape[0]

    @pl.when(j == 0)
    def _():
        xb_ref[...] = x_ref[...].astype(BF16)
        acc_ref[...] = jnp.zeros_like(acc_ref)

    xb = xb_ref[...]
    ug = jnp.dot(xb, wg_ref[...], preferred_element_type=F32)
    uv = jnp.dot(xb, wv_ref[...], preferred_element_type=F32)
    kept_rows = jnp.where(i % tiles_per_seq != 0, SUBLANES, 0)
    keep = lax.broadcasted_iota(jnp.int32, pg_ref.shape[1:], 0) < kept_rows
    cg, cg_top = _causal_conv(ug, jnp.where(keep, pg_ref[j], 0.0), cwg_ref, cbg_ref)
    cv, cv_top = _causal_conv(uv, jnp.where(keep, pv_ref[j], 0.0), cwv_ref, cbv_ref)
    pg_ref[j] = ug[bm - SUBLANES:, :]
    pv_ref[j] = uv[bm - SUBLANES:, :]
    h_ref[...] = (cg * _sigmoid(cg) * cv).astype(BF16)
    h_ref[0:HEAD_ROWS, :] = (cg_top * _sigmoid(cg_top) * cv_top).astype(BF16)
    acc_ref[...] += jnp.dot(h_ref[...], wd_ref[...], preferred_element_type=F32)

    @pl.when(j == pl.num_programs(1) - 1)
    def _():
        o_ref[...] = _layer_norm(alpha * x_ref[...] + acc_ref[...], lg_ref[...], lb_ref[...])


def _conv_ffn_ln(x, w_up, conv_w, conv_b, w_down, ln_g, ln_b, alpha, seq):
    m, d = x.shape
    f = w_down.shape[0]
    bm = _tile(seq, FFN_BM, 2 * SUBLANES)
    bf = _tile(f, FFN_BF, LANES)
    nf = f // bf
    return pl.pallas_call(
        functools.partial(_ffn_kernel, alpha=alpha, tiles_per_seq=seq // bm),
        grid=(m // bm, nf),
        in_specs=[
            pl.BlockSpec((bm, d), lambda i, j: (i, 0)),
            pl.BlockSpec((d, bf), lambda i, j: (0, j)),
            pl.BlockSpec((d, bf), lambda i, j: (0, nf + j)),
            pl.BlockSpec((CONV_WIDTH, bf), lambda i, j: (0, j)),
            pl.BlockSpec((CONV_WIDTH, bf), lambda i, j: (0, nf + j)),
            pl.BlockSpec((1, bf), lambda i, j: (0, j)),
            pl.BlockSpec((1, bf), lambda i, j: (0, nf + j)),
            pl.BlockSpec((bf, d), lambda i, j: (j, 0)),
            pl.BlockSpec((1, d), lambda i, j: (0, 0)),
            pl.BlockSpec((1, d), lambda i, j: (0, 0)),
        ],
        out_specs=pl.BlockSpec((bm, d), lambda i, j: (i, 0)),
        out_shape=jax.ShapeDtypeStruct((m, d), F32),
        scratch_shapes=[
            pltpu.VMEM((bm, d), BF16),
            pltpu.VMEM((bm, d), F32),
            pltpu.VMEM((bm, bf), BF16),
            pltpu.VMEM((nf, SUBLANES, bf), F32),
            pltpu.VMEM((nf, SUBLANES, bf), F32),
        ],
        compiler_params=_params(("arbitrary", "arbitrary")),
        name="conv_ffn_ln",
    )(x, w_up, w_up, conv_w, conv_w, conv_b, conv_b, w_down, ln_g, ln_b)


def _gate_params(w, b):
    d, n = w.shape
    wg = jnp.zeros((d, LANES), BF16).at[:, :n].set(w.astype(BF16))
    bg = jnp.zeros((1, LANES), F32).at[0, :n].set(b)
    wgt = jnp.zeros((GATE_ROWS, d), BF16).at[:n, :].set(w.T.astype(BF16))
    bgt = jnp.zeros((GATE_ROWS, 1), F32).at[:n, 0].set(b)
    return wg, bg, wgt, bgt


def _rope_tables(seq):
    half = HEAD_DIM // 2
    inv_freq = jnp.power(ROPE_THETA, -jnp.arange(half, dtype=F32) * (2.0 / HEAD_DIM))
    ang = jnp.arange(seq, dtype=F32)[:, None] * inv_freq[None, :]
    cos, sin = jnp.cos(ang), jnp.sin(ang)
    return jnp.concatenate([cos, cos], axis=-1), jnp.concatenate([-sin, sin], axis=-1)


def kernel(x, attn_w_in, attn_b_in, attn_sinks, attn_w_out, mlstm_w_in, mlstm_b_in, mlstm_w_out, ffn_w_up,
           ffn_conv_w, ffn_conv_b, ffn_w_down, ln1_g, ln1_b, ln2_g, ln2_b):
    batch, seq, d = x.shape
    depth = ln1_g.shape[0]
    alpha = float((2 * depth) ** 0.25)
    fox_heads = d // (2 * HEAD_DIM)
    swa_heads = d // (2 * HEAD_DIM)
    fox_dim = fox_heads * HEAD_DIM
    fox_f_off = 3 * fox_dim
    dk, dv = d // (2 * MLSTM_HEADS), d // MLSTM_HEADS
    mlstm_main = 2 * MLSTM_HEADS * dk + 2 * MLSTM_HEADS * dv
    cos, sin_signed = _rope_tables(seq)

    h = x.reshape(batch * seq, d)
    for layer in range(depth):
        j = layer // 2
        if layer % 2 == 0:
            w_in, b_in = attn_w_in[j], attn_b_in[j]
            w_main = jnp.concatenate([w_in[:, :fox_f_off], w_in[:, fox_f_off + fox_heads:]], axis=1).astype(BF16)
            b_main = jnp.concatenate([b_in[:fox_f_off], b_in[fox_f_off + fox_heads:]])[None, :]
            gates = _gate_params(w_in[:, fox_f_off:fox_f_off + fox_heads], b_in[fox_f_off:fox_f_off + fox_heads])
            proj, g, gt = _in_projection(h, w_main, b_main, *gates)
            c_col, c_row = _fox_gate_cumsum(g, gt, batch, seq)
            fox = _fox_attention(proj, c_col, c_row, batch, seq, fox_heads)
            swa_q = 3 * fox_dim
            swa_k = swa_q + swa_heads * HEAD_DIM
            swa_v = swa_k + SWA_KV_HEADS * HEAD_DIM
            swa = _swa_attention(proj, attn_sinks[j], cos, sin_signed, batch, seq, swa_q, swa_k, swa_v, swa_heads)
            w_out = attn_w_out[j].astype(BF16)
            h = _out_projection_ln([fox, swa], [w_out[:fox_dim], w_out[fox_dim:]], h,
                                   ln1_g[layer][None, :], ln1_b[layer][None, :], alpha)
        else:
            w_in, b_in = mlstm_w_in[j], mlstm_b_in[j]
            gates = _gate_params(w_in[:, mlstm_main:], b_in[mlstm_main:])
            proj, g, gt = _in_projection(h, w_in[:, :mlstm_main].astype(BF16), b_in[None, :mlstm_main], *gates)
            mixed = _mlstm(proj, g, gt, batch, seq, dk, dv)
            h = _out_projection_ln([mixed], [mlstm_w_out[j].astype(BF16)], h,
                                   ln1_g[layer][None, :], ln1_b[layer][None, :], alpha)
        h = _conv_ffn_ln(h, ffn_w_up[layer].astype(BF16), ffn_conv_w[layer], ffn_conv_b[layer][None, :],
                         ffn_w_down[layer].astype(BF16), ln2_g[layer][None, :], ln2_b[layer][None, :], alpha, seq)
    return h.reshape(batch, seq, d)
```

```python
import functools

import jax
import jax.numpy as jnp
from jax import lax
from jax.experimental import pallas as pl
from jax.experimental.pallas import tpu as pltpu

F32 = jnp.float32
BF16 = jnp.bfloat16

HEAD_DIM = 128
SWA_KV_HEADS = 2
SWA_WINDOW = 128
ROPE_THETA = 10000.0
MLSTM_HEADS = 8
CONV_WIDTH = 3
LN_EPS = 1e-5
LOG2_E = 1.4426950408889634

LANES = 128
SUBLANES = 8
GATE_ROWS = 16
HEAD_ROWS = 16
VMEM_LIMIT_BYTES = 56 * 1024 * 1024

PROJ_BM = 1024
PROJ_BN = 1536
OUT_BM = 512
FFN_BM = 512
FFN_BF = 512
FFN_CHUNK = 256
FFN_ROWS = 256
FFN_SLOTS = 2
FOX_T = 512
FOX_GROUP = 2
SWA_T = 512
CUM_T = 512
MLSTM_L = 256

NT_DIMS = (((1,), (1,)), ((), ()))
TN_DIMS = (((0,), (0,)), ((), ()))


def _tile(n, pref, unit):
    t = min(pref, n)
    while n % t or t % unit:
        t -= unit
    assert t > 0, (n, pref, unit)
    return t


def _params(sem):
    return pltpu.CompilerParams(dimension_semantics=sem, vmem_limit_bytes=VMEM_LIMIT_BYTES)


def _log_sigmoid(x):
    return jnp.minimum(x, 0.0) - jnp.log(1.0 + jnp.exp(-jnp.abs(x)))


def _sigmoid(x):
    return 1.0 / (1.0 + jnp.exp(-x))


def _split3(x):
    h1 = x.astype(BF16)
    r1 = x - h1.astype(F32)
    h2 = r1.astype(BF16)
    h3 = (r1 - h2.astype(F32)).astype(BF16)
    return h1, h2, h3


def _cumsum_rows(tri, x):
    return sum(jnp.dot(tri, h, preferred_element_type=F32) for h in _split3(x))


def _cumsum_lanes(x, tri_t):
    return sum(jnp.dot(h, tri_t, preferred_element_type=F32) for h in _split3(x))


def _layer_norm(z, g, b):
    mu = jnp.mean(z, axis=-1, keepdims=True)
    zc = z - mu
    var = jnp.mean(zc * zc, axis=-1, keepdims=True)
    return zc * lax.rsqrt(var + LN_EPS) * g + b


def _proj_kernel(x_ref, w_ref, b_ref, wg_ref, bg_ref, wgt_ref, bgt_ref, o_ref, g_ref, gt_ref, xb_ref):
    @pl.when(pl.program_id(1) == 0)
    def _():
        xb = x_ref[...].astype(BF16)
        xb_ref[...] = xb
        g_ref[...] = jnp.dot(xb, wg_ref[...], preferred_element_type=F32) + bg_ref[...]
        gt_ref[...] = lax.dot_general(wgt_ref[...], xb, NT_DIMS, preferred_element_type=F32) + bgt_ref[...]

    acc = jnp.dot(xb_ref[...], w_ref[...], preferred_element_type=F32)
    o_ref[...] = (acc + b_ref[...]).astype(o_ref.dtype)


def _in_projection(x, w, b, wg, bg, wgt, bgt):
    m, d = x.shape
    n = w.shape[1]
    bm = _tile(m, PROJ_BM, LANES)
    bn = _tile(n, PROJ_BN, LANES)
    return pl.pallas_call(
        _proj_kernel,
        grid=(m // bm, n // bn),
        in_specs=[
            pl.BlockSpec((bm, d), lambda i, j: (i, 0)),
            pl.BlockSpec((d, bn), lambda i, j: (0, j)),
            pl.BlockSpec((1, bn), lambda i, j: (0, j)),
            pl.BlockSpec((d, LANES), lambda i, j: (0, 0)),
            pl.BlockSpec((1, LANES), lambda i, j: (0, 0)),
            pl.BlockSpec((GATE_ROWS, d), lambda i, j: (0, 0)),
            pl.BlockSpec((GATE_ROWS, 1), lambda i, j: (0, 0)),
        ],
        out_specs=[
            pl.BlockSpec((bm, bn), lambda i, j: (i, j)),
            pl.BlockSpec((bm, LANES), lambda i, j: (i, 0)),
            pl.BlockSpec((GATE_ROWS, bm), lambda i, j: (0, i)),
        ],
        out_shape=[
            jax.ShapeDtypeStruct((m, n), BF16),
            jax.ShapeDtypeStruct((m, LANES), F32),
            jax.ShapeDtypeStruct((GATE_ROWS, m), F32),
        ],
        scratch_shapes=[pltpu.VMEM((bm, d), BF16)],
        compiler_params=_params(("arbitrary", "arbitrary")),
        name="in_projection",
    )(x, w, b, wg, bg, wgt, bgt)


def _fox_gate_kernel(g_ref, gt_ref, ccol_ref, crow_ref, carry_col_ref, carry_row_ref):
    @pl.when(pl.program_id(1) == 0)
    def _():
        carry_col_ref[...] = jnp.zeros_like(carry_col_ref)
        carry_row_ref[...] = jnp.zeros_like(carry_row_ref)

    t = g_ref.shape[0]
    row = lax.broadcasted_iota(jnp.int32, (t, t), 0)
    col = lax.broadcasted_iota(jnp.int32, (t, t), 1)
    tri = (col <= row).astype(BF16)
    tri_t = (row <= col).astype(BF16)
    c_col = _cumsum_rows(tri, _log_sigmoid(g_ref[...])) + carry_col_ref[...]
    c_row = _cumsum_lanes(_log_sigmoid(gt_ref[...]), tri_t) + carry_row_ref[...]
    ccol_ref[...] = c_col
    crow_ref[...] = c_row
    carry_col_ref[...] = c_col[t - 1:t, :]
    carry_row_ref[...] = c_row[:, t - 1:t]


def _fox_gate_cumsum(g, gt, batch, seq):
    t = _tile(seq, CUM_T, LANES)
    ns = seq // t
    return pl.pallas_call(
        _fox_gate_kernel,
        grid=(batch, ns),
        in_specs=[
            pl.BlockSpec((t, LANES), lambda b, s: (b * ns + s, 0)),
            pl.BlockSpec((GATE_ROWS, t), lambda b, s: (0, b * ns + s)),
        ],
        out_specs=[
            pl.BlockSpec((t, LANES), lambda b, s: (b * ns + s, 0)),
            pl.BlockSpec((None, GATE_ROWS, t), lambda b, s: (b, 0, s)),
        ],
        out_shape=[
            jax.ShapeDtypeStruct((batch * seq, LANES), F32),
            jax.ShapeDtypeStruct((batch, GATE_ROWS, seq), F32),
        ],
        scratch_shapes=[pltpu.VMEM((1, LANES), F32), pltpu.VMEM((GATE_ROWS, 1), F32)],
        compiler_params=_params(("arbitrary", "arbitrary")),
        name="fox_gate_cumsum",
    )(g, gt)


def _fox_kernel(q_ref, k_ref, v_ref, ccol_ref, crow_ref, o_ref, *, scale, group):
    first_head = pl.program_id(1) * group
    qi = pl.program_id(2)
    t = q_ref.shape[0]
    lane = lax.broadcasted_iota(jnp.int32, (t, LANES), 1)
    ccol = ccol_ref[...]
    heads = []
    for g in range(group):
        sl = slice(g * HEAD_DIM, (g + 1) * HEAD_DIM)
        q = (q_ref[:, sl].astype(F32) * (scale * LOG2_E)).astype(BF16)
        cq = jnp.sum(jnp.where(lane == first_head + g, ccol, 0.0), axis=-1, keepdims=True) * LOG2_E
        heads.append((sl, q, cq))

    def tile(ki, carry, diagonal):
        start = pl.multiple_of(ki * t, t)
        out = []
        for g, (sl, q, cq) in enumerate(heads):
            m, l, acc = carry[g]
            k = k_ref[pl.ds(start, t), sl]
            v = v_ref[pl.ds(start, t), sl]
            s = lax.dot_general(q, k, NT_DIMS, preferred_element_type=F32) + (cq - crow_ref[g, ki] * LOG2_E)
            if diagonal:
                row = lax.broadcasted_iota(jnp.int32, (t, t), 0)
                col = lax.broadcasted_iota(jnp.int32, (t, t), 1)
                s = jnp.where(col <= row, s, -jnp.inf)
            m_new = jnp.maximum(m, jnp.max(s, axis=-1, keepdims=True))
            alpha = jnp.exp2(m - m_new)
            p = jnp.exp2(s - m_new)
            l = alpha * l + jnp.sum(p, axis=-1, keepdims=True)
            acc = alpha * acc + jnp.dot(p.astype(BF16), v, preferred_element_type=F32)
            out.append((m_new, l, acc))
        return tuple(out)

    init = tuple((jnp.full((t, 1), -jnp.inf, F32), jnp.zeros((t, 1), F32), jnp.zeros((t, HEAD_DIM), F32))
                 for _ in range(group))
    carry = lax.fori_loop(0, qi, lambda ki, c: tile(ki, c, False), init)
    for (sl, _, _), (_, l, acc) in zip(heads, tile(qi, carry, True)):
        o_ref[:, sl] = (acc / l).astype(o_ref.dtype)


def _fox_attention(proj, c_col, c_row, batch, seq, heads):
    t = _tile(seq, FOX_T, LANES)
    nq = seq // t
    group = FOX_GROUP
    assert heads % group == 0
    hb = heads // group
    gw = group * HEAD_DIM
    c_row = c_row.reshape(batch, GATE_ROWS, nq, 1, t)
    return pl.pallas_call(
        functools.partial(_fox_kernel, scale=HEAD_DIM ** -0.5, group=group),
        grid=(batch, hb, nq),
        in_specs=[
            pl.BlockSpec((t, gw), lambda b, h, i: (b * nq + i, h)),
            pl.BlockSpec((seq, gw), lambda b, h, i: (b, hb + h)),
            pl.BlockSpec((seq, gw), lambda b, h, i: (b, 2 * hb + h)),
            pl.BlockSpec((t, LANES), lambda b, h, i: (b * nq + i, 0)),
            pl.BlockSpec((None, group, nq, 1, t), lambda b, h, i: (b, h, 0, 0, 0)),
        ],
        out_specs=pl.BlockSpec((t, gw), lambda b, h, i: (b * nq + i, h)),
        out_shape=jax.ShapeDtypeStruct((batch * seq, heads * HEAD_DIM), BF16),
        compiler_params=_params(("arbitrary", "arbitrary", "arbitrary")),
        name="fox_attention",
    )(proj, proj, proj, c_col, c_row)


def _rope(x, cos, sin_signed):
    xf = x.astype(F32)
    return xf * cos + pltpu.roll(xf, HEAD_DIM // 2, 1) * sin_signed


def _swa_kernel(sink_ref, q_ref, k_ref, v_ref, kp_ref, vp_ref, cos_ref, sin_ref, cosp_ref, sinp_ref, o_ref,
                *, scale, group):
    w = SWA_WINDOW
    first_visible = jnp.where(pl.program_id(1) == 0, w, 0)
    nblk = q_ref.shape[0] // w
    cos, sin = cos_ref[...], sin_ref[...]
    cosp, sinp = cosp_ref[...], sinp_ref[...]
    row = lax.broadcasted_iota(jnp.int32, (w, 2 * w), 0)
    col = lax.broadcasted_iota(jnp.int32, (w, 2 * w), 1)
    valid = jnp.logical_and(col > row, col - w <= row)
    valid_first = jnp.logical_and(valid, col >= first_visible)
    for kv in range(SWA_KV_HEADS):
        ksl = slice(kv * HEAD_DIM, (kv + 1) * HEAD_DIM)
        k_all = jnp.concatenate([_rope(kp_ref[:, ksl], cosp, sinp), _rope(k_ref[:, ksl], cos, sin)],
                                axis=0).astype(BF16)
        v_all = jnp.concatenate([vp_ref[:, ksl], v_ref[:, ksl]], axis=0)
        for g in range(group):
            hq = kv * group + g
            qsl = slice(hq * HEAD_DIM, (hq + 1) * HEAD_DIM)
            q = _rope(q_ref[:, qsl], cos, sin).astype(BF16)
            sink = sink_ref[hq]
            for blk in range(nblk):
                qb = q[blk * w:(blk + 1) * w, :]
                kb = k_all[blk * w:(blk + 2) * w, :]
                vb = v_all[blk * w:(blk + 2) * w, :]
                s = lax.dot_general(qb, kb, NT_DIMS, preferred_element_type=F32) * scale
                s = jnp.where(valid_first if blk == 0 else valid, s, -jnp.inf)
                m = jnp.maximum(jnp.max(s, axis=-1, keepdims=True), sink)
                p = jnp.exp(s - m)
                denom = jnp.sum(p, axis=-1, keepdims=True) + jnp.exp(sink - m)
                o = jnp.dot((p / denom).astype(BF16), vb, preferred_element_type=F32)
                o_ref[blk * w:(blk + 1) * w, qsl] = o.astype(o_ref.dtype)


def _swa_attention(proj, sinks, cos, sin_signed, batch, seq, q_col, k_col, v_col, q_heads):
    t = _tile(seq, SWA_T, SWA_WINDOW)
    nt = seq // t
    per = t // SWA_WINDOW
    qw = q_heads * HEAD_DIM
    kvw = SWA_KV_HEADS * HEAD_DIM
    assert q_col % qw == 0 and k_col % kvw == 0 and v_col % kvw == 0

    def prev(b, i):
        return jnp.maximum((b * nt + i) * per - 1, 0)

    def prev_pos(i):
        return jnp.maximum(i * per - 1, 0)

    return pl.pallas_call(
        functools.partial(_swa_kernel, scale=HEAD_DIM ** -0.5, group=q_heads // SWA_KV_HEADS),
        grid=(batch, nt),
        in_specs=[
            pl.BlockSpec(memory_space=pltpu.SMEM),
            pl.BlockSpec((t, qw), lambda b, i: (b * nt + i, q_col // qw)),
            pl.BlockSpec((t, kvw), lambda b, i: (b * nt + i, k_col // kvw)),
            pl.BlockSpec((t, kvw), lambda b, i: (b * nt + i, v_col // kvw)),
            pl.BlockSpec((SWA_WINDOW, kvw), lambda b, i: (prev(b, i), k_col // kvw)),
            pl.BlockSpec((SWA_WINDOW, kvw), lambda b, i: (prev(b, i), v_col // kvw)),
            pl.BlockSpec((t, HEAD_DIM), lambda b, i: (i, 0)),
            pl.BlockSpec((t, HEAD_DIM), lambda b, i: (i, 0)),
            pl.BlockSpec((SWA_WINDOW, HEAD_DIM), lambda b, i: (prev_pos(i), 0)),
            pl.BlockSpec((SWA_WINDOW, HEAD_DIM), lambda b, i: (prev_pos(i), 0)),
        ],
        out_specs=pl.BlockSpec((t, qw), lambda b, i: (b * nt + i, 0)),
        out_shape=jax.ShapeDtypeStruct((batch * seq, qw), BF16),
        compiler_params=_params(("arbitrary", "arbitrary")),
        name="swa_attention",
    )(sinks, proj, proj, proj, proj, proj, cos, sin_signed, cos, sin_signed)


def _mlstm_kernel(q_ref, k_ref, v_ref, og_ref, g_ref, gt_ref, o_ref, state_ref, m_ref, *, dk, dv):
    heads = MLSTM_HEADS

    @pl.when(pl.program_id(1) == 0)
    def _():
        state_ref[...] = jnp.zeros_like(state_ref)
        m_ref[...] = jnp.zeros_like(m_ref)

    n = q_ref.shape[0]
    g = g_ref[...]
    gt = gt_ref[...]
    row = lax.broadcasted_iota(jnp.int32, (n, n), 0)
    col = lax.broadcasted_iota(jnp.int32, (n, n), 1)
    causal = col <= row
    b_col_all = _cumsum_rows(causal.astype(BF16), _log_sigmoid(g))
    b_row_all = _cumsum_lanes(_log_sigmoid(gt), (row <= col).astype(BF16))
    unit = (lax.broadcasted_iota(jnp.int32, (n, LANES), 1) == 0).astype(BF16)

    for h in range(heads):
        i_col = g[:, h:h + 1]
        b_col = b_col_all[:, heads + h:heads + h + 1]
        i_row = gt[h:h + 1, :]
        b_row = b_row_all[heads + h:heads + h + 1, :]
        b_last = b_row[:, n - 1:n]
        m_prev = m_ref[h]

        d = jnp.where(causal, (b_col - b_row) + i_row, -jnp.inf)
        m_inter = b_col + m_prev
        m_t = jnp.maximum(m_inter, jnp.max(d, axis=-1, keepdims=True))
        inter = jnp.exp(m_inter - m_t)
        wgt = jnp.exp(d - m_t)

        qh = q_ref[:, h * dk:(h + 1) * dk]
        kh = (k_ref[:, h * dk:(h + 1) * dk].astype(F32) * (dk ** -0.5)).astype(BF16)
        v_ext = jnp.concatenate([v_ref[:, h * dv:(h + 1) * dv], unit], axis=1)
        state = state_ref[h]

        sm = wgt * lax.dot_general(qh, kh, NT_DIMS, preferred_element_type=F32)
        tot = inter * jnp.dot(qh, state.astype(BF16), preferred_element_type=F32)
        tot = tot + jnp.dot(sm.astype(BF16), v_ext, preferred_element_type=F32)
        den = jnp.maximum(jnp.abs(tot[:, dv:dv + 1]), jnp.exp(-m_t))
        hid = tot[:, :dv] / den
        gate = _sigmoid(og_ref[:, h * dv:(h + 1) * dv].astype(F32))
        o_ref[:, h * dv:(h + 1) * dv] = (gate * hid).astype(o_ref.dtype)

        m_new = jnp.maximum(b_last + m_prev, jnp.max((b_last - b_row) + i_row, axis=-1, keepdims=True))
        decay = jnp.exp(b_last + m_prev - m_new)
        w_end = jnp.exp(((b_last - b_col) + i_col) - m_new)
        vw = (v_ext.astype(F32) * w_end).astype(BF16)
        state_ref[h] = decay * state + lax.dot_general(kh, vw, TN_DIMS, preferred_element_type=F32)
        m_ref[h] = m_new


def _mlstm(proj, g, gt, batch, seq, dk, dv):
    heads = MLSTM_HEADS
    n = _tile(seq, MLSTM_L, LANES)
    nc = seq // n
    qk, vw = heads * dk, heads * dv
    assert vw % qk == 0
    r = vw // qk
    return pl.pallas_call(
        functools.partial(_mlstm_kernel, dk=dk, dv=dv),
        grid=(batch, nc),
        in_specs=[
            pl.BlockSpec((n, qk), lambda b, c: (b * nc + c, 0)),
            pl.BlockSpec((n, qk), lambda b, c: (b * nc + c, 1)),
            pl.BlockSpec((n, vw), lambda b, c: (b * nc + c, 2 // r)),
            pl.BlockSpec((n, vw), lambda b, c: (b * nc + c, 2 // r + 1)),
            pl.BlockSpec((n, LANES), lambda b, c: (b * nc + c, 0)),
            pl.BlockSpec((GATE_ROWS, n), lambda b, c: (0, b * nc + c)),
        ],
        out_specs=pl.BlockSpec((n, vw), lambda b, c: (b * nc + c, 0)),
        out_shape=jax.ShapeDtypeStruct((batch * seq, vw), BF16),
        scratch_shapes=[pltpu.VMEM((heads, dk, dv + LANES), F32), pltpu.VMEM((heads, 1, 1), F32)],
        compiler_params=_params(("arbitrary", "arbitrary")),
        name="mlstm",
    )(proj, proj, proj, proj, g, gt)


def _outproj_ln_kernel(*refs, alpha, n_in):
    h_refs, w_refs = refs[:n_in], refs[n_in:2 * n_in]
    x_ref, g_ref, b_ref, o_ref = refs[2 * n_in:]
    y = sum(jnp.dot(h[...], w[...], preferred_element_type=F32) for h, w in zip(h_refs, w_refs))
    o_ref[...] = _layer_norm(alpha * x_ref[...] + y, g_ref[...], b_ref[...])


def _out_projection_ln(hs, ws, x, ln_g, ln_b, alpha):
    m, d = x.shape
    bm = _tile(m, OUT_BM, LANES)
    in_specs = [pl.BlockSpec((bm, h.shape[1]), lambda i: (i, 0)) for h in hs]
    in_specs += [pl.BlockSpec(w.shape, lambda i: (0, 0)) for w in ws]
    in_specs += [pl.BlockSpec((bm, d), lambda i: (i, 0)),
                 pl.BlockSpec((1, d), lambda i: (0, 0)),
                 pl.BlockSpec((1, d), lambda i: (0, 0))]
    return pl.pallas_call(
        functools.partial(_outproj_ln_kernel, alpha=alpha, n_in=len(hs)),
        grid=(m // bm,),
        in_specs=in_specs,
        out_specs=pl.BlockSpec((bm, d), lambda i: (i, 0)),
        out_shape=jax.ShapeDtypeStruct((m, d), F32),
        compiler_params=_params(("arbitrary",)),
        name="out_projection_ln",
    )(*hs, *ws, x, ln_g, ln_b)


def _causal_conv(u, prev, w_ref, b_ref, cs):
    w0, w1, w2, b = w_ref[0:1, cs], w_ref[1:2, cs], w_ref[2:3, cs], b_ref[:, cs]
    body = b + w0 * pltpu.roll(u, 2, 0) + w1 * pltpu.roll(u, 1, 0) + w2 * u
    top = jnp.concatenate([prev, u[:HEAD_ROWS, :]], axis=0)
    head = b + w0 * pltpu.roll(top, 2, 0) + w1 * pltpu.roll(top, 1, 0) + w2 * top
    return body, head[SUBLANES:, :]


def _ffn_kernel(x_ref, wg_ref, wv_ref, cwg_ref, cwv_ref, cbg_ref, cbv_ref, wd_ref, lg_ref, lb_ref, o_ref,
                xb_ref, acc_ref, h0_ref, h1_ref, pg_ref, pv_ref, ug_ref, uv_ref, *, alpha, tiles_per_seq, nf):
    i = pl.program_id(0)
    j = pl.program_id(1)
    bm = x_ref.shape[0]
    h_refs = (h0_ref, h1_ref)

    def up(h_ref):
        xb = xb_ref[...]
        kept_rows = jnp.where(i % tiles_per_seq != 0, SUBLANES, 0)
        keep = lax.broadcasted_iota(jnp.int32, (SUBLANES, FFN_CHUNK), 0) < kept_rows
        chunks = [slice(c * FFN_CHUNK, (c + 1) * FFN_CHUNK) for c in range(h_ref.shape[1] // FFN_CHUNK)]
        prev_g = [jnp.where(keep, pg_ref[j, :, cs], 0.0) for cs in chunks]
        prev_v = [jnp.where(keep, pv_ref[j, :, cs], 0.0) for cs in chunks]
        for r in range(bm // FFN_ROWS):
            xr = xb[r * FFN_ROWS:(r + 1) * FFN_ROWS, :]
            for c, cs in enumerate(chunks):
                slot = (r * len(chunks) + c) % FFN_SLOTS
                ug_ref[slot] = jnp.dot(xr, wg_ref[:, cs], preferred_element_type=F32)
                uv_ref[slot] = jnp.dot(xr, wv_ref[:, cs], preferred_element_type=F32)
                ug, uv = ug_ref[slot], uv_ref[slot]
                cg, cg_top = _causal_conv(ug, prev_g[c], cwg_ref, cbg_ref, cs)
                cv, cv_top = _causal_conv(uv, prev_v[c], cwv_ref, cbv_ref, cs)
                prev_g[c] = ug[FFN_ROWS - SUBLANES:, :]
                prev_v[c] = uv[FFN_ROWS - SUBLANES:, :]
                h_ref[r * FFN_ROWS:(r + 1) * FFN_ROWS, cs] = (cg * _sigmoid(cg) * cv).astype(BF16)
                h_ref[r * FFN_ROWS:r * FFN_ROWS + HEAD_ROWS, cs] = (cg_top * _sigmoid(cg_top) * cv_top).astype(BF16)
        for c, cs in enumerate(chunks):
            pg_ref[j, :, cs] = prev_g[c]
            pv_ref[j, :, cs] = prev_v[c]

    def down(h_ref):
        acc_ref[...] += jnp.dot(h_ref[...], wd_ref[...], preferred_element_type=F32)

    @pl.when(j == 0)
    def _():
        xb_ref[...] = x_ref[...].astype(BF16)
        acc_ref[...] = jnp.zeros_like(acc_ref)
        up(h_refs[0])

    for parity in range(2):
        @pl.when(jnp.logical_and(jnp.logical_and(j > 0, j < nf), j % 2 == parity))
        def _():
            up(h_refs[parity])
            down(h_refs[1 - parity])

    @pl.when(j == nf)
    def _():
        down(h_refs[(nf - 1) % 2])
        o_ref[...] = _layer_norm(alpha * x_ref[...] + acc_ref[...], lg_ref[...], lb_ref[...])


def _conv_ffn_ln(x, w_up, conv_w, conv_b, w_down, ln_g, ln_b, alpha, seq):
    m, d = x.shape
    f = w_down.shape[0]
    bm = _tile(seq, FFN_BM, 2 * SUBLANES)
    bf = _tile(f, FFN_BF, LANES)
    nf = f // bf

    def up_blk(j):
        return jnp.minimum(j, nf - 1)

    def down_blk(j):
        return jnp.maximum(j - 1, 0)

    return pl.pallas_call(
        functools.partial(_ffn_kernel, alpha=alpha, tiles_per_seq=seq // bm, nf=nf),
        grid=(m // bm, nf + 1),
        in_specs=[
            pl.BlockSpec((bm, d), lambda i, j: (i, 0)),
            pl.BlockSpec((d, bf), lambda i, j: (0, up_blk(j))),
            pl.BlockSpec((d, bf), lambda i, j: (0, nf + up_blk(j))),
            pl.BlockSpec((CONV_WIDTH, bf), lambda i, j: (0, up_blk(j))),
            pl.BlockSpec((CONV_WIDTH, bf), lambda i, j: (0, nf + up_blk(j))),
            pl.BlockSpec((1, bf), lambda i, j: (0, up_blk(j))),
            pl.BlockSpec((1, bf), lambda i, j: (0, nf + up_blk(j))),
            pl.BlockSpec((bf, d), lambda i, j: (down_blk(j), 0)),
            pl.BlockSpec((1, d), lambda i, j: (0, 0)),
            pl.BlockSpec((1, d), lambda i, j: (0, 0)),
        ],
        out_specs=pl.BlockSpec((bm, d), lambda i, j: (i, 0)),
        out_shape=jax.ShapeDtypeStruct((m, d), F32),
        scratch_shapes=[
            pltpu.VMEM((bm, d), BF16),
            pltpu.VMEM((bm, d), F32),
            pltpu.VMEM((bm, bf), BF16),
            pltpu.VMEM((bm, bf), BF16),
            pltpu.VMEM((nf, SUBLANES, bf), F32),
            pltpu.VMEM((nf, SUBLANES, bf), F32),
            pltpu.VMEM((FFN_SLOTS, FFN_ROWS, FFN_CHUNK), F32),
            pltpu.VMEM((FFN_SLOTS, FFN_ROWS, FFN_CHUNK), F32),
        ],
        compiler_params=_params(("arbitrary", "arbitrary")),
        name="conv_ffn_ln",
    )(x, w_up, w_up, conv_w, conv_w, conv_b, conv_b, w_down, ln_g, ln_b)


def _gate_params(w, b):
    d, n = w.shape
    wg = jnp.zeros((d, LANES), BF16).at[:, :n].set(w.astype(BF16))
    bg = jnp.zeros((1, LANES), F32).at[0, :n].set(b)
    wgt = jnp.zeros((GATE_ROWS, d), BF16).at[:n, :].set(w.T.astype(BF16))
    bgt = jnp.zeros((GATE_ROWS, 1), F32).at[:n, 0].set(b)
    return wg, bg, wgt, bgt


def _rope_tables(seq):
    half = HEAD_DIM // 2
    inv_freq = jnp.power(ROPE_THETA, -jnp.arange(half, dtype=F32) * (2.0 / HEAD_DIM))
    ang = jnp.arange(seq, dtype=F32)[:, None] * inv_freq[None, :]
    cos, sin = jnp.cos(ang), jnp.sin(ang)
    return jnp.concatenate([cos, cos], axis=-1), jnp.concatenate([-sin, sin], axis=-1)


def kernel(x, attn_w_in, attn_b_in, attn_sinks, attn_w_out, mlstm_w_in, mlstm_b_in, mlstm_w_out, ffn_w_up,
           ffn_conv_w, ffn_conv_b, ffn_w_down, ln1_g, ln1_b, ln2_g, ln2_b):
    batch, seq, d = x.shape
    depth = ln1_g.shape[0]
    alpha = float((2 * depth) ** 0.25)
    fox_heads = d // (2 * HEAD_DIM)
    swa_heads = d // (2 * HEAD_DIM)
    fox_dim = fox_heads * HEAD_DIM
    fox_f_off = 3 * fox_dim
    dk, dv = d // (2 * MLSTM_HEADS), d // MLSTM_HEADS
    mlstm_main = 2 * MLSTM_HEADS * dk + 2 * MLSTM_HEADS * dv
    cos, sin_signed = _rope_tables(seq)

    h = x.reshape(batch * seq, d)
    for layer in range(depth):
        j = layer // 2
        if layer % 2 == 0:
            w_in, b_in = attn_w_in[j], attn_b_in[j]
            w_main = jnp.concatenate([w_in[:, :fox_f_off], w_in[:, fox_f_off + fox_heads:]], axis=1).astype(BF16)
            b_main = jnp.concatenate([b_in[:fox_f_off], b_in[fox_f_off + fox_heads:]])[None, :]
            gates = _gate_params(w_in[:, fox_f_off:fox_f_off + fox_heads], b_in[fox_f_off:fox_f_off + fox_heads])
            proj, g, gt = _in_projection(h, w_main, b_main, *gates)
            c_col, c_row = _fox_gate_cumsum(g, gt, batch, seq)
            fox = _fox_attention(proj, c_col, c_row, batch, seq, fox_heads)
            swa_q = 3 * fox_dim
            swa_k = swa_q + swa_heads * HEAD_DIM
            swa_v = swa_k + SWA_KV_HEADS * HEAD_DIM
            swa = _swa_attention(proj, attn_sinks[j], cos, sin_signed, batch, seq, swa_q, swa_k, swa_v, swa_heads)
            w_out = attn_w_out[j].astype(BF16)
            h = _out_projection_ln([fox, swa], [w_out[:fox_dim], w_out[fox_dim:]], h,
                                   ln1_g[layer][None, :], ln1_b[layer][None, :], alpha)
        else:
            w_in, b_in = mlstm_w_in[j], mlstm_b_in[j]
            gates = _gate_params(w_in[:, mlstm_main:], b_in[mlstm_main:])
            proj, g, gt = _in_projection(h, w_in[:, :mlstm_main].astype(BF16), b_in[None, :mlstm_main], *gates)
            mixed = _mlstm(proj, g, gt, batch, seq, dk, dv)
            h = _out_projection_ln([mixed], [mlstm_w_out[j].astype(BF16)], h,
                                   ln1_g[layer][None, :], ln1_b[layer][None, :], alpha)
        h = _conv_ffn_ln(h, ffn_w_up[layer].astype(BF16), ffn_conv_w[layer], ffn_conv_b[layer][None, :],
                         ffn_w_down[layer].astype(BF16), ln2_g[layer][None, :], ln2_b[layer][None, :], alpha, seq)
    return h.reshape(batch, seq, d)
```

```python
import functools

import jax
import jax.numpy as jnp
from jax import lax
from jax.experimental import pallas as pl
from jax.experimental.pallas import tpu as pltpu

F32 = jnp.float32
BF16 = jnp.bfloat16

HEAD_DIM = 128
SWA_KV_HEADS = 2
SWA_WINDOW = 128
ROPE_THETA = 10000.0
MLSTM_HEADS = 8
CONV_WIDTH = 3
LN_EPS = 1e-5
LOG2_E = 1.4426950408889634

LANES = 128
SUBLANES = 8
GATE_ROWS = 16
HEAD_ROWS = 16
VMEM_LIMIT_BYTES = 56 * 1024 * 1024

PROJ_BM = 1024
PROJ_BN = 1536
OUT_BM = 512
FFN_BM = 512
FFN_BF = 512
FFN_CHUNK = 256
FFN_ROWS = 256
FFN_SLOTS = 2
FOX_T = 512
FOX_GROUP = 2
SWA_T = 512
CUM_T = 512
MLSTM_L = 256

NT_DIMS = (((1,), (1,)), ((), ()))
TN_DIMS = (((0,), (0,)), ((), ()))


def _tile(n, pref, unit):
    t = min(pref, n)
    while n % t or t % unit:
        t -= unit
    assert t > 0, (n, pref, unit)
    return t


def _params(sem):
    return pltpu.CompilerParams(dimension_semantics=sem, vmem_limit_bytes=VMEM_LIMIT_BYTES)


def _log_sigmoid(x):
    return jnp.minimum(x, 0.0) - jnp.log(1.0 + jnp.exp(-jnp.abs(x)))


def _sigmoid(x):
    return 1.0 / (1.0 + jnp.exp(-x))


def _split3(x):
    h1 = x.astype(BF16)
    r1 = x - h1.astype(F32)
    h2 = r1.astype(BF16)
    h3 = (r1 - h2.astype(F32)).astype(BF16)
    return h1, h2, h3


def _cumsum_rows(tri, x):
    return sum(jnp.dot(tri, h, preferred_element_type=F32) for h in _split3(x))


def _cumsum_lanes(x, tri_t):
    return sum(jnp.dot(h, tri_t, preferred_element_type=F32) for h in _split3(x))


def _layer_norm(z, g, b):
    mu = jnp.mean(z, axis=-1, keepdims=True)
    zc = z - mu
    var = jnp.mean(zc * zc, axis=-1, keepdims=True)
    return zc * lax.rsqrt(var + LN_EPS) * g + b


def _proj_kernel(x_ref, w_ref, b_ref, wg_ref, bg_ref, wgt_ref, bgt_ref, o_ref, g_ref, gt_ref, xb_ref):
    @pl.when(pl.program_id(1) == 0)
    def _():
        xb = x_ref[...].astype(BF16)
        xb_ref[...] = xb
        g_ref[...] = jnp.dot(xb, wg_ref[...], preferred_element_type=F32) + bg_ref[...]
        gt_ref[...] = lax.dot_general(wgt_ref[...], xb, NT_DIMS, preferred_element_type=F32) + bgt_ref[...]

    acc = jnp.dot(xb_ref[...], w_ref[...], preferred_element_type=F32)
    o_ref[...] = (acc + b_ref[...]).astype(o_ref.dtype)


def _in_projection(x, w, layer, b, wg, bg, wgt, bgt):
    m, d = x.shape
    n = w.shape[2]
    bm = _tile(m, PROJ_BM, LANES)
    bn = _tile(n, PROJ_BN, LANES)
    return pl.pallas_call(
        _proj_kernel,
        grid=(m // bm, n // bn),
        in_specs=[
            pl.BlockSpec((bm, d), lambda i, j: (i, 0)),
            pl.BlockSpec((None, d, bn), lambda i, j: (layer, 0, j)),
            pl.BlockSpec((1, bn), lambda i, j: (0, j)),
            pl.BlockSpec((d, LANES), lambda i, j: (0, 0)),
            pl.BlockSpec((1, LANES), lambda i, j: (0, 0)),
            pl.BlockSpec((GATE_ROWS, d), lambda i, j: (0, 0)),
            pl.BlockSpec((GATE_ROWS, 1), lambda i, j: (0, 0)),
        ],
        out_specs=[
            pl.BlockSpec((bm, bn), lambda i, j: (i, j)),
            pl.BlockSpec((bm, LANES), lambda i, j: (i, 0)),
            pl.BlockSpec((GATE_ROWS, bm), lambda i, j: (0, i)),
        ],
        out_shape=[
            jax.ShapeDtypeStruct((m, n), BF16),
            jax.ShapeDtypeStruct((m, LANES), F32),
            jax.ShapeDtypeStruct((GATE_ROWS, m), F32),
        ],
        scratch_shapes=[pltpu.VMEM((bm, d), BF16)],
        compiler_params=_params(("arbitrary", "arbitrary")),
        name="in_projection",
    )(x, w, b, wg, bg, wgt, bgt)


def _fox_gate_kernel(g_ref, c_ref, carry_ref):
    @pl.when(pl.program_id(1) == 0)
    def _():
        carry_ref[...] = jnp.zeros_like(carry_ref)

    t = g_ref.shape[0]
    row = lax.broadcasted_iota(jnp.int32, (t, t), 0)
    col = lax.broadcasted_iota(jnp.int32, (t, t), 1)
    c = _cumsum_rows((col <= row).astype(BF16), _log_sigmoid(g_ref[...])) + carry_ref[...]
    c_ref[...] = c
    carry_ref[...] = c[t - 1:t, :]


def _fox_gate_cumsum(g, batch, seq):
    t = _tile(seq, CUM_T, LANES)
    ns = seq // t
    return pl.pallas_call(
        _fox_gate_kernel,
        grid=(batch, ns),
        in_specs=[pl.BlockSpec((t, LANES), lambda b, s: (b * ns + s, 0))],
        out_specs=pl.BlockSpec((t, LANES), lambda b, s: (b * ns + s, 0)),
        out_shape=jax.ShapeDtypeStruct((batch * seq, LANES), F32),
        scratch_shapes=[pltpu.VMEM((1, LANES), F32)],
        compiler_params=_params(("arbitrary", "arbitrary")),
        name="fox_gate_cumsum",
    )(g)


def _bias_lanes(c, ones_first):
    n = c.shape[0]
    h1, h2, h3 = (h.astype(F32) for h in _split3(c))
    lane = lax.broadcasted_iota(jnp.int32, (n, HEAD_DIM), 1)
    lo = 3 if ones_first else 0
    split = jnp.where(lane == lo, h1, jnp.where(lane == lo + 1, h2, jnp.where(lane == lo + 2, h3, 0.0)))
    ones = jnp.logical_and(lane >= 3 - lo, lane < 6 - lo)
    return jnp.where(ones, 1.0, split).astype(BF16)


def _fox_kernel(q_ref, k_ref, v_ref, ct_ref, cs_ref, o_ref, kb_ref, m_ref, acc_ref, s_ref, *, scale, group):
    first_head = pl.program_id(1) * group
    qi = pl.program_id(2)
    t = q_ref.shape[0]

    def head_lane(c, g):
        lane = lax.broadcasted_iota(jnp.int32, c.shape, 1)
        return jnp.sum(jnp.where(lane == first_head + g, c, 0.0), axis=-1, keepdims=True) * LOG2_E

    @pl.when(qi == 0)
    def _():
        cs = cs_ref[...]
        for g in range(group):
            kb_ref[g] = _bias_lanes(-head_lane(cs, g), ones_first=False)

    ct = ct_ref[...]
    unit = (lax.broadcasted_iota(jnp.int32, (t, HEAD_DIM), 1) == 0).astype(BF16)
    heads = []
    for g in range(group):
        sl = slice(g * HEAD_DIM, (g + 1) * HEAD_DIM)
        q = (q_ref[:, sl].astype(F32) * (scale * LOG2_E)).astype(BF16)
        heads.append((sl, jnp.concatenate([q, _bias_lanes(head_lane(ct, g), ones_first=True)], axis=1)))

    def logits(slot, ki):
        start = pl.multiple_of(ki * t, t)
        for g, (sl, q_ext) in enumerate(heads):
            k_ext = jnp.concatenate([k_ref[pl.ds(start, t), sl], kb_ref[g, pl.ds(start, t), :]], axis=1)
            s_ref[slot, g] = lax.dot_general(q_ext, k_ext, NT_DIMS, preferred_element_type=F32)

    def absorb(slot, ki, diagonal):
        start = pl.multiple_of(ki * t, t)
        for g, (sl, _) in enumerate(heads):
            s = s_ref[slot, g]
            if diagonal:
                row = lax.broadcasted_iota(jnp.int32, (t, t), 0)
                col = lax.broadcasted_iota(jnp.int32, (t, t), 1)
                s = jnp.where(col <= row, s, -jnp.inf)
            m = m_ref[g]
            m_new = jnp.maximum(m, jnp.max(s, axis=-1, keepdims=True))
            p = jnp.exp2(s - m_new).astype(BF16)
            v_ext = jnp.concatenate([v_ref[pl.ds(start, t), sl], unit], axis=1)
            acc_ref[g] = jnp.exp2(m - m_new) * acc_ref[g] + jnp.dot(p, v_ext, preferred_element_type=F32)
            m_ref[g] = m_new

    m_ref[...] = jnp.full(m_ref.shape, -jnp.inf, F32)
    acc_ref[...] = jnp.zeros_like(acc_ref)

    logits(0, 0)

    @pl.loop(0, lax.shift_right_logical(qi, 1))
    def _(pair):
        ki = 2 * pair
        logits(1, ki + 1)
        absorb(0, ki, False)
        logits(0, ki + 2)
        absorb(1, ki + 1, False)

    odd = lax.rem(qi, 2) == 1

    @pl.when(odd)
    def _():
        logits(1, qi)
        absorb(0, qi - 1, False)
        absorb(1, qi, True)

    @pl.when(jnp.logical_not(odd))
    def _():
        absorb(0, qi, True)

    for g, (sl, _) in enumerate(heads):
        acc = acc_ref[g]
        o_ref[:, sl] = (acc[:, :HEAD_DIM] / acc[:, HEAD_DIM:HEAD_DIM + 1]).astype(o_ref.dtype)


def _fox_attention(proj, c, batch, seq, heads):
    t = _tile(seq, FOX_T, LANES)
    nq = seq // t
    group = FOX_GROUP
    assert heads % group == 0
    hb = heads // group
    gw = group * HEAD_DIM
    return pl.pallas_call(
        functools.partial(_fox_kernel, scale=HEAD_DIM ** -0.5, group=group),
        grid=(batch, hb, nq),
        in_specs=[
            pl.BlockSpec((t, gw), lambda b, h, i: (b * nq + i, h)),
            pl.BlockSpec((seq, gw), lambda b, h, i: (b, hb + h)),
            pl.BlockSpec((seq, gw), lambda b, h, i: (b, 2 * hb + h)),
            pl.BlockSpec((t, LANES), lambda b, h, i: (b * nq + i, 0)),
            pl.BlockSpec((seq, LANES), lambda b, h, i: (b, 0)),
        ],
        out_specs=pl.BlockSpec((t, gw), lambda b, h, i: (b * nq + i, h)),
        out_shape=jax.ShapeDtypeStruct((batch * seq, heads * HEAD_DIM), BF16),
        scratch_shapes=[pltpu.VMEM((group, seq, HEAD_DIM), BF16),
                        pltpu.VMEM((group, t, 1), F32),
                        pltpu.VMEM((group, t, 2 * HEAD_DIM), F32),
                        pltpu.VMEM((2, group, t, t), F32)],
        compiler_params=_params(("arbitrary", "arbitrary", "arbitrary")),
        name="fox_attention",
    )(proj, proj, proj, c, c)


def _rope(x, cos, sin_signed):
    xf = x.astype(F32)
    return xf * cos + pltpu.roll(xf, HEAD_DIM // 2, 1) * sin_signed


def _swa_kernel(sink_ref, q_ref, k_ref, v_ref, kp_ref, vp_ref, cos_ref, sin_ref, cosp_ref, sinp_ref, o_ref,
                *, scale, group):
    w = SWA_WINDOW
    first_visible = jnp.where(pl.program_id(1) == 0, w, 0)
    nblk = q_ref.shape[0] // w
    cos, sin = cos_ref[...], sin_ref[...]
    cosp, sinp = cosp_ref[...], sinp_ref[...]
    row = lax.broadcasted_iota(jnp.int32, (w, 2 * w), 0)
    col = lax.broadcasted_iota(jnp.int32, (w, 2 * w), 1)
    valid = jnp.logical_and(col > row, col - w <= row)
    valid_first = jnp.logical_and(valid, col >= first_visible)
    for kv in range(SWA_KV_HEADS):
        ksl = slice(kv * HEAD_DIM, (kv + 1) * HEAD_DIM)
        k_all = jnp.concatenate([_rope(kp_ref[:, ksl], cosp, sinp), _rope(k_ref[:, ksl], cos, sin)],
                                axis=0).astype(BF16)
        v_all = jnp.concatenate([vp_ref[:, ksl], v_ref[:, ksl]], axis=0)
        for g in range(group):
            hq = kv * group + g
            qsl = slice(hq * HEAD_DIM, (hq + 1) * HEAD_DIM)
            q = _rope(q_ref[:, qsl], cos, sin).astype(BF16)
            sink = sink_ref[hq]
            for blk in range(nblk):
                qb = q[blk * w:(blk + 1) * w, :]
                kb = k_all[blk * w:(blk + 2) * w, :]
                vb = v_all[blk * w:(blk + 2) * w, :]
                s = lax.dot_general(qb, kb, NT_DIMS, preferred_element_type=F32) * scale
                s = jnp.where(valid_first if blk == 0 else valid, s, -jnp.inf)
                m = jnp.maximum(jnp.max(s, axis=-1, keepdims=True), sink)
                p = jnp.exp(s - m)
                denom = jnp.sum(p, axis=-1, keepdims=True) + jnp.exp(sink - m)
                o = jnp.dot((p / denom).astype(BF16), vb, preferred_element_type=F32)
                o_ref[blk * w:(blk + 1) * w, qsl] = o.astype(o_ref.dtype)


def _swa_attention(proj, sinks, cos, sin_signed, batch, seq, q_col, k_col, v_col, q_heads):
    t = _tile(seq, SWA_T, SWA_WINDOW)
    nt = seq // t
    per = t // SWA_WINDOW
    qw = q_heads * HEAD_DIM
    kvw = SWA_KV_HEADS * HEAD_DIM
    assert q_col % qw == 0 and k_col % kvw == 0 and v_col % kvw == 0

    def prev(b, i):
        return jnp.maximum((b * nt + i) * per - 1, 0)

    def prev_pos(i):
        return jnp.maximum(i * per - 1, 0)

    return pl.pallas_call(
        functools.partial(_swa_kernel, scale=HEAD_DIM ** -0.5, group=q_heads // SWA_KV_HEADS),
        grid=(batch, nt),
        in_specs=[
            pl.BlockSpec(memory_space=pltpu.SMEM),
            pl.BlockSpec((t, qw), lambda b, i: (b * nt + i, q_col // qw)),
            pl.BlockSpec((t, kvw), lambda b, i: (b * nt + i, k_col // kvw)),
            pl.BlockSpec((t, kvw), lambda b, i: (b * nt + i, v_col // kvw)),
            pl.BlockSpec((SWA_WINDOW, kvw), lambda b, i: (prev(b, i), k_col // kvw)),
            pl.BlockSpec((SWA_WINDOW, kvw), lambda b, i: (prev(b, i), v_col // kvw)),
            pl.BlockSpec((t, HEAD_DIM), lambda b, i: (i, 0)),
            pl.BlockSpec((t, HEAD_DIM), lambda b, i: (i, 0)),
            pl.BlockSpec((SWA_WINDOW, HEAD_DIM), lambda b, i: (prev_pos(i), 0)),
            pl.BlockSpec((SWA_WINDOW, HEAD_DIM), lambda b, i: (prev_pos(i), 0)),
        ],
        out_specs=pl.BlockSpec((t, qw), lambda b, i: (b * nt + i, 0)),
        out_shape=jax.ShapeDtypeStruct((batch * seq, qw), BF16),
        compiler_params=_params(("arbitrary", "arbitrary")),
        name="swa_attention",
    )(sinks, proj, proj, proj, proj, proj, cos, sin_signed, cos, sin_signed)


def _mlstm_kernel(q_ref, k_ref, v_ref, og_ref, g_ref, gt_ref, o_ref, state_ref, m_ref, *, dk, dv):
    heads = MLSTM_HEADS

    @pl.when(pl.program_id(1) == 0)
    def _():
        state_ref[...] = jnp.zeros_like(state_ref)
        m_ref[...] = jnp.zeros_like(m_ref)

    n = q_ref.shape[0]
    g = g_ref[...]
    gt = gt_ref[...]
    row = lax.broadcasted_iota(jnp.int32, (n, n), 0)
    col = lax.broadcasted_iota(jnp.int32, (n, n), 1)
    causal = col <= row
    b_col_all = _cumsum_rows(causal.astype(BF16), _log_sigmoid(g))
    b_row_all = _cumsum_lanes(_log_sigmoid(gt), (row <= col).astype(BF16))
    unit = (lax.broadcasted_iota(jnp.int32, (n, LANES), 1) == 0).astype(BF16)

    for h in range(heads):
        i_col = g[:, h:h + 1]
        b_col = b_col_all[:, heads + h:heads + h + 1]
        i_row = gt[h:h + 1, :]
        b_row = b_row_all[heads + h:heads + h + 1, :]
        b_last = b_row[:, n - 1:n]
        m_prev = m_ref[h]

        d = jnp.where(causal, (b_col - b_row) + i_row, -jnp.inf)
        m_inter = b_col + m_prev
        m_t = jnp.maximum(m_inter, jnp.max(d, axis=-1, keepdims=True))
        inter = jnp.exp(m_inter - m_t)
        wgt = jnp.exp(d - m_t)

        qh = q_ref[:, h * dk:(h + 1) * dk]
        kh = (k_ref[:, h * dk:(h + 1) * dk].astype(F32) * (dk ** -0.5)).astype(BF16)
        v_ext = jnp.concatenate([v_ref[:, h * dv:(h + 1) * dv], unit], axis=1)
        state = state_ref[h]

        sm = wgt * lax.dot_general(qh, kh, NT_DIMS, preferred_element_type=F32)
        tot = inter * jnp.dot(qh, state.astype(BF16), preferred_element_type=F32)
        tot = tot + jnp.dot(sm.astype(BF16), v_ext, preferred_element_type=F32)
        den = jnp.maximum(jnp.abs(tot[:, dv:dv + 1]), jnp.exp(-m_t))
        hid = tot[:, :dv] / den
        gate = _sigmoid(og_ref[:, h * dv:(h + 1) * dv].astype(F32))
        o_ref[:, h * dv:(h + 1) * dv] = (gate * hid).astype(o_ref.dtype)

        m_new = jnp.maximum(b_last + m_prev, jnp.max((b_last - b_row) + i_row, axis=-1, keepdims=True))
        decay = jnp.exp(b_last + m_prev - m_new)
        w_end = jnp.exp(((b_last - b_col) + i_col) - m_new)
        vw = (v_ext.astype(F32) * w_end).astype(BF16)
        state_ref[h] = decay * state + lax.dot_general(kh, vw, TN_DIMS, preferred_element_type=F32)
        m_ref[h] = m_new


def _mlstm(proj, g, gt, batch, seq, dk, dv):
    heads = MLSTM_HEADS
    n = _tile(seq, MLSTM_L, LANES)
    nc = seq // n
    qk, vw = heads * dk, heads * dv
    assert vw % qk == 0
    r = vw // qk
    return pl.pallas_call(
        functools.partial(_mlstm_kernel, dk=dk, dv=dv),
        grid=(batch, nc),
        in_specs=[
            pl.BlockSpec((n, qk), lambda b, c: (b * nc + c, 0)),
            pl.BlockSpec((n, qk), lambda b, c: (b * nc + c, 1)),
            pl.BlockSpec((n, vw), lambda b, c: (b * nc + c, 2 // r)),
            pl.BlockSpec((n, vw), lambda b, c: (b * nc + c, 2 // r + 1)),
            pl.BlockSpec((n, LANES), lambda b, c: (b * nc + c, 0)),
            pl.BlockSpec((GATE_ROWS, n), lambda b, c: (0, b * nc + c)),
        ],
        out_specs=pl.BlockSpec((n, vw), lambda b, c: (b * nc + c, 0)),
        out_shape=jax.ShapeDtypeStruct((batch * seq, vw), BF16),
        scratch_shapes=[pltpu.VMEM((heads, dk, dv + LANES), F32), pltpu.VMEM((heads, 1, 1), F32)],
        compiler_params=_params(("arbitrary", "arbitrary")),
        name="mlstm",
    )(proj, proj, proj, proj, g, gt)


def _outproj_ln_kernel(*refs, alpha, n_in):
    h_refs, w_refs = refs[:n_in], refs[n_in:2 * n_in]
    x_ref, g_ref, b_ref, o_ref = refs[2 * n_in:]
    y = sum(jnp.dot(h[...], w[...], preferred_element_type=F32) for h, w in zip(h_refs, w_refs))
    o_ref[...] = _layer_norm(alpha * x_ref[...] + y, g_ref[...], b_ref[...])


def _out_projection_ln(hs, w, layer, x, ln_g, ln_b, alpha):
    m, d = x.shape
    bm = _tile(m, OUT_BM, LANES)
    widths = [h.shape[1] for h in hs]
    assert all(wd == widths[0] for wd in widths) and w.shape[1] == sum(widths)
    in_specs = [pl.BlockSpec((bm, wd), lambda i: (i, 0)) for wd in widths]
    in_specs += [pl.BlockSpec((None, wd, d), lambda i, k=k: (layer, k, 0)) for k, wd in enumerate(widths)]
    in_specs += [pl.BlockSpec((bm, d), lambda i: (i, 0)),
                 pl.BlockSpec((1, d), lambda i: (0, 0)),
                 pl.BlockSpec((1, d), lambda i: (0, 0))]
    return pl.pallas_call(
        functools.partial(_outproj_ln_kernel, alpha=alpha, n_in=len(hs)),
        grid=(m // bm,),
        in_specs=in_specs,
        out_specs=pl.BlockSpec((bm, d), lambda i: (i, 0)),
        out_shape=jax.ShapeDtypeStruct((m, d), F32),
        compiler_params=_params(("arbitrary",)),
        name="out_projection_ln",
    )(*hs, *([w] * len(hs)), x, ln_g, ln_b)


def _causal_conv(u, prev, w_ref, b_ref, cs):
    w0, w1, w2, b = w_ref[0:1, cs], w_ref[1:2, cs], w_ref[2:3, cs], b_ref[:, cs]
    body = b + w0 * pltpu.roll(u, 2, 0) + w1 * pltpu.roll(u, 1, 0) + w2 * u
    top = jnp.concatenate([prev, u[:HEAD_ROWS, :]], axis=0)
    head = b + w0 * pltpu.roll(top, 2, 0) + w1 * pltpu.roll(top, 1, 0) + w2 * top
    return body, head[SUBLANES:, :]


def _ffn_kernel(x_ref, wg_ref, wv_ref, cwg_ref, cwv_ref, cbg_ref, cbv_ref, wd_ref, lg_ref, lb_ref, o_ref,
                xb_ref, acc_ref, h0_ref, h1_ref, pg_ref, pv_ref, ug_ref, uv_ref, *, alpha, tiles_per_seq, nf):
    i = pl.program_id(0)
    j = pl.program_id(1)
    bm = x_ref.shape[0]
    h_refs = (h0_ref, h1_ref)

    def up(h_ref):
        xb = xb_ref[...]
        kept_rows = jnp.where(i % tiles_per_seq != 0, SUBLANES, 0)
        keep = lax.broadcasted_iota(jnp.int32, (SUBLANES, FFN_CHUNK), 0) < kept_rows
        chunks = [slice(c * FFN_CHUNK, (c + 1) * FFN_CHUNK) for c in range(h_ref.shape[1] // FFN_CHUNK)]
        prev_g = [jnp.where(keep, pg_ref[j, :, cs], 0.0) for cs in chunks]
        prev_v = [jnp.where(keep, pv_ref[j, :, cs], 0.0) for cs in chunks]
        for r in range(bm // FFN_ROWS):
            xr = xb[r * FFN_ROWS:(r + 1) * FFN_ROWS, :]
            for c, cs in enumerate(chunks):
                slot = (r * len(chunks) + c) % FFN_SLOTS
                ug_ref[slot] = jnp.dot(xr, wg_ref[:, cs], preferred_element_type=F32)
                uv_ref[slot] = jnp.dot(xr, wv_ref[:, cs], preferred_element_type=F32)
                ug, uv = ug_ref[slot], uv_ref[slot]
                cg, cg_top = _causal_conv(ug, prev_g[c], cwg_ref, cbg_ref, cs)
                cv, cv_top = _causal_conv(uv, prev_v[c], cwv_ref, cbv_ref, cs)
                prev_g[c] = ug[FFN_ROWS - SUBLANES:, :]
                prev_v[c] = uv[FFN_ROWS - SUBLANES:, :]
                h_ref[r * FFN_ROWS:(r + 1) * FFN_ROWS, cs] = (cg * _sigmoid(cg) * cv).astype(BF16)
                h_ref[r * FFN_ROWS:r * FFN_ROWS + HEAD_ROWS, cs] = (cg_top * _sigmoid(cg_top) * cv_top).astype(BF16)
        for c, cs in enumerate(chunks):
            pg_ref[j, :, cs] = prev_g[c]
            pv_ref[j, :, cs] = prev_v[c]

    def down(h_ref):
        acc_ref[...] += jnp.dot(h_ref[...], wd_ref[...], preferred_element_type=F32)

    @pl.when(j == 0)
    def _():
        xb_ref[...] = x_ref[...].astype(BF16)
        acc_ref[...] = jnp.zeros_like(acc_ref)
        up(h_refs[0])

    for parity in range(2):
        @pl.when(jnp.logical_and(jnp.logical_and(j > 0, j < nf), j % 2 == parity))
        def _():
            up(h_refs[parity])
            down(h_refs[1 - parity])

    @pl.when(j == nf)
    def _():
        down(h_refs[(nf - 1) % 2])
        o_ref[...] = _layer_norm(alpha * x_ref[...] + acc_ref[...], lg_ref[...], lb_ref[...])


def _ffn_up_blocks(w_up, bf):
    layers, d, f2 = w_up.shape
    return w_up.reshape(layers, d, 2, f2 // (2 * bf), bf).transpose(0, 2, 3, 1, 4).astype(BF16)


def _conv_ffn_ln(x, w_up, w_down, layer, conv_w, conv_b, ln_g, ln_b, alpha, seq):
    m, d = x.shape
    bm = _tile(seq, FFN_BM, 2 * SUBLANES)
    nf, bf = w_up.shape[2], w_up.shape[4]

    def up_blk(j):
        return jnp.minimum(j, nf - 1)

    def down_blk(j):
        return jnp.maximum(j - 1, 0)

    return pl.pallas_call(
        functools.partial(_ffn_kernel, alpha=alpha, tiles_per_seq=seq // bm, nf=nf),
        grid=(m // bm, nf + 1),
        in_specs=[
            pl.BlockSpec((bm, d), lambda i, j: (i, 0)),
            pl.BlockSpec((None, None, None, d, bf), lambda i, j: (layer, 0, up_blk(j), 0, 0)),
            pl.BlockSpec((None, None, None, d, bf), lambda i, j: (layer, 1, up_blk(j), 0, 0)),
            pl.BlockSpec((CONV_WIDTH, bf), lambda i, j: (0, up_blk(j))),
            pl.BlockSpec((CONV_WIDTH, bf), lambda i, j: (0, nf + up_blk(j))),
            pl.BlockSpec((1, bf), lambda i, j: (0, up_blk(j))),
            pl.BlockSpec((1, bf), lambda i, j: (0, nf + up_blk(j))),
            pl.BlockSpec((None, bf, d), lambda i, j: (layer, down_blk(j), 0)),
            pl.BlockSpec((1, d), lambda i, j: (0, 0)),
            pl.BlockSpec((1, d), lambda i, j: (0, 0)),
        ],
        out_specs=pl.BlockSpec((bm, d), lambda i, j: (i, 0)),
        out_shape=jax.ShapeDtypeStruct((m, d), F32),
        scratch_shapes=[
            pltpu.VMEM((bm, d), BF16),
            pltpu.VMEM((bm, d), F32),
            pltpu.VMEM((bm, bf), BF16),
            pltpu.VMEM((bm, bf), BF16),
            pltpu.VMEM((nf, SUBLANES, bf), F32),
            pltpu.VMEM((nf, SUBLANES, bf), F32),
            pltpu.VMEM((FFN_SLOTS, FFN_ROWS, FFN_CHUNK), F32),
            pltpu.VMEM((FFN_SLOTS, FFN_ROWS, FFN_CHUNK), F32),
        ],
        compiler_params=_params(("arbitrary", "arbitrary")),
        name="conv_ffn_ln",
    )(x, w_up, w_up, conv_w, conv_w, conv_b, conv_b, w_down, ln_g, ln_b)


def _gate_params(w, b):
    d, n = w.shape
    wg = jnp.zeros((d, LANES), BF16).at[:, :n].set(w.astype(BF16))
    bg = jnp.zeros((1, LANES), F32).at[0, :n].set(b)
    wgt = jnp.zeros((GATE_ROWS, d), BF16).at[:n, :].set(w.T.astype(BF16))
    bgt = jnp.zeros((GATE_ROWS, 1), F32).at[:n, 0].set(b)
    return wg, bg, wgt, bgt


def _rope_tables(seq):
    half = HEAD_DIM // 2
    inv_freq = jnp.power(ROPE_THETA, -jnp.arange(half, dtype=F32) * (2.0 / HEAD_DIM))
    ang = jnp.arange(seq, dtype=F32)[:, None] * inv_freq[None, :]
    cos, sin = jnp.cos(ang), jnp.sin(ang)
    return jnp.concatenate([cos, cos], axis=-1), jnp.concatenate([-sin, sin], axis=-1)


def kernel(x, attn_w_in, attn_b_in, attn_sinks, attn_w_out, mlstm_w_in, mlstm_b_in, mlstm_w_out, ffn_w_up,
           ffn_conv_w, ffn_conv_b, ffn_w_down, ln1_g, ln1_b, ln2_g, ln2_b):
    batch, seq, d = x.shape
    depth = ln1_g.shape[0]
    alpha = float((2 * depth) ** 0.25)
    fox_heads = d // (2 * HEAD_DIM)
    swa_heads = d // (2 * HEAD_DIM)
    fox_dim = fox_heads * HEAD_DIM
    fox_f_off = 3 * fox_dim
    dk, dv = d // (2 * MLSTM_HEADS), d // MLSTM_HEADS
    mlstm_main = 2 * MLSTM_HEADS * dk + 2 * MLSTM_HEADS * dv
    cos, sin_signed = _rope_tables(seq)

    gate_lo, gate_hi = fox_f_off, fox_f_off + fox_heads
    attn_w_main = jnp.concatenate([attn_w_in[:, :, :gate_lo], attn_w_in[:, :, gate_hi:]], axis=2).astype(BF16)
    attn_b_main = jnp.concatenate([attn_b_in[:, :gate_lo], attn_b_in[:, gate_hi:]], axis=1)
    attn_w_out_b = attn_w_out.astype(BF16)
    mlstm_w_main = mlstm_w_in[:, :, :mlstm_main].astype(BF16)
    mlstm_w_out_b = mlstm_w_out.astype(BF16)
    ffn_w_up_b = _ffn_up_blocks(ffn_w_up, _tile(ffn_w_down.shape[1], FFN_BF, LANES))
    ffn_w_down_b = ffn_w_down.astype(BF16)

    h = x.reshape(batch * seq, d)
    for layer in range(depth):
        j = layer // 2
        if layer % 2 == 0:
            gates = _gate_params(attn_w_in[j][:, gate_lo:gate_hi], attn_b_in[j][gate_lo:gate_hi])
            proj, g, gt = _in_projection(h, attn_w_main, j, attn_b_main[j][None, :], *gates)
            fox = _fox_attention(proj, _fox_gate_cumsum(g, batch, seq), batch, seq, fox_heads)
            swa_q = 3 * fox_dim
            swa_k = swa_q + swa_heads * HEAD_DIM
            swa_v = swa_k + SWA_KV_HEADS * HEAD_DIM
            swa = _swa_attention(proj, attn_sinks[j], cos, sin_signed, batch, seq, swa_q, swa_k, swa_v, swa_heads)
            h = _out_projection_ln([fox, swa], attn_w_out_b, j, h, ln1_g[layer][None, :], ln1_b[layer][None, :], alpha)
        else:
            gates = _gate_params(mlstm_w_in[j][:, mlstm_main:], mlstm_b_in[j][mlstm_main:])
            proj, g, gt = _in_projection(h, mlstm_w_main, j, mlstm_b_in[j][None, :mlstm_main], *gates)
            mixed = _mlstm(proj, g, gt, batch, seq, dk, dv)
            h = _out_projection_ln([mixed], mlstm_w_out_b, j, h, ln1_g[layer][None, :], ln1_b[layer][None, :], alpha)
        h = _conv_ffn_ln(h, ffn_w_up_b, ffn_w_down_b, layer, ffn_conv_w[layer], ffn_conv_b[layer][None, :],
                         ln2_g[layer][None, :], ln2_b[layer][None, :], alpha, seq)
    return h.reshape(batch, seq, d)
```

```python
import functools

import jax
import jax.numpy as jnp
from jax import lax
from jax.experimental import pallas as pl
from jax.experimental.pallas import tpu as pltpu

F32 = jnp.float32
BF16 = jnp.bfloat16

HEAD_DIM = 128
SWA_KV_HEADS = 2
SWA_WINDOW = 128
ROPE_THETA = 10000.0
MLSTM_HEADS = 8
CONV_WIDTH = 3
LN_EPS = 1e-5
LOG2_E = 1.4426950408889634

LANES = 128
SUBLANES = 8
GATE_ROWS = 16
HEAD_ROWS = 16
VMEM_LIMIT_BYTES = 60 * 1024 * 1024

PROJ_BM = 1024
PROJ_BN = 1536
OUT_BM = 512
FFN_BM = 1024
FFN_BF = 512
FFN_CHUNK = 256
FFN_ROWS = 256
FFN_SLOTS = 2
FOX_T = 512
FOX_GROUP = 2
SWA_T = 512
CUM_T = 512
MLSTM_L = 256

NT_DIMS = (((1,), (1,)), ((), ()))
TN_DIMS = (((0,), (0,)), ((), ()))


def _tile(n, pref, unit):
    t = min(pref, n)
    while n % t or t % unit:
        t -= unit
    assert t > 0, (n, pref, unit)
    return t


def _params(sem):
    return pltpu.CompilerParams(dimension_semantics=sem, vmem_limit_bytes=VMEM_LIMIT_BYTES)


def _log_sigmoid(x):
    return jnp.minimum(x, 0.0) - jnp.log(1.0 + jnp.exp(-jnp.abs(x)))


def _sigmoid(x):
    return 1.0 / (1.0 + jnp.exp(-x))


def _split3(x):
    h1 = x.astype(BF16)
    r1 = x - h1.astype(F32)
    h2 = r1.astype(BF16)
    h3 = (r1 - h2.astype(F32)).astype(BF16)
    return h1, h2, h3


def _cumsum_rows(tri, x):
    return sum(jnp.dot(tri, h, preferred_element_type=F32) for h in _split3(x))


def _cumsum_lanes(x, tri_t):
    return sum(jnp.dot(h, tri_t, preferred_element_type=F32) for h in _split3(x))


def _layer_norm(z, g, b):
    mu = jnp.mean(z, axis=-1, keepdims=True)
    zc = z - mu
    var = jnp.mean(zc * zc, axis=-1, keepdims=True)
    return zc * lax.rsqrt(var + LN_EPS) * g + b


def _proj_kernel(x_ref, w_ref, b_ref, wg_ref, bg_ref, wgt_ref, bgt_ref, o_ref, g_ref, gt_ref, xb_ref):
    @pl.when(pl.program_id(1) == 0)
    def _():
        xb = x_ref[...].astype(BF16)
        xb_ref[...] = xb
        g_ref[...] = jnp.dot(xb, wg_ref[...], preferred_element_type=F32) + bg_ref[...]
        gt_ref[...] = lax.dot_general(wgt_ref[...], xb, NT_DIMS, preferred_element_type=F32) + bgt_ref[...]

    acc = jnp.dot(xb_ref[...], w_ref[...], preferred_element_type=F32)
    o_ref[...] = (acc + b_ref[...]).astype(o_ref.dtype)


def _in_projection(x, w, layer, b, wg, bg, wgt, bgt):
    m, d = x.shape
    n = w.shape[2]
    bm = _tile(m, PROJ_BM, LANES)
    bn = _tile(n, PROJ_BN, LANES)
    return pl.pallas_call(
        _proj_kernel,
        grid=(m // bm, n // bn),
        in_specs=[
            pl.BlockSpec((bm, d), lambda i, j: (i, 0)),
            pl.BlockSpec((None, d, bn), lambda i, j: (layer, 0, j)),
            pl.BlockSpec((1, bn), lambda i, j: (0, j)),
            pl.BlockSpec((d, LANES), lambda i, j: (0, 0)),
            pl.BlockSpec((1, LANES), lambda i, j: (0, 0)),
            pl.BlockSpec((GATE_ROWS, d), lambda i, j: (0, 0)),
            pl.BlockSpec((GATE_ROWS, 1), lambda i, j: (0, 0)),
        ],
        out_specs=[
            pl.BlockSpec((bm, bn), lambda i, j: (i, j)),
            pl.BlockSpec((bm, LANES), lambda i, j: (i, 0)),
            pl.BlockSpec((GATE_ROWS, bm), lambda i, j: (0, i)),
        ],
        out_shape=[
            jax.ShapeDtypeStruct((m, n), BF16),
            jax.ShapeDtypeStruct((m, LANES), F32),
            jax.ShapeDtypeStruct((GATE_ROWS, m), F32),
        ],
        scratch_shapes=[pltpu.VMEM((bm, d), BF16)],
        compiler_params=_params(("arbitrary", "arbitrary")),
        name="in_projection",
    )(x, w, b, wg, bg, wgt, bgt)


def _fox_gate_kernel(g_ref, c_ref, carry_ref):
    @pl.when(pl.program_id(1) == 0)
    def _():
        carry_ref[...] = jnp.zeros_like(carry_ref)

    t = g_ref.shape[0]
    row = lax.broadcasted_iota(jnp.int32, (t, t), 0)
    col = lax.broadcasted_iota(jnp.int32, (t, t), 1)
    c = _cumsum_rows((col <= row).astype(BF16), _log_sigmoid(g_ref[...])) + carry_ref[...]
    c_ref[...] = c
    carry_ref[...] = c[t - 1:t, :]


def _fox_gate_cumsum(g, batch, seq):
    t = _tile(seq, CUM_T, LANES)
    ns = seq // t
    return pl.pallas_call(
        _fox_gate_kernel,
        grid=(batch, ns),
        in_specs=[pl.BlockSpec((t, LANES), lambda b, s: (b * ns + s, 0))],
        out_specs=pl.BlockSpec((t, LANES), lambda b, s: (b * ns + s, 0)),
        out_shape=jax.ShapeDtypeStruct((batch * seq, LANES), F32),
        scratch_shapes=[pltpu.VMEM((1, LANES), F32)],
        compiler_params=_params(("arbitrary", "arbitrary")),
        name="fox_gate_cumsum",
    )(g)


def _bias_lanes(c, ones_first):
    n = c.shape[0]
    h1, h2, h3 = (h.astype(F32) for h in _split3(c))
    lane = lax.broadcasted_iota(jnp.int32, (n, HEAD_DIM), 1)
    lo = 3 if ones_first else 0
    split = jnp.where(lane == lo, h1, jnp.where(lane == lo + 1, h2, jnp.where(lane == lo + 2, h3, 0.0)))
    ones = jnp.logical_and(lane >= 3 - lo, lane < 6 - lo)
    return jnp.where(ones, 1.0, split).astype(BF16)


def _fox_kernel(q_ref, k_ref, v_ref, ct_ref, cs_ref, o_ref, kb_ref, m_ref, acc_ref, s_ref, *, scale, group):
    first_head = pl.program_id(1) * group
    qi = pl.program_id(2)
    t = q_ref.shape[0]

    def head_lane(c, g):
        lane = lax.broadcasted_iota(jnp.int32, c.shape, 1)
        return jnp.sum(jnp.where(lane == first_head + g, c, 0.0), axis=-1, keepdims=True) * LOG2_E

    @pl.when(qi == 0)
    def _():
        cs = cs_ref[...]
        for g in range(group):
            kb_ref[g] = _bias_lanes(-head_lane(cs, g), ones_first=False)

    ct = ct_ref[...]
    unit = (lax.broadcasted_iota(jnp.int32, (t, HEAD_DIM), 1) == 0).astype(BF16)
    heads = []
    for g in range(group):
        sl = slice(g * HEAD_DIM, (g + 1) * HEAD_DIM)
        q = (q_ref[:, sl].astype(F32) * (scale * LOG2_E)).astype(BF16)
        heads.append((sl, jnp.concatenate([q, _bias_lanes(head_lane(ct, g), ones_first=True)], axis=1)))

    def logits(slot, ki):
        start = pl.multiple_of(ki * t, t)
        for g, (sl, q_ext) in enumerate(heads):
            k_ext = jnp.concatenate([k_ref[pl.ds(start, t), sl], kb_ref[g, pl.ds(start, t), :]], axis=1)
            s_ref[slot, g] = lax.dot_general(q_ext, k_ext, NT_DIMS, preferred_element_type=F32)

    def absorb(slot, ki, diagonal):
        start = pl.multiple_of(ki * t, t)
        for g, (sl, _) in enumerate(heads):
            s = s_ref[slot, g]
            if diagonal:
                row = lax.broadcasted_iota(jnp.int32, (t, t), 0)
                col = lax.broadcasted_iota(jnp.int32, (t, t), 1)
                s = jnp.where(col <= row, s, -jnp.inf)
            m = m_ref[g]
            m_new = jnp.maximum(m, jnp.max(s, axis=-1, keepdims=True))
            p = jnp.exp2(s - m_new).astype(BF16)
            v_ext = jnp.concatenate([v_ref[pl.ds(start, t), sl], unit], axis=1)
            acc_ref[g] = jnp.exp2(m - m_new) * acc_ref[g] + jnp.dot(p, v_ext, preferred_element_type=F32)
            m_ref[g] = m_new

    m_ref[...] = jnp.full(m_ref.shape, -jnp.inf, F32)
    acc_ref[...] = jnp.zeros_like(acc_ref)

    logits(0, 0)

    @pl.loop(0, lax.shift_right_logical(qi, 1))
    def _(pair):
        ki = 2 * pair
        logits(1, ki + 1)
        absorb(0, ki, False)
        logits(0, ki + 2)
        absorb(1, ki + 1, False)

    odd = lax.rem(qi, 2) == 1

    @pl.when(odd)
    def _():
        logits(1, qi)
        absorb(0, qi - 1, False)
        absorb(1, qi, True)

    @pl.when(jnp.logical_not(odd))
    def _():
        absorb(0, qi, True)

    for g, (sl, _) in enumerate(heads):
        acc = acc_ref[g]
        o_ref[:, sl] = (acc[:, :HEAD_DIM] / acc[:, HEAD_DIM:HEAD_DIM + 1]).astype(o_ref.dtype)


def _fox_attention(proj, c, batch, seq, heads):
    t = _tile(seq, FOX_T, LANES)
    nq = seq // t
    group = FOX_GROUP
    assert heads % group == 0
    hb = heads // group
    gw = group * HEAD_DIM
    return pl.pallas_call(
        functools.partial(_fox_kernel, scale=HEAD_DIM ** -0.5, group=group),
        grid=(batch, hb, nq),
        in_specs=[
            pl.BlockSpec((t, gw), lambda b, h, i: (b * nq + i, h)),
            pl.BlockSpec((seq, gw), lambda b, h, i: (b, hb + h)),
            pl.BlockSpec((seq, gw), lambda b, h, i: (b, 2 * hb + h)),
            pl.BlockSpec((t, LANES), lambda b, h, i: (b * nq + i, 0)),
            pl.BlockSpec((seq, LANES), lambda b, h, i: (b, 0)),
        ],
        out_specs=pl.BlockSpec((t, gw), lambda b, h, i: (b * nq + i, h)),
        out_shape=jax.ShapeDtypeStruct((batch * seq, heads * HEAD_DIM), BF16),
        scratch_shapes=[pltpu.VMEM((group, seq, HEAD_DIM), BF16),
                        pltpu.VMEM((group, t, 1), F32),
                        pltpu.VMEM((group, t, 2 * HEAD_DIM), F32),
                        pltpu.VMEM((2, group, t, t), F32)],
        compiler_params=_params(("arbitrary", "arbitrary", "arbitrary")),
        name="fox_attention",
    )(proj, proj, proj, c, c)


def _rope(x, cos, sin_signed):
    xf = x.astype(F32)
    return xf * cos + pltpu.roll(xf, HEAD_DIM // 2, 1) * sin_signed


def _swa_kernel(sink_ref, q_ref, k_ref, v_ref, kp_ref, vp_ref, cos_ref, sin_ref, cosp_ref, sinp_ref, o_ref,
                *, scale, group):
    w = SWA_WINDOW
    first_visible = jnp.where(pl.program_id(1) == 0, w, 0)
    nblk = q_ref.shape[0] // w
    cos, sin = cos_ref[...], sin_ref[...]
    cosp, sinp = cosp_ref[...], sinp_ref[...]
    row = lax.broadcasted_iota(jnp.int32, (w, 2 * w), 0)
    col = lax.broadcasted_iota(jnp.int32, (w, 2 * w), 1)
    valid = jnp.logical_and(col > row, col - w <= row)
    valid_first = jnp.logical_and(valid, col >= first_visible)
    for kv in range(SWA_KV_HEADS):
        ksl = slice(kv * HEAD_DIM, (kv + 1) * HEAD_DIM)
        k_all = jnp.concatenate([_rope(kp_ref[:, ksl], cosp, sinp), _rope(k_ref[:, ksl], cos, sin)],
                                axis=0).astype(BF16)
        v_all = jnp.concatenate([vp_ref[:, ksl], v_ref[:, ksl]], axis=0)
        for g in range(group):
            hq = kv * group + g
            qsl = slice(hq * HEAD_DIM, (hq + 1) * HEAD_DIM)
            q = _rope(q_ref[:, qsl], cos, sin).astype(BF16)
            sink = sink_ref[hq]
            for blk in range(nblk):
                qb = q[blk * w:(blk + 1) * w, :]
                kb = k_all[blk * w:(blk + 2) * w, :]
                vb = v_all[blk * w:(blk + 2) * w, :]
                s = lax.dot_general(qb, kb, NT_DIMS, preferred_element_type=F32) * scale
                s = jnp.where(valid_first if blk == 0 else valid, s, -jnp.inf)
                m = jnp.maximum(jnp.max(s, axis=-1, keepdims=True), sink)
                p = jnp.exp(s - m)
                denom = jnp.sum(p, axis=-1, keepdims=True) + jnp.exp(sink - m)
                o = jnp.dot((p / denom).astype(BF16), vb, preferred_element_type=F32)
                o_ref[blk * w:(blk + 1) * w, qsl] = o.astype(o_ref.dtype)


def _swa_attention(proj, sinks, cos, sin_signed, batch, seq, q_col, k_col, v_col, q_heads):
    t = _tile(seq, SWA_T, SWA_WINDOW)
    nt = seq // t
    per = t // SWA_WINDOW
    qw = q_heads * HEAD_DIM
    kvw = SWA_KV_HEADS * HEAD_DIM
    assert q_col % qw == 0 and k_col % kvw == 0 and v_col % kvw == 0

    def prev(b, i):
        return jnp.maximum((b * nt + i) * per - 1, 0)

    def prev_pos(i):
        return jnp.maximum(i * per - 1, 0)

    return pl.pallas_call(
        functools.partial(_swa_kernel, scale=HEAD_DIM ** -0.5, group=q_heads // SWA_KV_HEADS),
        grid=(batch, nt),
        in_specs=[
            pl.BlockSpec(memory_space=pltpu.SMEM),
            pl.BlockSpec((t, qw), lambda b, i: (b * nt + i, q_col // qw)),
            pl.BlockSpec((t, kvw), lambda b, i: (b * nt + i, k_col // kvw)),
            pl.BlockSpec((t, kvw), lambda b, i: (b * nt + i, v_col // kvw)),
            pl.BlockSpec((SWA_WINDOW, kvw), lambda b, i: (prev(b, i), k_col // kvw)),
            pl.BlockSpec((SWA_WINDOW, kvw), lambda b, i: (prev(b, i), v_col // kvw)),
            pl.BlockSpec((t, HEAD_DIM), lambda b, i: (i, 0)),
            pl.BlockSpec((t, HEAD_DIM), lambda b, i: (i, 0)),
            pl.BlockSpec((SWA_WINDOW, HEAD_DIM), lambda b, i: (prev_pos(i), 0)),
            pl.BlockSpec((SWA_WINDOW, HEAD_DIM), lambda b, i: (prev_pos(i), 0)),
        ],
        out_specs=pl.BlockSpec((t, qw), lambda b, i: (b * nt + i, 0)),
        out_shape=jax.ShapeDtypeStruct((batch * seq, qw), BF16),
        compiler_params=_params(("arbitrary", "arbitrary")),
        name="swa_attention",
    )(sinks, proj, proj, proj, proj, proj, cos, sin_signed, cos, sin_signed)


def _mlstm_kernel(q_ref, k_ref, v_ref, og_ref, g_ref, gt_ref, o_ref, state_ref, m_ref, *, dk, dv):
    heads = MLSTM_HEADS

    @pl.when(pl.program_id(1) == 0)
    def _():
        state_ref[...] = jnp.zeros_like(state_ref)
        m_ref[...] = jnp.zeros_like(m_ref)

    n = q_ref.shape[0]
    g = g_ref[...]
    gt = gt_ref[...]
    row = lax.broadcasted_iota(jnp.int32, (n, n), 0)
    col = lax.broadcasted_iota(jnp.int32, (n, n), 1)
    causal = col <= row
    b_col_all = _cumsum_rows(causal.astype(BF16), _log_sigmoid(g))
    b_row_all = _cumsum_lanes(_log_sigmoid(gt), (row <= col).astype(BF16))
    unit = (lax.broadcasted_iota(jnp.int32, (n, LANES), 1) == 0).astype(BF16)

    for h in range(heads):
        i_col = g[:, h:h + 1]
        b_col = b_col_all[:, heads + h:heads + h + 1]
        i_row = gt[h:h + 1, :]
        b_row = b_row_all[heads + h:heads + h + 1, :]
        b_last = b_row[:, n - 1:n]
        m_prev = m_ref[h]

        d = jnp.where(causal, (b_col - b_row) + i_row, -jnp.inf)
        m_inter = b_col + m_prev
        m_t = jnp.maximum(m_inter, jnp.max(d, axis=-1, keepdims=True))
        inter = jnp.exp(m_inter - m_t)
        wgt = jnp.exp(d - m_t)

        qh = q_ref[:, h * dk:(h + 1) * dk]
        kh = (k_ref[:, h * dk:(h + 1) * dk].astype(F32) * (dk ** -0.5)).astype(BF16)
        v_ext = jnp.concatenate([v_ref[:, h * dv:(h + 1) * dv], unit], axis=1)
        state = state_ref[h]

        sm = wgt * lax.dot_general(qh, kh, NT_DIMS, preferred_element_type=F32)
        tot = inter * jnp.dot(qh, state.astype(BF16), preferred_element_type=F32)
        tot = tot + jnp.dot(sm.astype(BF16), v_ext, preferred_element_type=F32)
        den = jnp.maximum(jnp.abs(tot[:, dv:dv + 1]), jnp.exp(-m_t))
        hid = tot[:, :dv] / den
        gate = _sigmoid(og_ref[:, h * dv:(h + 1) * dv].astype(F32))
        o_ref[:, h * dv:(h + 1) * dv] = (gate * hid).astype(o_ref.dtype)

        m_new = jnp.maximum(b_last + m_prev, jnp.max((b_last - b_row) + i_row, axis=-1, keepdims=True))
        decay = jnp.exp(b_last + m_prev - m_new)
        w_end = jnp.exp(((b_last - b_col) + i_col) - m_new)
        vw = (v_ext.astype(F32) * w_end).astype(BF16)
        state_ref[h] = decay * state + lax.dot_general(kh, vw, TN_DIMS, preferred_element_type=F32)
        m_ref[h] = m_new


def _mlstm(proj, g, gt, batch, seq, dk, dv):
    heads = MLSTM_HEADS
    n = _tile(seq, MLSTM_L, LANES)
    nc = seq // n
    qk, vw = heads * dk, heads * dv
    assert vw % qk == 0
    r = vw // qk
    return pl.pallas_call(
        functools.partial(_mlstm_kernel, dk=dk, dv=dv),
        grid=(batch, nc),
        in_specs=[
            pl.BlockSpec((n, qk), lambda b, c: (b * nc + c, 0)),
            pl.BlockSpec((n, qk), lambda b, c: (b * nc + c, 1)),
            pl.BlockSpec((n, vw), lambda b, c: (b * nc + c, 2 // r)),
            pl.BlockSpec((n, vw), lambda b, c: (b * nc + c, 2 // r + 1)),
            pl.BlockSpec((n, LANES), lambda b, c: (b * nc + c, 0)),
            pl.BlockSpec((GATE_ROWS, n), lambda b, c: (0, b * nc + c)),
        ],
        out_specs=pl.BlockSpec((n, vw), lambda b, c: (b * nc + c, 0)),
        out_shape=jax.ShapeDtypeStruct((batch * seq, vw), BF16),
        scratch_shapes=[pltpu.VMEM((heads, dk, dv + LANES), F32), pltpu.VMEM((heads, 1, 1), F32)],
        compiler_params=_params(("arbitrary", "arbitrary")),
        name="mlstm",
    )(proj, proj, proj, proj, g, gt)


def _outproj_ln_kernel(*refs, alpha, n_in):
    h_refs, w_refs = refs[:n_in], refs[n_in:2 * n_in]
    x_ref, g_ref, b_ref, o_ref = refs[2 * n_in:]
    y = sum(jnp.dot(h[...], w[...], preferred_element_type=F32) for h, w in zip(h_refs, w_refs))
    o_ref[...] = _layer_norm(alpha * x_ref[...] + y, g_ref[...], b_ref[...])


def _out_projection_ln(hs, w, layer, x, ln_g, ln_b, alpha):
    m, d = x.shape
    bm = _tile(m, OUT_BM, LANES)
    widths = [h.shape[1] for h in hs]
    assert all(wd == widths[0] for wd in widths) and w.shape[1] == sum(widths)
    in_specs = [pl.BlockSpec((bm, wd), lambda i: (i, 0)) for wd in widths]
    in_specs += [pl.BlockSpec((None, wd, d), lambda i, k=k: (layer, k, 0)) for k, wd in enumerate(widths)]
    in_specs += [pl.BlockSpec((bm, d), lambda i: (i, 0)),
                 pl.BlockSpec((1, d), lambda i: (0, 0)),
                 pl.BlockSpec((1, d), lambda i: (0, 0))]
    return pl.pallas_call(
        functools.partial(_outproj_ln_kernel, alpha=alpha, n_in=len(hs)),
        grid=(m // bm,),
        in_specs=in_specs,
        out_specs=pl.BlockSpec((bm, d), lambda i: (i, 0)),
        out_shape=jax.ShapeDtypeStruct((m, d), F32),
        compiler_params=_params(("arbitrary",)),
        name="out_projection_ln",
    )(*hs, *([w] * len(hs)), x, ln_g, ln_b)


def _causal_conv(u, prev, w_ref, b_ref, cs):
    w0, w1, w2, b = w_ref[0:1, cs], w_ref[1:2, cs], w_ref[2:3, cs], b_ref[:, cs]
    body = b + w0 * pltpu.roll(u, 2, 0) + w1 * pltpu.roll(u, 1, 0) + w2 * u
    top = jnp.concatenate([prev, u[:HEAD_ROWS, :]], axis=0)
    head = b + w0 * pltpu.roll(top, 2, 0) + w1 * pltpu.roll(top, 1, 0) + w2 * top
    return body, head[SUBLANES:, :]


def _ffn_kernel(x_ref, wg_ref, wv_ref, cwg_ref, cwv_ref, cbg_ref, cbv_ref, wd_ref, lg_ref, lb_ref, o_ref,
                xb_ref, h0_ref, h1_ref, pg_ref, pv_ref, ug_ref, uv_ref, *, alpha, tiles_per_seq, nf):
    i = pl.program_id(0)
    j = pl.program_id(1)
    bm = x_ref.shape[0]
    h_refs = (h0_ref, h1_ref)

    def up(h_ref):
        xb = xb_ref[...]
        kept_rows = jnp.where(i % tiles_per_seq != 0, SUBLANES, 0)
        keep = lax.broadcasted_iota(jnp.int32, (SUBLANES, FFN_CHUNK), 0) < kept_rows
        chunks = [slice(c * FFN_CHUNK, (c + 1) * FFN_CHUNK) for c in range(h_ref.shape[1] // FFN_CHUNK)]
        prev_g = [jnp.where(keep, pg_ref[j, :, cs], 0.0) for cs in chunks]
        prev_v = [jnp.where(keep, pv_ref[j, :, cs], 0.0) for cs in chunks]
        for r in range(bm // FFN_ROWS):
            xr = xb[r * FFN_ROWS:(r + 1) * FFN_ROWS, :]
            for c, cs in enumerate(chunks):
                slot = (r * len(chunks) + c) % FFN_SLOTS
                ug_ref[slot] = jnp.dot(xr, wg_ref[:, cs], preferred_element_type=F32)
                uv_ref[slot] = jnp.dot(xr, wv_ref[:, cs], preferred_element_type=F32)
                ug, uv = ug_ref[slot], uv_ref[slot]
                cg, cg_top = _causal_conv(ug, prev_g[c], cwg_ref, cbg_ref, cs)
                cv, cv_top = _causal_conv(uv, prev_v[c], cwv_ref, cbv_ref, cs)
                prev_g[c] = ug[FFN_ROWS - SUBLANES:, :]
                prev_v[c] = uv[FFN_ROWS - SUBLANES:, :]
                h_ref[r * FFN_ROWS:(r + 1) * FFN_ROWS, cs] = (cg * _sigmoid(cg) * cv).astype(BF16)
                h_ref[r * FFN_ROWS:r * FFN_ROWS + HEAD_ROWS, cs] = (cg_top * _sigmoid(cg_top) * cv_top).astype(BF16)
        for c, cs in enumerate(chunks):
            pg_ref[j, :, cs] = prev_g[c]
            pv_ref[j, :, cs] = prev_v[c]

    def down(h_ref):
        o_ref[...] += jnp.dot(h_ref[...], wd_ref[...], preferred_element_type=F32)

    @pl.when(j == 0)
    def _():
        xb_ref[...] = x_ref[...].astype(BF16)
        o_ref[...] = jnp.zeros_like(o_ref)
        up(h_refs[0])

    for parity in range(2):
        @pl.when(jnp.logical_and(jnp.logical_and(j > 0, j < nf), j % 2 == parity))
        def _():
            up(h_refs[parity])
            down(h_refs[1 - parity])

    @pl.when(j == nf)
    def _():
        down(h_refs[(nf - 1) % 2])
        o_ref[...] = _layer_norm(alpha * x_ref[...] + o_ref[...], lg_ref[...], lb_ref[...])


def _ffn_up_blocks(w_up, bf):
    layers, d, f2 = w_up.shape
    return w_up.reshape(layers, d, 2, f2 // (2 * bf), bf).transpose(0, 2, 3, 1, 4).astype(BF16)


def _conv_ffn_ln(x, w_up, w_down, layer, conv_w, conv_b, ln_g, ln_b, alpha, seq):
    m, d = x.shape
    bm = _tile(seq, FFN_BM, 2 * SUBLANES)
    nf, bf = w_up.shape[2], w_up.shape[4]

    def up_blk(j):
        return jnp.minimum(j, nf - 1)

    def down_blk(j):
        return jnp.maximum(j - 1, 0)

    return pl.pallas_call(
        functools.partial(_ffn_kernel, alpha=alpha, tiles_per_seq=seq // bm, nf=nf),
        grid=(m // bm, nf + 1),
        in_specs=[
            pl.BlockSpec((bm, d), lambda i, j: (i, 0)),
            pl.BlockSpec((None, None, None, d, bf), lambda i, j: (layer, 0, up_blk(j), 0, 0)),
            pl.BlockSpec((None, None, None, d, bf), lambda i, j: (layer, 1, up_blk(j), 0, 0)),
            pl.BlockSpec((CONV_WIDTH, bf), lambda i, j: (0, up_blk(j))),
            pl.BlockSpec((CONV_WIDTH, bf), lambda i, j: (0, nf + up_blk(j))),
            pl.BlockSpec((1, bf), lambda i, j: (0, up_blk(j))),
            pl.BlockSpec((1, bf), lambda i, j: (0, nf + up_blk(j))),
            pl.BlockSpec((None, bf, d), lambda i, j: (layer, down_blk(j), 0)),
            pl.BlockSpec((1, d), lambda i, j: (0, 0)),
            pl.BlockSpec((1, d), lambda i, j: (0, 0)),
        ],
        out_specs=pl.BlockSpec((bm, d), lambda i, j: (i, 0)),
        out_shape=jax.ShapeDtypeStruct((m, d), F32),
        scratch_shapes=[
            pltpu.VMEM((bm, d), BF16),
            pltpu.VMEM((bm, bf), BF16),
            pltpu.VMEM((bm, bf), BF16),
            pltpu.VMEM((nf, SUBLANES, bf), F32),
            pltpu.VMEM((nf, SUBLANES, bf), F32),
            pltpu.VMEM((FFN_SLOTS, FFN_ROWS, FFN_CHUNK), F32),
            pltpu.VMEM((FFN_SLOTS, FFN_ROWS, FFN_CHUNK), F32),
        ],
        compiler_params=_params(("arbitrary", "arbitrary")),
        name="conv_ffn_ln",
    )(x, w_up, w_up, conv_w, conv_w, conv_b, conv_b, w_down, ln_g, ln_b)


def _gate_params(w, b):
    d, n = w.shape
    wg = jnp.zeros((d, LANES), BF16).at[:, :n].set(w.astype(BF16))
    bg = jnp.zeros((1, LANES), F32).at[0, :n].set(b)
    wgt = jnp.zeros((GATE_ROWS, d), BF16).at[:n, :].set(w.T.astype(BF16))
    bgt = jnp.zeros((GATE_ROWS, 1), F32).at[:n, 0].set(b)
    return wg, bg, wgt, bgt


def _rope_tables(seq):
    half = HEAD_DIM // 2
    inv_freq = jnp.power(ROPE_THETA, -jnp.arange(half, dtype=F32) * (2.0 / HEAD_DIM))
    ang = jnp.arange(seq, dtype=F32)[:, None] * inv_freq[None, :]
    cos, sin = jnp.cos(ang), jnp.sin(ang)
    return jnp.concatenate([cos, cos], axis=-1), jnp.concatenate([-sin, sin], axis=-1)


def kernel(x, attn_w_in, attn_b_in, attn_sinks, attn_w_out, mlstm_w_in, mlstm_b_in, mlstm_w_out, ffn_w_up,
           ffn_conv_w, ffn_conv_b, ffn_w_down, ln1_g, ln1_b, ln2_g, ln2_b):
    batch, seq, d = x.shape
    depth = ln1_g.shape[0]
    alpha = float((2 * depth) ** 0.25)
    fox_heads = d // (2 * HEAD_DIM)
    swa_heads = d // (2 * HEAD_DIM)
    fox_dim = fox_heads * HEAD_DIM
    fox_f_off = 3 * fox_dim
    dk, dv = d // (2 * MLSTM_HEADS), d // MLSTM_HEADS
    mlstm_main = 2 * MLSTM_HEADS * dk + 2 * MLSTM_HEADS * dv
    cos, sin_signed = _rope_tables(seq)

    gate_lo, gate_hi = fox_f_off, fox_f_off + fox_heads
    attn_w_main = jnp.concatenate([attn_w_in[:, :, :gate_lo], attn_w_in[:, :, gate_hi:]], axis=2).astype(BF16)
    attn_b_main = jnp.concatenate([attn_b_in[:, :gate_lo], attn_b_in[:, gate_hi:]], axis=1)
    attn_w_out_b = attn_w_out.astype(BF16)
    mlstm_w_main = mlstm_w_in[:, :, :mlstm_main].astype(BF16)
    mlstm_w_out_b = mlstm_w_out.astype(BF16)
    ffn_w_up_b = _ffn_up_blocks(ffn_w_up, _tile(ffn_w_down.shape[1], FFN_BF, LANES))
    ffn_w_down_b = ffn_w_down.astype(BF16)

    h = x.reshape(batch * seq, d)
    for layer in range(depth):
        j = layer // 2
        if layer % 2 == 0:
            gates = _gate_params(attn_w_in[j][:, gate_lo:gate_hi], attn_b_in[j][gate_lo:gate_hi])
            proj, g, gt = _in_projection(h, attn_w_main, j, attn_b_main[j][None, :], *gates)
            fox = _fox_attention(proj, _fox_gate_cumsum(g, batch, seq), batch, seq, fox_heads)
            swa_q = 3 * fox_dim
            swa_k = swa_q + swa_heads * HEAD_DIM
            swa_v = swa_k + SWA_KV_HEADS * HEAD_DIM
            swa = _swa_attention(proj, attn_sinks[j], cos, sin_signed, batch, seq, swa_q, swa_k, swa_v, swa_heads)
            h = _out_projection_ln([fox, swa], attn_w_out_b, j, h, ln1_g[layer][None, :], ln1_b[layer][None, :], alpha)
        else:
            gates = _gate_params(mlstm_w_in[j][:, mlstm_main:], mlstm_b_in[j][mlstm_main:])
            proj, g, gt = _in_projection(h, mlstm_w_main, j, mlstm_b_in[j][None, :mlstm_main], *gates)
            mixed = _mlstm(proj, g, gt, batch, seq, dk, dv)
            h = _out_projection_ln([mixed], mlstm_w_out_b, j, h, ln1_g[layer][None, :], ln1_b[layer][None, :], alpha)
        h = _conv_ffn_ln(h, ffn_w_up_b, ffn_w_down_b, layer, ffn_conv_w[layer], ffn_conv_b[layer][None, :],
                         ln2_g[layer][None, :], ln2_b[layer][None, :], alpha, seq)
    return h.reshape(batch, seq, d)
```

```python
import functools

import jax
import jax.numpy as jnp
from jax import lax
from jax.experimental import pallas as pl
from jax.experimental.pallas import tpu as pltpu

F32 = jnp.float32
BF16 = jnp.bfloat16

HEAD_DIM = 128
SWA_KV_HEADS = 2
SWA_WINDOW = 128
ROPE_THETA = 10000.0
MLSTM_HEADS = 8
CONV_WIDTH = 3
LN_EPS = 1e-5
LOG2_E = 1.4426950408889634

LANES = 128
SUBLANES = 8
GATE_ROWS = 16
HEAD_ROWS = 16
VMEM_LIMIT_BYTES = 60 * 1024 * 1024

PROJ_BM = 1024
PROJ_BN = 1536
OUT_BM = 512
OUT_ROWS = 128
FFN_BM = 1024
FFN_BF = 512
FFN_CHUNK = 256
FFN_ROWS = 256
FFN_SLOTS = 2
FOX_T = 512
FOX_GROUP = 2
SWA_T = 512
CUM_T = 512
MLSTM_L = 256

NT_DIMS = (((1,), (1,)), ((), ()))
TN_DIMS = (((0,), (0,)), ((), ()))


def _tile(n, pref, unit):
    t = min(pref, n)
    while n % t or t % unit:
        t -= unit
    assert t > 0, (n, pref, unit)
    return t


def _params(sem):
    return pltpu.CompilerParams(dimension_semantics=sem, vmem_limit_bytes=VMEM_LIMIT_BYTES)


def _log_sigmoid(x):
    return jnp.minimum(x, 0.0) - jnp.log(1.0 + jnp.exp(-jnp.abs(x)))


def _sigmoid(x):
    return 1.0 / (1.0 + jnp.exp(-x))


def _split3(x):
    h1 = x.astype(BF16)
    r1 = x - h1.astype(F32)
    h2 = r1.astype(BF16)
    h3 = (r1 - h2.astype(F32)).astype(BF16)
    return h1, h2, h3


def _cumsum_rows(tri, x):
    return sum(jnp.dot(tri, h, preferred_element_type=F32) for h in _split3(x))


def _cumsum_lanes(x, tri_t):
    return sum(jnp.dot(h, tri_t, preferred_element_type=F32) for h in _split3(x))


def _layer_norm(z, g, b):
    mu = jnp.mean(z, axis=-1, keepdims=True)
    zc = z - mu
    var = jnp.mean(zc * zc, axis=-1, keepdims=True)
    return zc * lax.rsqrt(var + LN_EPS) * g + b


def _proj_kernel(x_ref, w_ref, b_ref, wg_ref, bg_ref, o_ref, g_ref, gt_ref, xb_ref):
    @pl.when(pl.program_id(1) == 0)
    def _():
        xb = x_ref[...].astype(BF16)
        xb_ref[...] = xb
        g = jnp.dot(xb, wg_ref[...], preferred_element_type=F32) + bg_ref[...]
        g_ref[...] = g
        gt_ref[...] = g.T[:GATE_ROWS, :]

    acc = jnp.dot(xb_ref[...], w_ref[...], preferred_element_type=F32)
    o_ref[...] = (acc + b_ref[...]).astype(o_ref.dtype)


def _in_projection(x, w, layer, b, wg, bg):
    m, d = x.shape
    n = w.shape[2]
    bm = _tile(m, PROJ_BM, LANES)
    bn = _tile(n, PROJ_BN, LANES)
    return pl.pallas_call(
        _proj_kernel,
        grid=(m // bm, n // bn),
        in_specs=[
            pl.BlockSpec((bm, d), lambda i, j: (i, 0)),
            pl.BlockSpec((None, d, bn), lambda i, j: (layer, 0, j)),
            pl.BlockSpec((1, bn), lambda i, j: (0, j)),
            pl.BlockSpec((d, LANES), lambda i, j: (0, 0)),
            pl.BlockSpec((1, LANES), lambda i, j: (0, 0)),
        ],
        out_specs=[
            pl.BlockSpec((bm, bn), lambda i, j: (i, j)),
            pl.BlockSpec((bm, LANES), lambda i, j: (i, 0)),
            pl.BlockSpec((GATE_ROWS, bm), lambda i, j: (0, i)),
        ],
        out_shape=[
            jax.ShapeDtypeStruct((m, n), BF16),
            jax.ShapeDtypeStruct((m, LANES), F32),
            jax.ShapeDtypeStruct((GATE_ROWS, m), F32),
        ],
        scratch_shapes=[pltpu.VMEM((bm, d), BF16)],
        compiler_params=_params(("arbitrary", "arbitrary")),
        name="in_projection",
    )(x, w, b, wg, bg)


def _fox_gate_kernel(g_ref, c_ref, carry_ref):
    @pl.when(pl.program_id(1) == 0)
    def _():
        carry_ref[...] = jnp.zeros_like(carry_ref)

    t = g_ref.shape[0]
    row = lax.broadcasted_iota(jnp.int32, (t, t), 0)
    col = lax.broadcasted_iota(jnp.int32, (t, t), 1)
    c = _cumsum_rows((col <= row).astype(BF16), _log_sigmoid(g_ref[...])) + carry_ref[...]
    c_ref[...] = c
    carry_ref[...] = c[t - 1:t, :]


def _fox_gate_cumsum(g, batch, seq):
    t = _tile(seq, CUM_T, LANES)
    ns = seq // t
    return pl.pallas_call(
        _fox_gate_kernel,
        grid=(batch, ns),
        in_specs=[pl.BlockSpec((t, LANES), lambda b, s: (b * ns + s, 0))],
        out_specs=pl.BlockSpec((t, LANES), lambda b, s: (b * ns + s, 0)),
        out_shape=jax.ShapeDtypeStruct((batch * seq, LANES), F32),
        scratch_shapes=[pltpu.VMEM((1, LANES), F32)],
        compiler_params=_params(("arbitrary", "arbitrary")),
        name="fox_gate_cumsum",
    )(g)


def _bias_lanes(c, ones_first):
    n = c.shape[0]
    h1, h2, h3 = (h.astype(F32) for h in _split3(c))
    lane = lax.broadcasted_iota(jnp.int32, (n, HEAD_DIM), 1)
    lo = 3 if ones_first else 0
    split = jnp.where(lane == lo, h1, jnp.where(lane == lo + 1, h2, jnp.where(lane == lo + 2, h3, 0.0)))
    ones = jnp.logical_and(lane >= 3 - lo, lane < 6 - lo)
    return jnp.where(ones, 1.0, split).astype(BF16)


def _fox_kernel(q_ref, k_ref, v_ref, ct_ref, cs_ref, o_ref, kb_ref, m_ref, acc_ref, s_ref, *, scale, group):
    first_head = pl.program_id(1) * group
    qi = pl.program_id(2)
    t = q_ref.shape[0]

    def head_lane(c, g):
        lane = lax.broadcasted_iota(jnp.int32, c.shape, 1)
        return jnp.sum(jnp.where(lane == first_head + g, c, 0.0), axis=-1, keepdims=True) * LOG2_E

    @pl.when(qi == 0)
    def _():
        cs = cs_ref[...]
        for g in range(group):
            kb_ref[g] = _bias_lanes(-head_lane(cs, g), ones_first=False)

    ct = ct_ref[...]
    unit = (lax.broadcasted_iota(jnp.int32, (t, HEAD_DIM), 1) == 0).astype(BF16)
    heads = []
    for g in range(group):
        sl = slice(g * HEAD_DIM, (g + 1) * HEAD_DIM)
        q = (q_ref[:, sl].astype(F32) * (scale * LOG2_E)).astype(BF16)
        heads.append((sl, jnp.concatenate([q, _bias_lanes(head_lane(ct, g), ones_first=True)], axis=1)))

    def logits(slot, ki):
        start = pl.multiple_of(ki * t, t)
        for g, (sl, q_ext) in enumerate(heads):
            k_ext = jnp.concatenate([k_ref[pl.ds(start, t), sl], kb_ref[g, pl.ds(start, t), :]], axis=1)
            s_ref[slot, g] = lax.dot_general(q_ext, k_ext, NT_DIMS, preferred_element_type=F32)

    def absorb(slot, ki, diagonal):
        start = pl.multiple_of(ki * t, t)
        for g, (sl, _) in enumerate(heads):
            s = s_ref[slot, g]
            if diagonal:
                row = lax.broadcasted_iota(jnp.int32, (t, t), 0)
                col = lax.broadcasted_iota(jnp.int32, (t, t), 1)
                s = jnp.where(col <= row, s, -jnp.inf)
            m = m_ref[g]
            m_new = jnp.maximum(m, jnp.max(s, axis=-1, keepdims=True))
            p = jnp.exp2(s - m_new).astype(BF16)
            v_ext = jnp.concatenate([v_ref[pl.ds(start, t), sl], unit], axis=1)
            acc_ref[g] = jnp.exp2(m - m_new) * acc_ref[g] + jnp.dot(p, v_ext, preferred_element_type=F32)
            m_ref[g] = m_new

    m_ref[...] = jnp.full(m_ref.shape, -jnp.inf, F32)
    acc_ref[...] = jnp.zeros_like(acc_ref)

    logits(0, 0)

    @pl.loop(0, lax.shift_right_logical(qi, 1))
    def _(pair):
        ki = 2 * pair
        logits(1, ki + 1)
        absorb(0, ki, False)
        logits(0, ki + 2)
        absorb(1, ki + 1, False)

    odd = lax.rem(qi, 2) == 1

    @pl.when(odd)
    def _():
        logits(1, qi)
        absorb(0, qi - 1, False)
        absorb(1, qi, True)

    @pl.when(jnp.logical_not(odd))
    def _():
        absorb(0, qi, True)

    for g, (sl, _) in enumerate(heads):
        acc = acc_ref[g]
        o_ref[:, sl] = (acc[:, :HEAD_DIM] / acc[:, HEAD_DIM:HEAD_DIM + 1]).astype(o_ref.dtype)


def _fox_attention(proj, c, batch, seq, heads):
    t = _tile(seq, FOX_T, LANES)
    nq = seq // t
    group = FOX_GROUP
    assert heads % group == 0
    hb = heads // group
    gw = group * HEAD_DIM
    return pl.pallas_call(
        functools.partial(_fox_kernel, scale=HEAD_DIM ** -0.5, group=group),
        grid=(batch, hb, nq),
        in_specs=[
            pl.BlockSpec((t, gw), lambda b, h, i: (b * nq + i, h)),
            pl.BlockSpec((seq, gw), lambda b, h, i: (b, hb + h)),
            pl.BlockSpec((seq, gw), lambda b, h, i: (b, 2 * hb + h)),
            pl.BlockSpec((t, LANES), lambda b, h, i: (b * nq + i, 0)),
            pl.BlockSpec((seq, LANES), lambda b, h, i: (b, 0)),
        ],
        out_specs=pl.BlockSpec((t, gw), lambda b, h, i: (b * nq + i, h)),
        out_shape=jax.ShapeDtypeStruct((batch * seq, heads * HEAD_DIM), BF16),
        scratch_shapes=[pltpu.VMEM((group, seq, HEAD_DIM), BF16),
                        pltpu.VMEM((group, t, 1), F32),
                        pltpu.VMEM((group, t, 2 * HEAD_DIM), F32),
                        pltpu.VMEM((2, group, t, t), F32)],
        compiler_params=_params(("arbitrary", "arbitrary", "arbitrary")),
        name="fox_attention",
    )(proj, proj, proj, c, c)


def _rope(x, cos, sin_signed):
    xf = x.astype(F32)
    return xf * cos + pltpu.roll(xf, HEAD_DIM // 2, 1) * sin_signed


def _swa_kernel(sink_ref, q_ref, k_ref, v_ref, kp_ref, vp_ref, cos_ref, sin_ref, cosp_ref, sinp_ref, o_ref,
                *, scale, group):
    w = SWA_WINDOW
    first_visible = jnp.where(pl.program_id(1) == 0, w, 0)
    nblk = q_ref.shape[0] // w
    cos, sin = cos_ref[...], sin_ref[...]
    cosp, sinp = cosp_ref[...], sinp_ref[...]
    row = lax.broadcasted_iota(jnp.int32, (w, 2 * w), 0)
    col = lax.broadcasted_iota(jnp.int32, (w, 2 * w), 1)
    valid = jnp.logical_and(col > row, col - w <= row)
    valid_first = jnp.logical_and(valid, col >= first_visible)
    for kv in range(SWA_KV_HEADS):
        ksl = slice(kv * HEAD_DIM, (kv + 1) * HEAD_DIM)
        k_all = jnp.concatenate([_rope(kp_ref[:, ksl], cosp, sinp), _rope(k_ref[:, ksl], cos, sin)],
                                axis=0).astype(BF16)
        v_all = jnp.concatenate([vp_ref[:, ksl], v_ref[:, ksl]], axis=0)
        for g in range(group):
            hq = kv * group + g
            qsl = slice(hq * HEAD_DIM, (hq + 1) * HEAD_DIM)
            q = _rope(q_ref[:, qsl], cos, sin).astype(BF16)
            sink = sink_ref[hq]
            for blk in range(nblk):
                qb = q[blk * w:(blk + 1) * w, :]
                kb = k_all[blk * w:(blk + 2) * w, :]
                vb = v_all[blk * w:(blk + 2) * w, :]
                s = lax.dot_general(qb, kb, NT_DIMS, preferred_element_type=F32) * scale
                s = jnp.where(valid_first if blk == 0 else valid, s, -jnp.inf)
                m = jnp.maximum(jnp.max(s, axis=-1, keepdims=True), sink)
                p = jnp.exp(s - m)
                denom = jnp.sum(p, axis=-1, keepdims=True) + jnp.exp(sink - m)
                o = jnp.dot((p / denom).astype(BF16), vb, preferred_element_type=F32)
                o_ref[blk * w:(blk + 1) * w, qsl] = o.astype(o_ref.dtype)


def _swa_attention(proj, sinks, cos, sin_signed, batch, seq, q_col, k_col, v_col, q_heads):
    t = _tile(seq, SWA_T, SWA_WINDOW)
    nt = seq // t
    per = t // SWA_WINDOW
    qw = q_heads * HEAD_DIM
    kvw = SWA_KV_HEADS * HEAD_DIM
    assert q_col % qw == 0 and k_col % kvw == 0 and v_col % kvw == 0

    def prev(b, i):
        return jnp.maximum((b * nt + i) * per - 1, 0)

    def prev_pos(i):
        return jnp.maximum(i * per - 1, 0)

    return pl.pallas_call(
        functools.partial(_swa_kernel, scale=HEAD_DIM ** -0.5, group=q_heads // SWA_KV_HEADS),
        grid=(batch, nt),
        in_specs=[
            pl.BlockSpec(memory_space=pltpu.SMEM),
            pl.BlockSpec((t, qw), lambda b, i: (b * nt + i, q_col // qw)),
            pl.BlockSpec((t, kvw), lambda b, i: (b * nt + i, k_col // kvw)),
            pl.BlockSpec((t, kvw), lambda b, i: (b * nt + i, v_col // kvw)),
            pl.BlockSpec((SWA_WINDOW, kvw), lambda b, i: (prev(b, i), k_col // kvw)),
            pl.BlockSpec((SWA_WINDOW, kvw), lambda b, i: (prev(b, i), v_col // kvw)),
            pl.BlockSpec((t, HEAD_DIM), lambda b, i: (i, 0)),
            pl.BlockSpec((t, HEAD_DIM), lambda b, i: (i, 0)),
            pl.BlockSpec((SWA_WINDOW, HEAD_DIM), lambda b, i: (prev_pos(i), 0)),
            pl.BlockSpec((SWA_WINDOW, HEAD_DIM), lambda b, i: (prev_pos(i), 0)),
        ],
        out_specs=pl.BlockSpec((t, qw), lambda b, i: (b * nt + i, 0)),
        out_shape=jax.ShapeDtypeStruct((batch * seq, qw), BF16),
        compiler_params=_params(("arbitrary", "arbitrary")),
        name="swa_attention",
    )(sinks, proj, proj, proj, proj, proj, cos, sin_signed, cos, sin_signed)


def _mlstm_kernel(q_ref, k_ref, v_ref, og_ref, g_ref, gt_ref, o_ref, state_ref, m_ref, *, dk, dv):
    heads = MLSTM_HEADS

    @pl.when(pl.program_id(1) == 0)
    def _():
        state_ref[...] = jnp.zeros_like(state_ref)
        m_ref[...] = jnp.zeros_like(m_ref)

    n = q_ref.shape[0]
    g = g_ref[...]
    gt = gt_ref[...]
    row = lax.broadcasted_iota(jnp.int32, (n, n), 0)
    col = lax.broadcasted_iota(jnp.int32, (n, n), 1)
    causal = col <= row
    b_col_all = _cumsum_rows(causal.astype(BF16), _log_sigmoid(g))
    b_row_all = _cumsum_lanes(_log_sigmoid(gt), (row <= col).astype(BF16))
    unit = (lax.broadcasted_iota(jnp.int32, (n, LANES), 1) == 0).astype(BF16)

    for h in range(heads):
        i_col = g[:, h:h + 1]
        b_col = b_col_all[:, heads + h:heads + h + 1]
        i_row = gt[h:h + 1, :]
        b_row = b_row_all[heads + h:heads + h + 1, :]
        b_last = b_row[:, n - 1:n]
        m_prev = m_ref[h]

        d = jnp.where(causal, (b_col - b_row) + i_row, -jnp.inf)
        m_inter = b_col + m_prev
        m_t = jnp.maximum(m_inter, jnp.max(d, axis=-1, keepdims=True))
        inter = jnp.exp(m_inter - m_t)
        wgt = jnp.exp(d - m_t)

        qh = q_ref[:, h * dk:(h + 1) * dk]
        kh = (k_ref[:, h * dk:(h + 1) * dk].astype(F32) * (dk ** -0.5)).astype(BF16)
        v_ext = jnp.concatenate([v_ref[:, h * dv:(h + 1) * dv], unit], axis=1)
        state = state_ref[h]

        sm = wgt * lax.dot_general(qh, kh, NT_DIMS, preferred_element_type=F32)
        tot = inter * jnp.dot(qh, state.astype(BF16), preferred_element_type=F32)
        tot = tot + jnp.dot(sm.astype(BF16), v_ext, preferred_element_type=F32)
        den = jnp.maximum(jnp.abs(tot[:, dv:dv + 1]), jnp.exp(-m_t))
        hid = tot[:, :dv] / den
        gate = _sigmoid(og_ref[:, h * dv:(h + 1) * dv].astype(F32))
        o_ref[:, h * dv:(h + 1) * dv] = (gate * hid).astype(o_ref.dtype)

        m_new = jnp.maximum(b_last + m_prev, jnp.max((b_last - b_row) + i_row, axis=-1, keepdims=True))
        decay = jnp.exp(b_last + m_prev - m_new)
        w_end = jnp.exp(((b_last - b_col) + i_col) - m_new)
        vw = (v_ext.astype(F32) * w_end).astype(BF16)
        state_ref[h] = decay * state + lax.dot_general(kh, vw, TN_DIMS, preferred_element_type=F32)
        m_ref[h] = m_new


def _mlstm(proj, g, gt, batch, seq, dk, dv):
    heads = MLSTM_HEADS
    n = _tile(seq, MLSTM_L, LANES)
    nc = seq // n
    qk, vw = heads * dk, heads * dv
    assert vw % qk == 0
    r = vw // qk
    return pl.pallas_call(
        functools.partial(_mlstm_kernel, dk=dk, dv=dv),
        grid=(batch, nc),
        in_specs=[
            pl.BlockSpec((n, qk), lambda b, c: (b * nc + c, 0)),
            pl.BlockSpec((n, qk), lambda b, c: (b * nc + c, 1)),
            pl.BlockSpec((n, vw), lambda b, c: (b * nc + c, 2 // r)),
            pl.BlockSpec((n, vw), lambda b, c: (b * nc + c, 2 // r + 1)),
            pl.BlockSpec((n, LANES), lambda b, c: (b * nc + c, 0)),
            pl.BlockSpec((GATE_ROWS, n), lambda b, c: (0, b * nc + c)),
        ],
        out_specs=pl.BlockSpec((n, vw), lambda b, c: (b * nc + c, 0)),
        out_shape=jax.ShapeDtypeStruct((batch * seq, vw), BF16),
        scratch_shapes=[pltpu.VMEM((heads, dk, dv + LANES), F32), pltpu.VMEM((heads, 1, 1), F32)],
        compiler_params=_params(("arbitrary", "arbitrary")),
        name="mlstm",
    )(proj, proj, proj, proj, g, gt)


def _outproj_ln_kernel(*refs, alpha, n_in):
    h_refs, w_refs = refs[:n_in], refs[n_in:2 * n_in]
    x_ref, g_ref, b_ref, o_ref = refs[2 * n_in:]
    for r in range(o_ref.shape[0] // OUT_ROWS):
        rows = slice(r * OUT_ROWS, (r + 1) * OUT_ROWS)
        o_ref[rows, :] = sum(jnp.dot(h[rows, :], w[...], preferred_element_type=F32) for h, w in zip(h_refs, w_refs))
        o_ref[rows, :] = _layer_norm(alpha * x_ref[rows, :] + o_ref[rows, :], g_ref[...], b_ref[...])


def _out_projection_ln(hs, w, layer, x, ln_g, ln_b, alpha):
    m, d = x.shape
    bm = _tile(m, OUT_BM, LANES)
    widths = [h.shape[1] for h in hs]
    assert all(wd == widths[0] for wd in widths) and w.shape[1] == sum(widths)
    in_specs = [pl.BlockSpec((bm, wd), lambda i: (i, 0)) for wd in widths]
    in_specs += [pl.BlockSpec((None, wd, d), lambda i, k=k: (layer, k, 0)) for k, wd in enumerate(widths)]
    in_specs += [pl.BlockSpec((bm, d), lambda i: (i, 0)),
                 pl.BlockSpec((1, d), lambda i: (0, 0)),
                 pl.BlockSpec((1, d), lambda i: (0, 0))]
    return pl.pallas_call(
        functools.partial(_outproj_ln_kernel, alpha=alpha, n_in=len(hs)),
        grid=(m // bm,),
        in_specs=in_specs,
        out_specs=pl.BlockSpec((bm, d), lambda i: (i, 0)),
        out_shape=jax.ShapeDtypeStruct((m, d), F32),
        compiler_params=_params(("arbitrary",)),
        name="out_projection_ln",
    )(*hs, *([w] * len(hs)), x, ln_g, ln_b)


def _causal_conv(u, prev, w_ref, b_ref, cs):
    w0, w1, w2, b = w_ref[0:1, cs], w_ref[1:2, cs], w_ref[2:3, cs], b_ref[:, cs]
    body = b + w0 * pltpu.roll(u, 2, 0) + w1 * pltpu.roll(u, 1, 0) + w2 * u
    top = jnp.concatenate([prev, u[:HEAD_ROWS, :]], axis=0)
    head = b + w0 * pltpu.roll(top, 2, 0) + w1 * pltpu.roll(top, 1, 0) + w2 * top
    return body, head[SUBLANES:, :]


def _ffn_kernel(x_ref, wg_ref, wv_ref, cwg_ref, cwv_ref, cbg_ref, cbv_ref, wd_ref, lg_ref, lb_ref, o_ref,
                xb_ref, h0_ref, h1_ref, pg_ref, pv_ref, ug_ref, uv_ref, *, alpha, tiles_per_seq, nf):
    i = pl.program_id(0)
    j = pl.program_id(1)
    bm = x_ref.shape[0]
    h_refs = (h0_ref, h1_ref)

    def up(h_ref):
        xb = xb_ref[...]
        kept_rows = jnp.where(i % tiles_per_seq != 0, SUBLANES, 0)
        keep = lax.broadcasted_iota(jnp.int32, (SUBLANES, FFN_CHUNK), 0) < kept_rows
        chunks = [slice(c * FFN_CHUNK, (c + 1) * FFN_CHUNK) for c in range(h_ref.shape[1] // FFN_CHUNK)]
        prev_g = [jnp.where(keep, pg_ref[j, :, cs], 0.0) for cs in chunks]
        prev_v = [jnp.where(keep, pv_ref[j, :, cs], 0.0) for cs in chunks]
        for r in range(bm // FFN_ROWS):
            xr = xb[r * FFN_ROWS:(r + 1) * FFN_ROWS, :]
            for c, cs in enumerate(chunks):
                slot = (r * len(chunks) + c) % FFN_SLOTS
                ug_ref[slot] = jnp.dot(xr, wg_ref[:, cs], preferred_element_type=F32)
                uv_ref[slot] = jnp.dot(xr, wv_ref[:, cs], preferred_element_type=F32)
                ug, uv = ug_ref[slot], uv_ref[slot]
                cg, cg_top = _causal_conv(ug, prev_g[c], cwg_ref, cbg_ref, cs)
                cv, cv_top = _causal_conv(uv, prev_v[c], cwv_ref, cbv_ref, cs)
                prev_g[c] = ug[FFN_ROWS - SUBLANES:, :]
                prev_v[c] = uv[FFN_ROWS - SUBLANES:, :]
                h_ref[r * FFN_ROWS:(r + 1) * FFN_ROWS, cs] = (cg * _sigmoid(cg) * cv).astype(BF16)
                h_ref[r * FFN_ROWS:r * FFN_ROWS + HEAD_ROWS, cs] = (cg_top * _sigmoid(cg_top) * cv_top).astype(BF16)
        for c, cs in enumerate(chunks):
            pg_ref[j, :, cs] = prev_g[c]
            pv_ref[j, :, cs] = prev_v[c]

    def down(h_ref):
        o_ref[...] += jnp.dot(h_ref[...], wd_ref[...], preferred_element_type=F32)

    @pl.when(j == 0)
    def _():
        xb_ref[...] = x_ref[...].astype(BF16)
        o_ref[...] = jnp.zeros_like(o_ref)
        up(h_refs[0])

    for parity in range(2):
        @pl.when(jnp.logical_and(jnp.logical_and(j > 0, j < nf), j % 2 == parity))
        def _():
            up(h_refs[parity])
            down(h_refs[1 - parity])

    @pl.when(j == nf)
    def _():
        down(h_refs[(nf - 1) % 2])
        o_ref[...] = _layer_norm(alpha * x_ref[...] + o_ref[...], lg_ref[...], lb_ref[...])


def _conv_ffn_ln(x, w_up, w_down, layer, conv_w, conv_b, ln_g, ln_b, alpha, seq):
    m, d = x.shape
    f = w_down.shape[1]
    bm = _tile(seq, FFN_BM, 2 * SUBLANES)
    bf = _tile(f, FFN_BF, LANES)
    nf = f // bf

    def up_blk(j):
        return jnp.minimum(j, nf - 1)

    def down_blk(j):
        return jnp.maximum(j - 1, 0)

    return pl.pallas_call(
        functools.partial(_ffn_kernel, alpha=alpha, tiles_per_seq=seq // bm, nf=nf),
        grid=(m // bm, nf + 1),
        in_specs=[
            pl.BlockSpec((bm, d), lambda i, j: (i, 0)),
            pl.BlockSpec((None, d, bf), lambda i, j: (layer, 0, up_blk(j))),
            pl.BlockSpec((None, d, bf), lambda i, j: (layer, 0, nf + up_blk(j))),
            pl.BlockSpec((CONV_WIDTH, bf), lambda i, j: (0, up_blk(j))),
            pl.BlockSpec((CONV_WIDTH, bf), lambda i, j: (0, nf + up_blk(j))),
            pl.BlockSpec((1, bf), lambda i, j: (0, up_blk(j))),
            pl.BlockSpec((1, bf), lambda i, j: (0, nf + up_blk(j))),
            pl.BlockSpec((None, bf, d), lambda i, j: (layer, down_blk(j), 0)),
            pl.BlockSpec((1, d), lambda i, j: (0, 0)),
            pl.BlockSpec((1, d), lambda i, j: (0, 0)),
        ],
        out_specs=pl.BlockSpec((bm, d), lambda i, j: (i, 0)),
        out_shape=jax.ShapeDtypeStruct((m, d), F32),
        scratch_shapes=[
            pltpu.VMEM((bm, d), BF16),
            pltpu.VMEM((bm, bf), BF16),
            pltpu.VMEM((bm, bf), BF16),
            pltpu.VMEM((nf, SUBLANES, bf), F32),
            pltpu.VMEM((nf, SUBLANES, bf), F32),
            pltpu.VMEM((FFN_SLOTS, FFN_ROWS, FFN_CHUNK), F32),
            pltpu.VMEM((FFN_SLOTS, FFN_ROWS, FFN_CHUNK), F32),
        ],
        compiler_params=_params(("arbitrary", "arbitrary")),
        name="conv_ffn_ln",
    )(x, w_up, w_up, conv_w, conv_w, conv_b, conv_b, w_down, ln_g, ln_b)


def _gate_params(w, b):
    d, n = w.shape
    wg = jnp.zeros((d, LANES), BF16).at[:, :n].set(w.astype(BF16))
    bg = jnp.zeros((1, LANES), F32).at[0, :n].set(b)
    return wg, bg


def _rope_tables(seq):
    half = HEAD_DIM // 2
    inv_freq = jnp.power(ROPE_THETA, -jnp.arange(half, dtype=F32) * (2.0 / HEAD_DIM))
    ang = jnp.arange(seq, dtype=F32)[:, None] * inv_freq[None, :]
    cos, sin = jnp.cos(ang), jnp.sin(ang)
    return jnp.concatenate([cos, cos], axis=-1), jnp.concatenate([-sin, sin], axis=-1)


def kernel(x, attn_w_in, attn_b_in, attn_sinks, attn_w_out, mlstm_w_in, mlstm_b_in, mlstm_w_out, ffn_w_up,
           ffn_conv_w, ffn_conv_b, ffn_w_down, ln1_g, ln1_b, ln2_g, ln2_b):
    batch, seq, d = x.shape
    depth = ln1_g.shape[0]
    alpha = float((2 * depth) ** 0.25)
    fox_heads = d // (2 * HEAD_DIM)
    swa_heads = d // (2 * HEAD_DIM)
    fox_dim = fox_heads * HEAD_DIM
    fox_f_off = 3 * fox_dim
    dk, dv = d // (2 * MLSTM_HEADS), d // MLSTM_HEADS
    mlstm_main = 2 * MLSTM_HEADS * dk + 2 * MLSTM_HEADS * dv
    cos, sin_signed = _rope_tables(seq)

    gate_lo, gate_hi = fox_f_off, fox_f_off + fox_heads
    attn_w_gate = lax.optimization_barrier(attn_w_in[:, :, gate_lo:gate_hi])
    mlstm_w_gate = lax.optimization_barrier(mlstm_w_in[:, :, mlstm_main:])
    attn_w_main = jnp.concatenate([attn_w_in[:, :, :gate_lo], attn_w_in[:, :, gate_hi:]], axis=2).astype(BF16)
    attn_b_main = jnp.concatenate([attn_b_in[:, :gate_lo], attn_b_in[:, gate_hi:]], axis=1)
    attn_w_out_b = attn_w_out.astype(BF16)
    mlstm_w_main = mlstm_w_in[:, :, :mlstm_main].astype(BF16)
    mlstm_w_out_b = mlstm_w_out.astype(BF16)
    ffn_w_up_b = ffn_w_up.astype(BF16)
    ffn_w_down_b = ffn_w_down.astype(BF16)

    h = x.reshape(batch * seq, d)
    for layer in range(depth):
        j = layer // 2
        if layer % 2 == 0:
            gates = _gate_params(attn_w_gate[j], attn_b_in[j][gate_lo:gate_hi])
            proj, g, gt = _in_projection(h, attn_w_main, j, attn_b_main[j][None, :], *gates)
            fox = _fox_attention(proj, _fox_gate_cumsum(g, batch, seq), batch, seq, fox_heads)
            swa_q = 3 * fox_dim
            swa_k = swa_q + swa_heads * HEAD_DIM
            swa_v = swa_k + SWA_KV_HEADS * HEAD_DIM
            swa = _swa_attention(proj, attn_sinks[j], cos, sin_signed, batch, seq, swa_q, swa_k, swa_v, swa_heads)
            h = _out_projection_ln([fox, swa], attn_w_out_b, j, h, ln1_g[layer][None, :], ln1_b[layer][None, :], alpha)
        else:
            gates = _gate_params(mlstm_w_gate[j], mlstm_b_in[j][mlstm_main:])
            proj, g, gt = _in_projection(h, mlstm_w_main, j, mlstm_b_in[j][None, :mlstm_main], *gates)
            mixed = _mlstm(proj, g, gt, batch, seq, dk, dv)
            h = _out_projection_ln([mixed], mlstm_w_out_b, j, h, ln1_g[layer][None, :], ln1_b[layer][None, :], alpha)
        h = _conv_ffn_ln(h, ffn_w_up_b, ffn_w_down_b, layer, ffn_conv_w[layer], ffn_conv_b[layer][None, :],
                         ln2_g[layer][None, :], ln2_b[layer][None, :], alpha, seq)
    return h.reshape(batch, seq, d)
```

```python
import functools

import jax
import jax.numpy as jnp
from jax import lax
from jax.experimental import pallas as pl
from jax.experimental.pallas import tpu as pltpu

F32 = jnp.float32
BF16 = jnp.bfloat16

HEAD_DIM = 128
SWA_KV_HEADS = 2
SWA_WINDOW = 128
ROPE_THETA = 10000.0
MLSTM_HEADS = 8
CONV_WIDTH = 3
LN_EPS = 1e-5
LOG2_E = 1.4426950408889634

LANES = 128
SUBLANES = 8
GATE_ROWS = 16
HEAD_ROWS = 16
VMEM_LIMIT_BYTES = 60 * 1024 * 1024

PROJ_BM = 1024
PROJ_BN = 1536
OUT_BM = 512
OUT_ROWS = 128
FFN_BM = 1024
FFN_BF = 512
FFN_CHUNK = 256
FFN_ROWS = 256
FFN_SLOTS = 2
FOX_T = 512
FOX_GROUP = 2
SWA_T = 512
CUM_T = 512
MLSTM_L = 256

NT_DIMS = (((1,), (1,)), ((), ()))
TN_DIMS = (((0,), (0,)), ((), ()))


def _tile(n, pref, unit):
    t = min(pref, n)
    while n % t or t % unit:
        t -= unit
    assert t > 0, (n, pref, unit)
    return t


def _params(sem):
    return pltpu.CompilerParams(dimension_semantics=sem, vmem_limit_bytes=VMEM_LIMIT_BYTES)


def _log_sigmoid(x):
    return jnp.minimum(x, 0.0) - jnp.log(1.0 + jnp.exp(-jnp.abs(x)))


def _sigmoid(x):
    return 1.0 / (1.0 + jnp.exp(-x))


def _split3(x):
    h1 = x.astype(BF16)
    r1 = x - h1.astype(F32)
    h2 = r1.astype(BF16)
    h3 = (r1 - h2.astype(F32)).astype(BF16)
    return h1, h2, h3


def _cumsum_rows(tri, x):
    return sum(jnp.dot(tri, h, preferred_element_type=F32) for h in _split3(x))


def _cumsum_lanes(x, tri_t):
    return sum(jnp.dot(h, tri_t, preferred_element_type=F32) for h in _split3(x))


def _layer_norm(z, g, b):
    mu = jnp.mean(z, axis=-1, keepdims=True)
    zc = z - mu
    var = jnp.mean(zc * zc, axis=-1, keepdims=True)
    return zc * lax.rsqrt(var + LN_EPS) * g + b


def _proj_kernel(x_ref, w_ref, b_ref, wg_ref, bg_ref, o_ref, g_ref, gt_ref, xb_ref):
    @pl.when(pl.program_id(1) == 0)
    def _():
        xb = x_ref[...].astype(BF16)
        xb_ref[...] = xb
        g = jnp.dot(xb, wg_ref[...], preferred_element_type=F32) + bg_ref[...]
        g_ref[...] = g
        gt_ref[...] = g.T[:GATE_ROWS, :]

    acc = jnp.dot(xb_ref[...], w_ref[...], preferred_element_type=F32)
    o_ref[...] = (acc + b_ref[...]).astype(o_ref.dtype)


def _in_projection(x, w, layer, b, wg, bg):
    m, d = x.shape
    n = w.shape[2]
    bm = _tile(m, PROJ_BM, LANES)
    bn = _tile(n, PROJ_BN, LANES)
    return pl.pallas_call(
        _proj_kernel,
        grid=(m // bm, n // bn),
        in_specs=[
            pl.BlockSpec((bm, d), lambda i, j: (i, 0)),
            pl.BlockSpec((None, d, bn), lambda i, j: (layer, 0, j)),
            pl.BlockSpec((1, bn), lambda i, j: (0, j)),
            pl.BlockSpec((d, LANES), lambda i, j: (0, 0)),
            pl.BlockSpec((1, LANES), lambda i, j: (0, 0)),
        ],
        out_specs=[
            pl.BlockSpec((bm, bn), lambda i, j: (i, j)),
            pl.BlockSpec((bm, LANES), lambda i, j: (i, 0)),
            pl.BlockSpec((GATE_ROWS, bm), lambda i, j: (0, i)),
        ],
        out_shape=[
            jax.ShapeDtypeStruct((m, n), BF16),
            jax.ShapeDtypeStruct((m, LANES), F32),
            jax.ShapeDtypeStruct((GATE_ROWS, m), F32),
        ],
        scratch_shapes=[pltpu.VMEM((bm, d), BF16)],
        compiler_params=_params(("arbitrary", "arbitrary")),
        name="in_projection",
    )(x, w, b, wg, bg)


def _fox_gate_kernel(g_ref, c_ref, carry_ref):
    @pl.when(pl.program_id(1) == 0)
    def _():
        carry_ref[...] = jnp.zeros_like(carry_ref)

    t = g_ref.shape[0]
    row = lax.broadcasted_iota(jnp.int32, (t, t), 0)
    col = lax.broadcasted_iota(jnp.int32, (t, t), 1)
    c = _cumsum_rows((col <= row).astype(BF16), _log_sigmoid(g_ref[...])) + carry_ref[...]
    c_ref[...] = c
    carry_ref[...] = c[t - 1:t, :]


def _fox_gate_cumsum(g, batch, seq):
    t = _tile(seq, CUM_T, LANES)
    ns = seq // t
    return pl.pallas_call(
        _fox_gate_kernel,
        grid=(batch, ns),
        in_specs=[pl.BlockSpec((t, LANES), lambda b, s: (b * ns + s, 0))],
        out_specs=pl.BlockSpec((t, LANES), lambda b, s: (b * ns + s, 0)),
        out_shape=jax.ShapeDtypeStruct((batch * seq, LANES), F32),
        scratch_shapes=[pltpu.VMEM((1, LANES), F32)],
        compiler_params=_params(("arbitrary", "arbitrary")),
        name="fox_gate_cumsum",
    )(g)


def _bias_lanes(c, ones_first):
    n = c.shape[0]
    h1, h2, h3 = (h.astype(F32) for h in _split3(c))
    lane = lax.broadcasted_iota(jnp.int32, (n, HEAD_DIM), 1)
    lo = 3 if ones_first else 0
    split = jnp.where(lane == lo, h1, jnp.where(lane == lo + 1, h2, jnp.where(lane == lo + 2, h3, 0.0)))
    ones = jnp.logical_and(lane >= 3 - lo, lane < 6 - lo)
    return jnp.where(ones, 1.0, split).astype(BF16)


def _fox_kernel(q_ref, k_ref, v_ref, c_ref, o_ref, kb_ref, m_ref, acc_ref, s_ref, *, scale, group, t):
    first_head = pl.program_id(1) * group
    unit = (lax.broadcasted_iota(jnp.int32, (t, HEAD_DIM), 1) == 0).astype(BF16)
    slices = [slice(g * HEAD_DIM, (g + 1) * HEAD_DIM) for g in range(group)]

    def head_lane(c, g):
        lane = lax.broadcasted_iota(jnp.int32, c.shape, 1)
        return jnp.sum(jnp.where(lane == first_head + g, c, 0.0), axis=-1, keepdims=True) * LOG2_E

    for g in range(group):
        kb_ref[g] = _bias_lanes(-head_lane(c_ref[...], g), ones_first=False)

    @pl.loop(0, q_ref.shape[0] // t)
    def _(qi):
        q_rows = pl.ds(pl.multiple_of(qi * t, t), t)
        ct = c_ref[q_rows, :]
        q_ext = []
        for g, sl in enumerate(slices):
            q = (q_ref[q_rows, sl].astype(F32) * (scale * LOG2_E)).astype(BF16)
            q_ext.append(jnp.concatenate([q, _bias_lanes(head_lane(ct, g), ones_first=True)], axis=1))

        def logits(slot, ki):
            k_rows = pl.ds(pl.multiple_of(ki * t, t), t)
            for g, sl in enumerate(slices):
                k_ext = jnp.concatenate([k_ref[k_rows, sl], kb_ref[g, k_rows, :]], axis=1)
                s_ref[slot, g] = lax.dot_general(q_ext[g], k_ext, NT_DIMS, preferred_element_type=F32)

        def absorb(slot, ki, diagonal):
            k_rows = pl.ds(pl.multiple_of(ki * t, t), t)
            for g, sl in enumerate(slices):
                s = s_ref[slot, g]
                if diagonal:
                    row = lax.broadcasted_iota(jnp.int32, (t, t), 0)
                    col = lax.broadcasted_iota(jnp.int32, (t, t), 1)
                    s = jnp.where(col <= row, s, -jnp.inf)
                m = m_ref[g]
                m_new = jnp.maximum(m, jnp.max(s, axis=-1, keepdims=True))
                p = jnp.exp2(s - m_new).astype(BF16)
                v_ext = jnp.concatenate([v_ref[k_rows, sl], unit], axis=1)
                acc_ref[g] = jnp.exp2(m - m_new) * acc_ref[g] + jnp.dot(p, v_ext, preferred_element_type=F32)
                m_ref[g] = m_new

        m_ref[...] = jnp.full(m_ref.shape, -jnp.inf, F32)
        acc_ref[...] = jnp.zeros_like(acc_ref)

        logits(0, 0)

        @pl.loop(0, lax.shift_right_logical(qi, 1))
        def _(pair):
            ki = 2 * pair
            logits(1, ki + 1)
            absorb(0, ki, False)
            logits(0, ki + 2)
            absorb(1, ki + 1, False)

        odd = lax.rem(qi, 2) == 1

        @pl.when(odd)
        def _():
            logits(1, qi)
            absorb(0, qi - 1, False)
            absorb(1, qi, True)

        @pl.when(jnp.logical_not(odd))
        def _():
            absorb(0, qi, True)

        for g, sl in enumerate(slices):
            acc = acc_ref[g]
            o_ref[q_rows, sl] = (acc[:, :HEAD_DIM] / acc[:, HEAD_DIM:HEAD_DIM + 1]).astype(o_ref.dtype)


def _fox_attention(proj, c, batch, seq, heads):
    t = _tile(seq, FOX_T, LANES)
    group = FOX_GROUP
    assert heads % group == 0
    hb = heads // group
    gw = group * HEAD_DIM
    return pl.pallas_call(
        functools.partial(_fox_kernel, scale=HEAD_DIM ** -0.5, group=group, t=t),
        grid=(batch, hb),
        in_specs=[
            pl.BlockSpec((seq, gw), lambda b, h: (b, h)),
            pl.BlockSpec((seq, gw), lambda b, h: (b, hb + h)),
            pl.BlockSpec((seq, gw), lambda b, h: (b, 2 * hb + h)),
            pl.BlockSpec((seq, LANES), lambda b, h: (b, 0)),
        ],
        out_specs=pl.BlockSpec((seq, gw), lambda b, h: (b, h)),
        out_shape=jax.ShapeDtypeStruct((batch * seq, heads * HEAD_DIM), BF16),
        scratch_shapes=[pltpu.VMEM((group, seq, HEAD_DIM), BF16),
                        pltpu.VMEM((group, t, 1), F32),
                        pltpu.VMEM((group, t, 2 * HEAD_DIM), F32),
                        pltpu.VMEM((2, group, t, t), F32)],
        compiler_params=_params(("arbitrary", "arbitrary")),
        name="fox_attention",
    )(proj, proj, proj, c)


def _rope(x, cos, sin_signed):
    xf = x.astype(F32)
    return xf * cos + pltpu.roll(xf, HEAD_DIM // 2, 1) * sin_signed


def _swa_kernel(sink_ref, q_ref, k_ref, v_ref, kp_ref, vp_ref, cos_ref, sin_ref, cosp_ref, sinp_ref, o_ref,
                *, scale, group):
    w = SWA_WINDOW
    first_visible = jnp.where(pl.program_id(1) == 0, w, 0)
    nblk = q_ref.shape[0] // w
    cos, sin = cos_ref[...], sin_ref[...]
    cosp, sinp = cosp_ref[...], sinp_ref[...]
    row = lax.broadcasted_iota(jnp.int32, (w, 2 * w), 0)
    col = lax.broadcasted_iota(jnp.int32, (w, 2 * w), 1)
    valid = jnp.logical_and(col > row, col - w <= row)
    valid_first = jnp.logical_and(valid, col >= first_visible)
    for kv in range(SWA_KV_HEADS):
        ksl = slice(kv * HEAD_DIM, (kv + 1) * HEAD_DIM)
        k_all = jnp.concatenate([_rope(kp_ref[:, ksl], cosp, sinp), _rope(k_ref[:, ksl], cos, sin)],
                                axis=0).astype(BF16)
        v_all = jnp.concatenate([vp_ref[:, ksl], v_ref[:, ksl]], axis=0)
        for g in range(group):
            hq = kv * group + g
            qsl = slice(hq * HEAD_DIM, (hq + 1) * HEAD_DIM)
            q = _rope(q_ref[:, qsl], cos, sin).astype(BF16)
            sink = sink_ref[hq]
            for blk in range(nblk):
                qb = q[blk * w:(blk + 1) * w, :]
                kb = k_all[blk * w:(blk + 2) * w, :]
                vb = v_all[blk * w:(blk + 2) * w, :]
                s = lax.dot_general(qb, kb, NT_DIMS, preferred_element_type=F32) * scale
                s = jnp.where(valid_first if blk == 0 else valid, s, -jnp.inf)
                m = jnp.maximum(jnp.max(s, axis=-1, keepdims=True), sink)
                p = jnp.exp(s - m)
                denom = jnp.sum(p, axis=-1, keepdims=True) + jnp.exp(sink - m)
                o = jnp.dot((p / denom).astype(BF16), vb, preferred_element_type=F32)
                o_ref[blk * w:(blk + 1) * w, qsl] = o.astype(o_ref.dtype)


def _swa_attention(proj, sinks, cos, sin_signed, batch, seq, q_col, k_col, v_col, q_heads):
    t = _tile(seq, SWA_T, SWA_WINDOW)
    nt = seq // t
    per = t // SWA_WINDOW
    qw = q_heads * HEAD_DIM
    kvw = SWA_KV_HEADS * HEAD_DIM
    assert q_col % qw == 0 and k_col % kvw == 0 and v_col % kvw == 0

    def prev(b, i):
        return jnp.maximum((b * nt + i) * per - 1, 0)

    def prev_pos(i):
        return jnp.maximum(i * per - 1, 0)

    return pl.pallas_call(
        functools.partial(_swa_kernel, scale=HEAD_DIM ** -0.5, group=q_heads // SWA_KV_HEADS),
        grid=(batch, nt),
        in_specs=[
            pl.BlockSpec(memory_space=pltpu.SMEM),
            pl.BlockSpec((t, qw), lambda b, i: (b * nt + i, q_col // qw)),
            pl.BlockSpec((t, kvw), lambda b, i: (b * nt + i, k_col // kvw)),
            pl.BlockSpec((t, kvw), lambda b, i: (b * nt + i, v_col // kvw)),
            pl.BlockSpec((SWA_WINDOW, kvw), lambda b, i: (prev(b, i), k_col // kvw)),
            pl.BlockSpec((SWA_WINDOW, kvw), lambda b, i: (prev(b, i), v_col // kvw)),
            pl.BlockSpec((t, HEAD_DIM), lambda b, i: (i, 0)),
            pl.BlockSpec((t, HEAD_DIM), lambda b, i: (i, 0)),
            pl.BlockSpec((SWA_WINDOW, HEAD_DIM), lambda b, i: (prev_pos(i), 0)),
            pl.BlockSpec((SWA_WINDOW, HEAD_DIM), lambda b, i: (prev_pos(i), 0)),
        ],
        out_specs=pl.BlockSpec((t, qw), lambda b, i: (b * nt + i, 0)),
        out_shape=jax.ShapeDtypeStruct((batch * seq, qw), BF16),
        compiler_params=_params(("arbitrary", "arbitrary")),
        name="swa_attention",
    )(sinks, proj, proj, proj, proj, proj, cos, sin_signed, cos, sin_signed)


def _mlstm_kernel(q_ref, k_ref, v_ref, og_ref, g_ref, gt_ref, o_ref, state_ref, m_ref, *, dk, dv):
    heads = MLSTM_HEADS

    @pl.when(pl.program_id(1) == 0)
    def _():
        state_ref[...] = jnp.zeros_like(state_ref)
        m_ref[...] = jnp.zeros_like(m_ref)

    n = q_ref.shape[0]
    g = g_ref[...]
    gt = gt_ref[...]
    row = lax.broadcasted_iota(jnp.int32, (n, n), 0)
    col = lax.broadcasted_iota(jnp.int32, (n, n), 1)
    causal = col <= row
    b_col_all = _cumsum_rows(causal.astype(BF16), _log_sigmoid(g))
    b_row_all = _cumsum_lanes(_log_sigmoid(gt), (row <= col).astype(BF16))
    unit = (lax.broadcasted_iota(jnp.int32, (n, LANES), 1) == 0).astype(BF16)

    for h in range(heads):
        i_col = g[:, h:h + 1]
        b_col = b_col_all[:, heads + h:heads + h + 1]
        i_row = gt[h:h + 1, :]
        b_row = b_row_all[heads + h:heads + h + 1, :]
        b_last = b_row[:, n - 1:n]
        m_prev = m_ref[h]

        d = jnp.where(causal, (b_col - b_row) + i_row, -jnp.inf)
        m_inter = b_col + m_prev
        m_t = jnp.maximum(m_inter, jnp.max(d, axis=-1, keepdims=True))
        inter = jnp.exp(m_inter - m_t)
        wgt = jnp.exp(d - m_t)

        qh = q_ref[:, h * dk:(h + 1) * dk]
        kf = k_ref[:, h * dk:(h + 1) * dk].astype(F32) * (dk ** -0.5)
        kh = kf.astype(BF16)
        v_ext = jnp.concatenate([v_ref[:, h * dv:(h + 1) * dv], unit], axis=1)
        state = state_ref[h]

        sm = wgt * lax.dot_general(qh, kh, NT_DIMS, preferred_element_type=F32)
        tot = inter * jnp.dot(qh, state.astype(BF16), preferred_element_type=F32)
        tot = tot + jnp.dot(sm.astype(BF16), v_ext, preferred_element_type=F32)
        den = jnp.maximum(jnp.abs(tot[:, dv:dv + 1]), jnp.exp(-m_t))
        hid = tot[:, :dv] * (1.0 / den)
        gate = _sigmoid(og_ref[:, h * dv:(h + 1) * dv].astype(F32))
        o_ref[:, h * dv:(h + 1) * dv] = (gate * hid).astype(o_ref.dtype)

        m_new = jnp.maximum(b_last + m_prev, jnp.max((b_last - b_row) + i_row, axis=-1, keepdims=True))
        decay = jnp.exp(b_last + m_prev - m_new)
        w_end = jnp.exp(((b_last - b_col) + i_col) - m_new)
        kw = (kf * w_end).astype(BF16)
        state_ref[h] = decay * state + lax.dot_general(kw, v_ext, TN_DIMS, preferred_element_type=F32)
        m_ref[h] = m_new


def _mlstm(proj, g, gt, batch, seq, dk, dv):
    heads = MLSTM_HEADS
    n = _tile(seq, MLSTM_L, LANES)
    nc = seq // n
    qk, vw = heads * dk, heads * dv
    assert vw % qk == 0
    r = vw // qk
    return pl.pallas_call(
        functools.partial(_mlstm_kernel, dk=dk, dv=dv),
        grid=(batch, nc),
        in_specs=[
            pl.BlockSpec((n, qk), lambda b, c: (b * nc + c, 0)),
            pl.BlockSpec((n, qk), lambda b, c: (b * nc + c, 1)),
            pl.BlockSpec((n, vw), lambda b, c: (b * nc + c, 2 // r)),
            pl.BlockSpec((n, vw), lambda b, c: (b * nc + c, 2 // r + 1)),
            pl.BlockSpec((n, LANES), lambda b, c: (b * nc + c, 0)),
            pl.BlockSpec((GATE_ROWS, n), lambda b, c: (0, b * nc + c)),
        ],
        out_specs=pl.BlockSpec((n, vw), lambda b, c: (b * nc + c, 0)),
        out_shape=jax.ShapeDtypeStruct((batch * seq, vw), BF16),
        scratch_shapes=[pltpu.VMEM((heads, dk, dv + LANES), F32), pltpu.VMEM((heads, 1, 1), F32)],
        compiler_params=_params(("arbitrary", "arbitrary")),
        name="mlstm",
    )(proj, proj, proj, proj, g, gt)


def _outproj_ln_kernel(*refs, alpha, n_in):
    h_refs, w_refs = refs[:n_in], refs[n_in:2 * n_in]
    x_ref, g_ref, b_ref, o_ref = refs[2 * n_in:]
    for r in range(o_ref.shape[0] // OUT_ROWS):
        rows = slice(r * OUT_ROWS, (r + 1) * OUT_ROWS)
        o_ref[rows, :] = sum(jnp.dot(h[rows, :], w[...], preferred_element_type=F32) for h, w in zip(h_refs, w_refs))
        o_ref[rows, :] = _layer_norm(alpha * x_ref[rows, :] + o_ref[rows, :], g_ref[...], b_ref[...])


def _out_projection_ln(hs, w, layer, x, ln_g, ln_b, alpha):
    m, d = x.shape
    bm = _tile(m, OUT_BM, LANES)
    widths = [h.shape[1] for h in hs]
    assert all(wd == widths[0] for wd in widths) and w.shape[1] == sum(widths)
    in_specs = [pl.BlockSpec((bm, wd), lambda i: (i, 0)) for wd in widths]
    in_specs += [pl.BlockSpec((None, wd, d), lambda i, k=k: (layer, k, 0)) for k, wd in enumerate(widths)]
    in_specs += [pl.BlockSpec((bm, d), lambda i: (i, 0)),
                 pl.BlockSpec((1, d), lambda i: (0, 0)),
                 pl.BlockSpec((1, d), lambda i: (0, 0))]
    return pl.pallas_call(
        functools.partial(_outproj_ln_kernel, alpha=alpha, n_in=len(hs)),
        grid=(m // bm,),
        in_specs=in_specs,
        out_specs=pl.BlockSpec((bm, d), lambda i: (i, 0)),
        out_shape=jax.ShapeDtypeStruct((m, d), F32),
        compiler_params=_params(("arbitrary",)),
        name="out_projection_ln",
    )(*hs, *([w] * len(hs)), x, ln_g, ln_b)


def _causal_conv(u, prev, w_ref, b_ref, cs):
    w0, w1, w2, b = w_ref[0:1, cs], w_ref[1:2, cs], w_ref[2:3, cs], b_ref[:, cs]
    body = b + w0 * pltpu.roll(u, 2, 0) + w1 * pltpu.roll(u, 1, 0) + w2 * u
    top = jnp.concatenate([prev, u[:HEAD_ROWS, :]], axis=0)
    head = b + w0 * pltpu.roll(top, 2, 0) + w1 * pltpu.roll(top, 1, 0) + w2 * top
    return body, head[SUBLANES:, :]


def _ffn_kernel(x_ref, wg_ref, wv_ref, cwg_ref, cwv_ref, cbg_ref, cbv_ref, wd_ref, lg_ref, lb_ref, o_ref,
                xb_ref, h0_ref, h1_ref, pg_ref, pv_ref, ug_ref, uv_ref, *, alpha, tiles_per_seq, nf):
    i = pl.program_id(0)
    j = pl.program_id(1)
    bm = x_ref.shape[0]
    h_refs = (h0_ref, h1_ref)

    def up(h_ref):
        xb = xb_ref[...]
        kept_rows = jnp.where(i % tiles_per_seq != 0, SUBLANES, 0)
        keep = lax.broadcasted_iota(jnp.int32, (SUBLANES, FFN_CHUNK), 0) < kept_rows
        chunks = [slice(c * FFN_CHUNK, (c + 1) * FFN_CHUNK) for c in range(h_ref.shape[1] // FFN_CHUNK)]
        prev_g = [jnp.where(keep, pg_ref[j, :, cs], 0.0) for cs in chunks]
        prev_v = [jnp.where(keep, pv_ref[j, :, cs], 0.0) for cs in chunks]
        for r in range(bm // FFN_ROWS):
            xr = xb[r * FFN_ROWS:(r + 1) * FFN_ROWS, :]
            for c, cs in enumerate(chunks):
                slot = (r * len(chunks) + c) % FFN_SLOTS
                ug_ref[slot] = jnp.dot(xr, wg_ref[:, cs], preferred_element_type=F32)
                uv_ref[slot] = jnp.dot(xr, wv_ref[:, cs], preferred_element_type=F32)
                ug, uv = ug_ref[slot], uv_ref[slot]
                cg, cg_top = _causal_conv(ug, prev_g[c], cwg_ref, cbg_ref, cs)
                cv, cv_top = _causal_conv(uv, prev_v[c], cwv_ref, cbv_ref, cs)
                prev_g[c] = ug[FFN_ROWS - SUBLANES:, :]
                prev_v[c] = uv[FFN_ROWS - SUBLANES:, :]
                h_ref[r * FFN_ROWS:(r + 1) * FFN_ROWS, cs] = (cg * _sigmoid(cg) * cv).astype(BF16)
                h_ref[r * FFN_ROWS:r * FFN_ROWS + HEAD_ROWS, cs] = (cg_top * _sigmoid(cg_top) * cv_top).astype(BF16)
        for c, cs in enumerate(chunks):
            pg_ref[j, :, cs] = prev_g[c]
            pv_ref[j, :, cs] = prev_v[c]

    def down(h_ref):
        o_ref[...] += jnp.dot(h_ref[...], wd_ref[...], preferred_element_type=F32)

    @pl.when(j == 0)
    def _():
        xb_ref[...] = x_ref[...].astype(BF16)
        o_ref[...] = jnp.zeros_like(o_ref)
        up(h_refs[0])

    for parity in range(2):
        @pl.when(jnp.logical_and(jnp.logical_and(j > 0, j < nf), j % 2 == parity))
        def _():
            up(h_refs[parity])
            down(h_refs[1 - parity])

    @pl.when(j == nf)
    def _():
        down(h_refs[(nf - 1) % 2])
        o_ref[...] = _layer_norm(alpha * x_ref[...] + o_ref[...], lg_ref[...], lb_ref[...])


def _conv_ffn_ln(x, w_up, w_down, layer, conv_w, conv_b, ln_g, ln_b, alpha, seq):
    m, d = x.shape
    f = w_down.shape[1]
    bm = _tile(seq, FFN_BM, 2 * SUBLANES)
    bf = _tile(f, FFN_BF, LANES)
    nf = f // bf

    def up_blk(j):
        return jnp.minimum(j, nf - 1)

    def down_blk(j):
        return jnp.maximum(j - 1, 0)

    return pl.pallas_call(
        functools.partial(_ffn_kernel, alpha=alpha, tiles_per_seq=seq // bm, nf=nf),
        grid=(m // bm, nf + 1),
        in_specs=[
            pl.BlockSpec((bm, d), lambda i, j: (i, 0)),
            pl.BlockSpec((None, d, bf), lambda i, j: (layer, 0, up_blk(j))),
            pl.BlockSpec((None, d, bf), lambda i, j: (layer, 0, nf + up_blk(j))),
            pl.BlockSpec((CONV_WIDTH, bf), lambda i, j: (0, up_blk(j))),
            pl.BlockSpec((CONV_WIDTH, bf), lambda i, j: (0, nf + up_blk(j))),
            pl.BlockSpec((1, bf), lambda i, j: (0, up_blk(j))),
            pl.BlockSpec((1, bf), lambda i, j: (0, nf + up_blk(j))),
            pl.BlockSpec((None, bf, d), lambda i, j: (layer, down_blk(j), 0)),
            pl.BlockSpec((1, d), lambda i, j: (0, 0)),
            pl.BlockSpec((1, d), lambda i, j: (0, 0)),
        ],
        out_specs=pl.BlockSpec((bm, d), lambda i, j: (i, 0)),
        out_shape=jax.ShapeDtypeStruct((m, d), F32),
        scratch_shapes=[
            pltpu.VMEM((bm, d), BF16),
            pltpu.VMEM((bm, bf), BF16),
            pltpu.VMEM((bm, bf), BF16),
            pltpu.VMEM((nf, SUBLANES, bf), F32),
            pltpu.VMEM((nf, SUBLANES, bf), F32),
            pltpu.VMEM((FFN_SLOTS, FFN_ROWS, FFN_CHUNK), F32),
            pltpu.VMEM((FFN_SLOTS, FFN_ROWS, FFN_CHUNK), F32),
        ],
        compiler_params=_params(("arbitrary", "arbitrary")),
        name="conv_ffn_ln",
    )(x, w_up, w_up, conv_w, conv_w, conv_b, conv_b, w_down, ln_g, ln_b)


def _gate_params(w, b):
    d, n = w.shape
    wg = jnp.zeros((d, LANES), BF16).at[:, :n].set(w.astype(BF16))
    bg = jnp.zeros((1, LANES), F32).at[0, :n].set(b)
    return wg, bg


def _rope_tables(seq):
    half = HEAD_DIM // 2
    inv_freq = jnp.power(ROPE_THETA, -jnp.arange(half, dtype=F32) * (2.0 / HEAD_DIM))
    ang = jnp.arange(seq, dtype=F32)[:, None] * inv_freq[None, :]
    cos, sin = jnp.cos(ang), jnp.sin(ang)
    return jnp.concatenate([cos, cos], axis=-1), jnp.concatenate([-sin, sin], axis=-1)


def kernel(x, attn_w_in, attn_b_in, attn_sinks, attn_w_out, mlstm_w_in, mlstm_b_in, mlstm_w_out, ffn_w_up,
           ffn_conv_w, ffn_conv_b, ffn_w_down, ln1_g, ln1_b, ln2_g, ln2_b):
    batch, seq, d = x.shape
    depth = ln1_g.shape[0]
    alpha = float((2 * depth) ** 0.25)
    fox_heads = d // (2 * HEAD_DIM)
    swa_heads = d // (2 * HEAD_DIM)
    fox_dim = fox_heads * HEAD_DIM
    fox_f_off = 3 * fox_dim
    dk, dv = d // (2 * MLSTM_HEADS), d // MLSTM_HEADS
    mlstm_main = 2 * MLSTM_HEADS * dk + 2 * MLSTM_HEADS * dv
    cos, sin_signed = _rope_tables(seq)

    gate_lo, gate_hi = fox_f_off, fox_f_off + fox_heads
    attn_w_gate = lax.optimization_barrier(attn_w_in[:, :, gate_lo:gate_hi])
    mlstm_w_gate = lax.optimization_barrier(mlstm_w_in[:, :, mlstm_main:])
    attn_w_main = jnp.concatenate([attn_w_in[:, :, :gate_lo], attn_w_in[:, :, gate_hi:]], axis=2).astype(BF16)
    attn_b_main = jnp.concatenate([attn_b_in[:, :gate_lo], attn_b_in[:, gate_hi:]], axis=1)
    attn_w_out_b = attn_w_out.astype(BF16)
    mlstm_w_main = mlstm_w_in[:, :, :mlstm_main].astype(BF16)
    mlstm_w_out_b = mlstm_w_out.astype(BF16)
    ffn_w_up_b = ffn_w_up.astype(BF16)
    ffn_w_down_b = ffn_w_down.astype(BF16)

    h = x.reshape(batch * seq, d)
    for layer in range(depth):
        j = layer // 2
        if layer % 2 == 0:
            gates = _gate_params(attn_w_gate[j], attn_b_in[j][gate_lo:gate_hi])
            proj, g, gt = _in_projection(h, attn_w_main, j, attn_b_main[j][None, :], *gates)
            fox = _fox_attention(proj, _fox_gate_cumsum(g, batch, seq), batch, seq, fox_heads)
            swa_q = 3 * fox_dim
            swa_k = swa_q + swa_heads * HEAD_DIM
            swa_v = swa_k + SWA_KV_HEADS * HEAD_DIM
            swa = _swa_attention(proj, attn_sinks[j], cos, sin_signed, batch, seq, swa_q, swa_k, swa_v, swa_heads)
            h = _out_projection_ln([fox, swa], attn_w_out_b, j, h, ln1_g[layer][None, :], ln1_b[layer][None, :], alpha)
        else:
            gates = _gate_params(mlstm_w_gate[j], mlstm_b_in[j][mlstm_main:])
            proj, g, gt = _in_projection(h, mlstm_w_main, j, mlstm_b_in[j][None, :mlstm_main], *gates)
            mixed = _mlstm(proj, g, gt, batch, seq, dk, dv)
            h = _out_projection_ln([mixed], mlstm_w_out_b, j, h, ln1_g[layer][None, :], ln1_b[layer][None, :], alpha)
        h = _conv_ffn_ln(h, ffn_w_up_b, ffn_w_down_b, layer, ffn_conv_w[layer], ffn_conv_b[layer][None, :],
                         ln2_g[layer][None, :], ln2_b[layer][None, :], alpha, seq)
    return h.reshape(batch, seq, d)
```

```python
import functools

import jax
import jax.numpy as jnp
from jax import lax
from jax.experimental import pallas as pl
from jax.experimental.pallas import tpu as pltpu

F32 = jnp.float32
BF16 = jnp.bfloat16

HEAD_DIM = 128
SWA_KV_HEADS = 2
SWA_WINDOW = 128
ROPE_THETA = 10000.0
MLSTM_HEADS = 8
CONV_WIDTH = 3
LN_EPS = 1e-5
LOG2_E = 1.4426950408889634

LANES = 128
SUBLANES = 8
GATE_ROWS = 16
HEAD_ROWS = 16
VMEM_LIMIT_BYTES = 60 * 1024 * 1024

PROJ_BM = 1024
PROJ_BN = 1536
OUT_BM = 1024
OUT_ROWS = 128
FFN_BM = 1024
FFN_BF = 512
FFN_CHUNK = 256
FFN_ROWS = 256
FFN_SLOTS = 2
FOX_T = 512
FOX_GROUP = 2
SWA_T = 512
CUM_T = 512
MLSTM_L = 256

NT_DIMS = (((1,), (1,)), ((), ()))
TN_DIMS = (((0,), (0,)), ((), ()))


def _tile(n, pref, unit):
    t = min(pref, n)
    while n % t or t % unit:
        t -= unit
    assert t > 0, (n, pref, unit)
    return t


def _params(sem):
    return pltpu.CompilerParams(dimension_semantics=sem, vmem_limit_bytes=VMEM_LIMIT_BYTES)


def _log_sigmoid(x):
    return jnp.minimum(x, 0.0) - jnp.log(1.0 + jnp.exp(-jnp.abs(x)))


def _sigmoid(x):
    return 1.0 / (1.0 + jnp.exp(-x))


def _split3(x):
    h1 = x.astype(BF16)
    r1 = x - h1.astype(F32)
    h2 = r1.astype(BF16)
    h3 = (r1 - h2.astype(F32)).astype(BF16)
    return h1, h2, h3


def _cumsum_rows(tri, x):
    return sum(jnp.dot(tri, h, preferred_element_type=F32) for h in _split3(x))


def _cumsum_lanes(x, tri_t):
    return sum(jnp.dot(h, tri_t, preferred_element_type=F32) for h in _split3(x))


def _layer_norm(z, g, b):
    mu = jnp.mean(z, axis=-1, keepdims=True)
    zc = z - mu
    var = jnp.mean(zc * zc, axis=-1, keepdims=True)
    return zc * lax.rsqrt(var + LN_EPS) * g + b


def _proj_kernel(x_ref, w_ref, b_ref, wg_ref, bg_ref, o_ref, g_ref, gt_ref, xb_ref):
    @pl.when(pl.program_id(1) == 0)
    def _():
        xb = x_ref[...].astype(BF16)
        xb_ref[...] = xb
        g = jnp.dot(xb, wg_ref[...], preferred_element_type=F32) + bg_ref[...]
        g_ref[...] = g
        gt_ref[...] = g.T[:GATE_ROWS, :]

    acc = jnp.dot(xb_ref[...], w_ref[...], preferred_element_type=F32)
    o_ref[...] = (acc + b_ref[...]).astype(o_ref.dtype)


def _in_projection(x, w, layer, b, wg, bg):
    m, d = x.shape
    n = w.shape[2]
    bm = _tile(m, PROJ_BM, LANES)
    bn = _tile(n, PROJ_BN, LANES)
    return pl.pallas_call(
        _proj_kernel,
        grid=(m // bm, n // bn),
        in_specs=[
            pl.BlockSpec((bm, d), lambda i, j: (i, 0)),
            pl.BlockSpec((None, d, bn), lambda i, j: (layer, 0, j)),
            pl.BlockSpec((1, bn), lambda i, j: (0, j)),
            pl.BlockSpec((d, LANES), lambda i, j: (0, 0)),
            pl.BlockSpec((1, LANES), lambda i, j: (0, 0)),
        ],
        out_specs=[
            pl.BlockSpec((bm, bn), lambda i, j: (i, j)),
            pl.BlockSpec((bm, LANES), lambda i, j: (i, 0)),
            pl.BlockSpec((GATE_ROWS, bm), lambda i, j: (0, i)),
        ],
        out_shape=[
            jax.ShapeDtypeStruct((m, n), BF16),
            jax.ShapeDtypeStruct((m, LANES), F32),
            jax.ShapeDtypeStruct((GATE_ROWS, m), F32),
        ],
        scratch_shapes=[pltpu.VMEM((bm, d), BF16)],
        compiler_params=_params(("arbitrary", "arbitrary")),
        name="in_projection",
    )(x, w, b, wg, bg)


def _fox_gate_kernel(g_ref, c_ref, carry_ref):
    @pl.when(pl.program_id(1) == 0)
    def _():
        carry_ref[...] = jnp.zeros_like(carry_ref)

    t = g_ref.shape[0]
    row = lax.broadcasted_iota(jnp.int32, (t, t), 0)
    col = lax.broadcasted_iota(jnp.int32, (t, t), 1)
    c = _cumsum_rows((col <= row).astype(BF16), _log_sigmoid(g_ref[...])) + carry_ref[...]
    c_ref[...] = c
    carry_ref[...] = c[t - 1:t, :]


def _fox_gate_cumsum(g, batch, seq):
    t = _tile(seq, CUM_T, LANES)
    ns = seq // t
    return pl.pallas_call(
        _fox_gate_kernel,
        grid=(batch, ns),
        in_specs=[pl.BlockSpec((t, LANES), lambda b, s: (b * ns + s, 0))],
        out_specs=pl.BlockSpec((t, LANES), lambda b, s: (b * ns + s, 0)),
        out_shape=jax.ShapeDtypeStruct((batch * seq, LANES), F32),
        scratch_shapes=[pltpu.VMEM((1, LANES), F32)],
        compiler_params=_params(("arbitrary", "arbitrary")),
        name="fox_gate_cumsum",
    )(g)


def _bias_lanes(c, ones_first):
    n = c.shape[0]
    h1, h2, h3 = (h.astype(F32) for h in _split3(c))
    lane = lax.broadcasted_iota(jnp.int32, (n, HEAD_DIM), 1)
    lo = 3 if ones_first else 0
    split = jnp.where(lane == lo, h1, jnp.where(lane == lo + 1, h2, jnp.where(lane == lo + 2, h3, 0.0)))
    ones = jnp.logical_and(lane >= 3 - lo, lane < 6 - lo)
    return jnp.where(ones, 1.0, split).astype(BF16)


def _fox_kernel(q_ref, k_ref, v_ref, c_ref, o_ref, kb_ref, m_ref, acc_ref, s_ref, *, scale, group, t):
    first_head = pl.program_id(1) * group
    unit = (lax.broadcasted_iota(jnp.int32, (t, HEAD_DIM), 1) == 0).astype(BF16)
    slices = [slice(g * HEAD_DIM, (g + 1) * HEAD_DIM) for g in range(group)]

    def head_lane(c, g):
        lane = lax.broadcasted_iota(jnp.int32, c.shape, 1)
        return jnp.sum(jnp.where(lane == first_head + g, c, 0.0), axis=-1, keepdims=True) * LOG2_E

    for g in range(group):
        kb_ref[g] = _bias_lanes(-head_lane(c_ref[...], g), ones_first=False)

    @pl.loop(0, q_ref.shape[0] // t)
    def _(qi):
        q_rows = pl.ds(pl.multiple_of(qi * t, t), t)
        ct = c_ref[q_rows, :]
        q_ext = []
        for g, sl in enumerate(slices):
            q = (q_ref[q_rows, sl].astype(F32) * (scale * LOG2_E)).astype(BF16)
            q_ext.append(jnp.concatenate([q, _bias_lanes(head_lane(ct, g), ones_first=True)], axis=1))

        def logits(slot, ki):
            k_rows = pl.ds(pl.multiple_of(ki * t, t), t)
            for g, sl in enumerate(slices):
                k_ext = jnp.concatenate([k_ref[k_rows, sl], kb_ref[g, k_rows, :]], axis=1)
                s_ref[slot, g] = lax.dot_general(q_ext[g], k_ext, NT_DIMS, preferred_element_type=F32)

        def absorb(slot, ki, diagonal):
            k_rows = pl.ds(pl.multiple_of(ki * t, t), t)
            for g, sl in enumerate(slices):
                s = s_ref[slot, g]
                if diagonal:
                    row = lax.broadcasted_iota(jnp.int32, (t, t), 0)
                    col = lax.broadcasted_iota(jnp.int32, (t, t), 1)
                    s = jnp.where(col <= row, s, -jnp.inf)
                m = m_ref[g]
                m_new = jnp.maximum(m, jnp.max(s, axis=-1, keepdims=True))
                p = jnp.exp2(s - m_new).astype(BF16)
                v_ext = jnp.concatenate([v_ref[k_rows, sl], unit], axis=1)
                acc_ref[g] = jnp.exp2(m - m_new) * acc_ref[g] + jnp.dot(p, v_ext, preferred_element_type=F32)
                m_ref[g] = m_new

        m_ref[...] = jnp.full(m_ref.shape, -jnp.inf, F32)
        acc_ref[...] = jnp.zeros_like(acc_ref)

        logits(0, 0)

        @pl.loop(0, lax.shift_right_logical(qi, 1))
        def _(pair):
            ki = 2 * pair
            logits(1, ki + 1)
            absorb(0, ki, False)
            logits(0, ki + 2)
            absorb(1, ki + 1, False)

        odd = lax.rem(qi, 2) == 1

        @pl.when(odd)
        def _():
            logits(1, qi)
            absorb(0, qi - 1, False)
            absorb(1, qi, True)

        @pl.when(jnp.logical_not(odd))
        def _():
            absorb(0, qi, True)

        for g, sl in enumerate(slices):
            acc = acc_ref[g]
            o_ref[q_rows, sl] = (acc[:, :HEAD_DIM] / acc[:, HEAD_DIM:HEAD_DIM + 1]).astype(o_ref.dtype)


def _fox_attention(proj, c, batch, seq, heads):
    t = _tile(seq, FOX_T, LANES)
    group = FOX_GROUP
    assert heads % group == 0
    hb = heads // group
    gw = group * HEAD_DIM
    return pl.pallas_call(
        functools.partial(_fox_kernel, scale=HEAD_DIM ** -0.5, group=group, t=t),
        grid=(batch, hb),
        in_specs=[
            pl.BlockSpec((seq, gw), lambda b, h: (b, h)),
            pl.BlockSpec((seq, gw), lambda b, h: (b, hb + h)),
            pl.BlockSpec((seq, gw), lambda b, h: (b, 2 * hb + h)),
            pl.BlockSpec((seq, LANES), lambda b, h: (b, 0)),
        ],
        out_specs=pl.BlockSpec((seq, gw), lambda b, h: (b, h)),
        out_shape=jax.ShapeDtypeStruct((batch * seq, heads * HEAD_DIM), BF16),
        scratch_shapes=[pltpu.VMEM((group, seq, HEAD_DIM), BF16),
                        pltpu.VMEM((group, t, 1), F32),
                        pltpu.VMEM((group, t, 2 * HEAD_DIM), F32),
                        pltpu.VMEM((2, group, t, t), F32)],
        compiler_params=_params(("arbitrary", "arbitrary")),
        name="fox_attention",
    )(proj, proj, proj, c)


def _rope(x, cos, sin_signed):
    xf = x.astype(F32)
    return xf * cos + pltpu.roll(xf, HEAD_DIM // 2, 1) * sin_signed


def _swa_kernel(sink_ref, q_ref, k_ref, v_ref, kp_ref, vp_ref, cos_ref, sin_ref, cosp_ref, sinp_ref, o_ref,
                *, scale, group):
    w = SWA_WINDOW
    first_visible = jnp.where(pl.program_id(1) == 0, w, 0)
    nblk = q_ref.shape[0] // w
    cos, sin = cos_ref[...], sin_ref[...]
    cosp, sinp = cosp_ref[...], sinp_ref[...]
    row = lax.broadcasted_iota(jnp.int32, (w, 2 * w), 0)
    col = lax.broadcasted_iota(jnp.int32, (w, 2 * w), 1)
    valid = jnp.logical_and(col > row, col - w <= row)
    valid_first = jnp.logical_and(valid, col >= first_visible)
    for kv in range(SWA_KV_HEADS):
        ksl = slice(kv * HEAD_DIM, (kv + 1) * HEAD_DIM)
        k_all = jnp.concatenate([_rope(kp_ref[:, ksl], cosp, sinp), _rope(k_ref[:, ksl], cos, sin)],
                                axis=0).astype(BF16)
        v_all = jnp.concatenate([vp_ref[:, ksl], v_ref[:, ksl]], axis=0)
        for g in range(group):
            hq = kv * group + g
            qsl = slice(hq * HEAD_DIM, (hq + 1) * HEAD_DIM)
            q = _rope(q_ref[:, qsl], cos, sin).astype(BF16)
            sink = sink_ref[hq]
            for blk in range(nblk):
                qb = q[blk * w:(blk + 1) * w, :]
                kb = k_all[blk * w:(blk + 2) * w, :]
                vb = v_all[blk * w:(blk + 2) * w, :]
                s = lax.dot_general(qb, kb, NT_DIMS, preferred_element_type=F32) * scale
                s = jnp.where(valid_first if blk == 0 else valid, s, -jnp.inf)
                m = jnp.maximum(jnp.max(s, axis=-1, keepdims=True), sink)
                p = jnp.exp(s - m)
                denom = jnp.sum(p, axis=-1, keepdims=True) + jnp.exp(sink - m)
                o = jnp.dot((p / denom).astype(BF16), vb, preferred_element_type=F32)
                o_ref[blk * w:(blk + 1) * w, qsl] = o.astype(o_ref.dtype)


def _swa_attention(proj, sinks, cos, sin_signed, batch, seq, q_col, k_col, v_col, q_heads):
    t = _tile(seq, SWA_T, SWA_WINDOW)
    nt = seq // t
    per = t // SWA_WINDOW
    qw = q_heads * HEAD_DIM
    kvw = SWA_KV_HEADS * HEAD_DIM
    assert q_col % qw == 0 and k_col % kvw == 0 and v_col % kvw == 0

    def prev(b, i):
        return jnp.maximum((b * nt + i) * per - 1, 0)

    def prev_pos(i):
        return jnp.maximum(i * per - 1, 0)

    return pl.pallas_call(
        functools.partial(_swa_kernel, scale=HEAD_DIM ** -0.5, group=q_heads // SWA_KV_HEADS),
        grid=(batch, nt),
        in_specs=[
            pl.BlockSpec(memory_space=pltpu.SMEM),
            pl.BlockSpec((t, qw), lambda b, i: (b * nt + i, q_col // qw)),
            pl.BlockSpec((t, kvw), lambda b, i: (b * nt + i, k_col // kvw)),
            pl.BlockSpec((t, kvw), lambda b, i: (b * nt + i, v_col // kvw)),
            pl.BlockSpec((SWA_WINDOW, kvw), lambda b, i: (prev(b, i), k_col // kvw)),
            pl.BlockSpec((SWA_WINDOW, kvw), lambda b, i: (prev(b, i), v_col // kvw)),
            pl.BlockSpec((t, HEAD_DIM), lambda b, i: (i, 0)),
            pl.BlockSpec((t, HEAD_DIM), lambda b, i: (i, 0)),
            pl.BlockSpec((SWA_WINDOW, HEAD_DIM), lambda b, i: (prev_pos(i), 0)),
            pl.BlockSpec((SWA_WINDOW, HEAD_DIM), lambda b, i: (prev_pos(i), 0)),
        ],
        out_specs=pl.BlockSpec((t, qw), lambda b, i: (b * nt + i, 0)),
        out_shape=jax.ShapeDtypeStruct((batch * seq, qw), BF16),
        compiler_params=_params(("arbitrary", "arbitrary")),
        name="swa_attention",
    )(sinks, proj, proj, proj, proj, proj, cos, sin_signed, cos, sin_signed)


def _cummax_rows(x):
    n = x.shape[0]
    row = lax.broadcasted_iota(jnp.int32, x.shape, 0)
    shift = 1
    while shift < n:
        x = jnp.maximum(x, jnp.where(row >= shift, pltpu.roll(x, shift, 0), -jnp.inf))
        shift *= 2
    return x


def _mlstm_kernel(q_ref, k_ref, v_ref, og_ref, g_ref, gt_ref, o_ref, state_ref, m_ref, *, dk, dv):
    heads = MLSTM_HEADS

    @pl.when(pl.program_id(1) == 0)
    def _():
        state_ref[...] = jnp.zeros_like(state_ref)
        m_ref[...] = jnp.zeros_like(m_ref)

    n = q_ref.shape[0]
    g = g_ref[...]
    gt = gt_ref[...]
    row = lax.broadcasted_iota(jnp.int32, (n, n), 0)
    col = lax.broadcasted_iota(jnp.int32, (n, n), 1)
    causal = col <= row
    b_row_all = _cumsum_lanes(_log_sigmoid(gt), (row <= col).astype(BF16))
    b_all = pltpu.roll(_cumsum_rows(causal.astype(BF16), _log_sigmoid(g)), LANES - heads, 1)
    kc_all = g - b_all
    kcmax_all = _cummax_rows(kc_all)
    spread = (lax.broadcasted_iota(jnp.int32, (LANES, heads * LANES), 1) // LANES
              == lax.broadcasted_iota(jnp.int32, (LANES, heads * LANES), 0)).astype(BF16)
    b_wide, kc_wide, kcmax_wide = (
        sum(jnp.dot(term, spread, preferred_element_type=F32) for term in _split3(x))
        for x in (b_all, kc_all, kcmax_all))
    ones = jnp.ones((n, LANES), BF16)
    tiles = n // LANES

    def wide(x, reps):
        return jnp.concatenate([x] * reps, axis=1)

    for h in range(heads):
        hs = slice(h * LANES, (h + 1) * LANES)
        b_col, kc_col, kcmax_col = b_wide[:, hs], kc_wide[:, hs], kcmax_wide[:, hs]
        i_row = gt[h:h + 1, :]
        b_row = b_row_all[heads + h:heads + h + 1, :]
        b_last = b_row[:, n - 1:n]
        m_prev = m_ref[h]

        m_inter = b_col + m_prev
        m_t = jnp.maximum(m_inter, b_col + kcmax_col)
        inter = jnp.exp(m_inter - m_t)
        d = jnp.where(causal, (wide(b_col, tiles) - b_row) + i_row, -jnp.inf)
        wgt = jnp.exp(d - wide(m_t, tiles))

        qh = q_ref[:, h * dk:(h + 1) * dk]
        kf = k_ref[:, h * dk:(h + 1) * dk].astype(F32) * (dk ** -0.5)
        kh = kf.astype(BF16)
        v_ext = jnp.concatenate([v_ref[:, h * dv:(h + 1) * dv], ones], axis=1)
        state = state_ref[h]

        sm = wgt * lax.dot_general(qh, kh, NT_DIMS, preferred_element_type=F32)
        tot = wide(inter, dv // LANES + 1) * jnp.dot(qh, state.astype(BF16), preferred_element_type=F32)
        tot = tot + jnp.dot(sm.astype(BF16), v_ext, preferred_element_type=F32)
        den = jnp.maximum(jnp.abs(tot[:, dv:]), jnp.exp(-m_t))
        hid = tot[:, :dv] * wide(1.0 / den, dv // LANES)
        gate = _sigmoid(og_ref[:, h * dv:(h + 1) * dv].astype(F32))
        o_ref[:, h * dv:(h + 1) * dv] = (gate * hid).astype(o_ref.dtype)

        m_new = jnp.maximum(b_last + m_prev, jnp.max((b_last - b_row) + i_row, axis=-1, keepdims=True))
        decay = jnp.exp(b_last + m_prev - m_new)
        w_end = jnp.exp((b_last + kc_col) - m_new)
        kw = (kf * w_end).astype(BF16)
        state_ref[h] = decay * state + lax.dot_general(kw, v_ext, TN_DIMS, preferred_element_type=F32)
        m_ref[h] = m_new


def _mlstm(proj, g, gt, batch, seq, dk, dv):
    heads = MLSTM_HEADS
    n = _tile(seq, MLSTM_L, LANES)
    nc = seq // n
    qk, vw = heads * dk, heads * dv
    assert vw % qk == 0
    r = vw // qk
    return pl.pallas_call(
        functools.partial(_mlstm_kernel, dk=dk, dv=dv),
        grid=(batch, nc),
        in_specs=[
            pl.BlockSpec((n, qk), lambda b, c: (b * nc + c, 0)),
            pl.BlockSpec((n, qk), lambda b, c: (b * nc + c, 1)),
            pl.BlockSpec((n, vw), lambda b, c: (b * nc + c, 2 // r)),
            pl.BlockSpec((n, vw), lambda b, c: (b * nc + c, 2 // r + 1)),
            pl.BlockSpec((n, LANES), lambda b, c: (b * nc + c, 0)),
            pl.BlockSpec((GATE_ROWS, n), lambda b, c: (0, b * nc + c)),
        ],
        out_specs=pl.BlockSpec((n, vw), lambda b, c: (b * nc + c, 0)),
        out_shape=jax.ShapeDtypeStruct((batch * seq, vw), BF16),
        scratch_shapes=[pltpu.VMEM((heads, dk, dv + LANES), F32), pltpu.VMEM((heads, 1, 1), F32)],
        compiler_params=_params(("arbitrary", "arbitrary")),
        name="mlstm",
    )(proj, proj, proj, proj, g, gt)


def _outproj_ln_kernel(*refs, alpha, n_in):
    h_refs, w_refs = refs[:n_in], refs[n_in:2 * n_in]
    x_ref, g_ref, b_ref, o_ref = refs[2 * n_in:]
    for r in range(o_ref.shape[0] // OUT_ROWS):
        rows = slice(r * OUT_ROWS, (r + 1) * OUT_ROWS)
        o_ref[rows, :] = sum(jnp.dot(h[rows, :], w[...], preferred_element_type=F32) for h, w in zip(h_refs, w_refs))
        o_ref[rows, :] = _layer_norm(alpha * x_ref[rows, :] + o_ref[rows, :], g_ref[...], b_ref[...])


def _out_projection_ln(hs, w, layer, x, ln_g, ln_b, alpha):
    m, d = x.shape
    bm = _tile(m, OUT_BM, LANES)
    widths = [h.shape[1] for h in hs]
    assert all(wd == widths[0] for wd in widths) and w.shape[1] == sum(widths)
    in_specs = [pl.BlockSpec((bm, wd), lambda i: (i, 0)) for wd in widths]
    in_specs += [pl.BlockSpec((None, wd, d), lambda i, k=k: (layer, k, 0), pipeline_mode=pl.Buffered(1))
                 for k, wd in enumerate(widths)]
    in_specs += [pl.BlockSpec((bm, d), lambda i: (i, 0)),
                 pl.BlockSpec((1, d), lambda i: (0, 0)),
                 pl.BlockSpec((1, d), lambda i: (0, 0))]
    return pl.pallas_call(
        functools.partial(_outproj_ln_kernel, alpha=alpha, n_in=len(hs)),
        grid=(m // bm,),
        in_specs=in_specs,
        out_specs=pl.BlockSpec((bm, d), lambda i: (i, 0)),
        out_shape=jax.ShapeDtypeStruct((m, d), F32),
        compiler_params=_params(("arbitrary",)),
        name="out_projection_ln",
    )(*hs, *([w] * len(hs)), x, ln_g, ln_b)


def _causal_conv(u, prev, w_ref, b_ref, cs):
    w0, w1, w2, b = w_ref[0:1, cs], w_ref[1:2, cs], w_ref[2:3, cs], b_ref[:, cs]
    body = b + w0 * pltpu.roll(u, 2, 0) + w1 * pltpu.roll(u, 1, 0) + w2 * u
    top = jnp.concatenate([prev, u[:HEAD_ROWS, :]], axis=0)
    head = b + w0 * pltpu.roll(top, 2, 0) + w1 * pltpu.roll(top, 1, 0) + w2 * top
    return body, head[SUBLANES:, :]


def _ffn_kernel(x_ref, wg_ref, wv_ref, cwg_ref, cwv_ref, cbg_ref, cbv_ref, wd_ref, lg_ref, lb_ref, o_ref,
                xb_ref, h0_ref, h1_ref, pg_ref, pv_ref, ug_ref, uv_ref, *, alpha, tiles_per_seq, nf):
    i = pl.program_id(0)
    j = pl.program_id(1)
    bm = x_ref.shape[0]
    h_refs = (h0_ref, h1_ref)

    def up(h_ref, first_step=False):
        kept_rows = jnp.where(i % tiles_per_seq != 0, SUBLANES, 0)
        keep = lax.broadcasted_iota(jnp.int32, (SUBLANES, FFN_CHUNK), 0) < kept_rows
        chunks = [slice(c * FFN_CHUNK, (c + 1) * FFN_CHUNK) for c in range(h_ref.shape[1] // FFN_CHUNK)]
        prev_g = [jnp.where(keep, pg_ref[j, :, cs], 0.0) for cs in chunks]
        prev_v = [jnp.where(keep, pv_ref[j, :, cs], 0.0) for cs in chunks]
        for r in range(bm // FFN_ROWS):
            rows = slice(r * FFN_ROWS, (r + 1) * FFN_ROWS)
            if first_step:
                xb_ref[rows, :] = x_ref[rows, :].astype(BF16)
            xr = xb_ref[rows, :]
            for c, cs in enumerate(chunks):
                slot = (r * len(chunks) + c) % FFN_SLOTS
                ug_ref[slot] = jnp.dot(xr, wg_ref[:, cs], preferred_element_type=F32)
                uv_ref[slot] = jnp.dot(xr, wv_ref[:, cs], preferred_element_type=F32)
                ug, uv = ug_ref[slot], uv_ref[slot]
                cg, cg_top = _causal_conv(ug, prev_g[c], cwg_ref, cbg_ref, cs)
                cv, cv_top = _causal_conv(uv, prev_v[c], cwv_ref, cbv_ref, cs)
                prev_g[c] = ug[FFN_ROWS - SUBLANES:, :]
                prev_v[c] = uv[FFN_ROWS - SUBLANES:, :]
                h_ref[r * FFN_ROWS:(r + 1) * FFN_ROWS, cs] = (cg * _sigmoid(cg) * cv).astype(BF16)
                h_ref[r * FFN_ROWS:r * FFN_ROWS + HEAD_ROWS, cs] = (cg_top * _sigmoid(cg_top) * cv_top).astype(BF16)
        for c, cs in enumerate(chunks):
            pg_ref[j, :, cs] = prev_g[c]
            pv_ref[j, :, cs] = prev_v[c]

    def down(h_ref):
        o_ref[...] += jnp.dot(h_ref[...], wd_ref[...], preferred_element_type=F32)

    @pl.when(j == 0)
    def _():
        o_ref[...] = jnp.zeros_like(o_ref)
        up(h_refs[0], first_step=True)

    for parity in range(2):
        @pl.when(jnp.logical_and(jnp.logical_and(j > 0, j < nf), j % 2 == parity))
        def _():
            up(h_refs[parity])
            down(h_refs[1 - parity])

    @pl.when(j == nf)
    def _():
        h_ref = h_refs[(nf - 1) % 2]
        for r in range(bm // FFN_ROWS):
            rows = slice(r * FFN_ROWS, (r + 1) * FFN_ROWS)
            o_ref[rows, :] += jnp.dot(h_ref[rows, :], wd_ref[...], preferred_element_type=F32)
            o_ref[rows, :] = _layer_norm(alpha * x_ref[rows, :] + o_ref[rows, :], lg_ref[...], lb_ref[...])


def _conv_ffn_ln(x, w_up, w_down, layer, conv_w, conv_b, ln_g, ln_b, alpha, seq):
    m, d = x.shape
    f = w_down.shape[1]
    bm = _tile(seq, FFN_BM, 2 * SUBLANES)
    bf = _tile(f, FFN_BF, LANES)
    nf = f // bf

    def up_blk(j):
        return jnp.minimum(j, nf - 1)

    def down_blk(j):
        return jnp.maximum(j - 1, 0)

    return pl.pallas_call(
        functools.partial(_ffn_kernel, alpha=alpha, tiles_per_seq=seq // bm, nf=nf),
        grid=(m // bm, nf + 1),
        in_specs=[
            pl.BlockSpec((bm, d), lambda i, j: (i, 0)),
            pl.BlockSpec((None, d, bf), lambda i, j: (layer, 0, up_blk(j))),
            pl.BlockSpec((None, d, bf), lambda i, j: (layer, 0, nf + up_blk(j))),
            pl.BlockSpec((CONV_WIDTH, bf), lambda i, j: (0, up_blk(j))),
            pl.BlockSpec((CONV_WIDTH, bf), lambda i, j: (0, nf + up_blk(j))),
            pl.BlockSpec((1, bf), lambda i, j: (0, up_blk(j))),
            pl.BlockSpec((1, bf), lambda i, j: (0, nf + up_blk(j))),
            pl.BlockSpec((None, bf, d), lambda i, j: (layer, down_blk(j), 0)),
            pl.BlockSpec((1, d), lambda i, j: (0, 0)),
            pl.BlockSpec((1, d), lambda i, j: (0, 0)),
        ],
        out_specs=pl.BlockSpec((bm, d), lambda i, j: (i, 0)),
        out_shape=jax.ShapeDtypeStruct((m, d), F32),
        scratch_shapes=[
            pltpu.VMEM((bm, d), BF16),
            pltpu.VMEM((bm, bf), BF16),
            pltpu.VMEM((bm, bf), BF16),
            pltpu.VMEM((nf, SUBLANES, bf), F32),
            pltpu.VMEM((nf, SUBLANES, bf), F32),
            pltpu.VMEM((FFN_SLOTS, FFN_ROWS, FFN_CHUNK), F32),
            pltpu.VMEM((FFN_SLOTS, FFN_ROWS, FFN_CHUNK), F32),
        ],
        compiler_params=_params(("arbitrary", "arbitrary")),
        name="conv_ffn_ln",
    )(x, w_up, w_up, conv_w, conv_w, conv_b, conv_b, w_down, ln_g, ln_b)


def _gate_params(w, b):
    d, n = w.shape
    wg = jnp.zeros((d, LANES), BF16).at[:, :n].set(w.astype(BF16))
    bg = jnp.zeros((1, LANES), F32).at[0, :n].set(b)
    return wg, bg


def _rope_tables(seq):
    half = HEAD_DIM // 2
    inv_freq = jnp.power(ROPE_THETA, -jnp.arange(half, dtype=F32) * (2.0 / HEAD_DIM))
    ang = jnp.arange(seq, dtype=F32)[:, None] * inv_freq[None, :]
    cos, sin = jnp.cos(ang), jnp.sin(ang)
    return jnp.concatenate([cos, cos], axis=-1), jnp.concatenate([-sin, sin], axis=-1)


def kernel(x, attn_w_in, attn_b_in, attn_sinks, attn_w_out, mlstm_w_in, mlstm_b_in, mlstm_w_out, ffn_w_up,
           ffn_conv_w, ffn_conv_b, ffn_w_down, ln1_g, ln1_b, ln2_g, ln2_b):
    batch, seq, d = x.shape
    depth = ln1_g.shape[0]
    alpha = float((2 * depth) ** 0.25)
    fox_heads = d // (2 * HEAD_DIM)
    swa_heads = d // (2 * HEAD_DIM)
    fox_dim = fox_heads * HEAD_DIM
    fox_f_off = 3 * fox_dim
    dk, dv = d // (2 * MLSTM_HEADS), d // MLSTM_HEADS
    mlstm_main = 2 * MLSTM_HEADS * dk + 2 * MLSTM_HEADS * dv
    cos, sin_signed = _rope_tables(seq)

    gate_lo, gate_hi = fox_f_off, fox_f_off + fox_heads
    attn_w_gate = lax.optimization_barrier(attn_w_in[:, :, gate_lo:gate_hi])
    mlstm_w_gate = lax.optimization_barrier(mlstm_w_in[:, :, mlstm_main:])
    attn_w_main = jnp.concatenate([attn_w_in[:, :, :gate_lo], attn_w_in[:, :, gate_hi:]], axis=2).astype(BF16)
    attn_b_main = jnp.concatenate([attn_b_in[:, :gate_lo], attn_b_in[:, gate_hi:]], axis=1)
    attn_w_out_b = attn_w_out.astype(BF16)
    mlstm_w_main = mlstm_w_in[:, :, :mlstm_main].astype(BF16)
    mlstm_w_out_b = mlstm_w_out.astype(BF16)
    ffn_w_up_b = ffn_w_up.astype(BF16)
    ffn_w_down_b = ffn_w_down.astype(BF16)

    h = x.reshape(batch * seq, d)
    for layer in range(depth):
        j = layer // 2
        if layer % 2 == 0:
            gates = _gate_params(attn_w_gate[j], attn_b_in[j][gate_lo:gate_hi])
            proj, g, gt = _in_projection(h, attn_w_main, j, attn_b_main[j][None, :], *gates)
            fox = _fox_attention(proj, _fox_gate_cumsum(g, batch, seq), batch, seq, fox_heads)
            swa_q = 3 * fox_dim
            swa_k = swa_q + swa_heads * HEAD_DIM
            swa_v = swa_k + SWA_KV_HEADS * HEAD_DIM
            swa = _swa_attention(proj, attn_sinks[j], cos, sin_signed, batch, seq, swa_q, swa_k, swa_v, swa_heads)
            h = _out_projection_ln([fox, swa], attn_w_out_b, j, h, ln1_g[layer][None, :], ln1_b[layer][None, :], alpha)
        else:
            gates = _gate_params(mlstm_w_gate[j], mlstm_b_in[j][mlstm_main:])
            proj, g, gt = _in_projection(h, mlstm_w_main, j, mlstm_b_in[j][None, :mlstm_main], *gates)
            mixed = _mlstm(proj, g, gt, batch, seq, dk, dv)
            h = _out_projection_ln([mixed], mlstm_w_out_b, j, h, ln1_g[layer][None, :], ln1_b[layer][None, :], alpha)
        h = _conv_ffn_ln(h, ffn_w_up_b, ffn_w_down_b, layer, ffn_conv_w[layer], ffn_conv_b[layer][None, :],
                         ln2_g[layer][None, :], ln2_b[layer][None, :], alpha, seq)
    return h.reshape(batch, seq, d)
```

```python
import functools

import jax
import jax.numpy as jnp
from jax import lax
from jax.experimental import pallas as pl
from jax.experimental.pallas import tpu as pltpu

F32 = jnp.float32
BF16 = jnp.bfloat16

HEAD_DIM = 128
SWA_KV_HEADS = 2
SWA_WINDOW = 128
ROPE_THETA = 10000.0
MLSTM_HEADS = 8
CONV_WIDTH = 3
LN_EPS = 1e-5
LOG2_E = 1.4426950408889634

LANES = 128
SUBLANES = 8
GATE_ROWS = 16
HEAD_ROWS = 16
VMEM_LIMIT_BYTES = 60 * 1024 * 1024

PROJ_BM = 1024
PROJ_BN = 1536
OUT_BM = 512
OUT_ROWS = 128
FFN_BM = 1024
FFN_BF = 512
FFN_CHUNK = 256
FFN_ROWS = 256
FFN_SLOTS = 2
FOX_T = 512
FOX_GROUP = 2
SWA_T = 512
CUM_T = 512
MLSTM_L = 256

NT_DIMS = (((1,), (1,)), ((), ()))
TN_DIMS = (((0,), (0,)), ((), ()))


def _tile(n, pref, unit):
    t = min(pref, n)
    while n % t or t % unit:
        t -= unit
    assert t > 0, (n, pref, unit)
    return t


def _params(sem):
    return pltpu.CompilerParams(dimension_semantics=sem, vmem_limit_bytes=VMEM_LIMIT_BYTES)


def _log_sigmoid(x):
    return jnp.minimum(x, 0.0) - jnp.log(1.0 + jnp.exp(-jnp.abs(x)))


def _sigmoid(x):
    return 1.0 / (1.0 + jnp.exp(-x))


def _split3(x):
    h1 = x.astype(BF16)
    r1 = x - h1.astype(F32)
    h2 = r1.astype(BF16)
    h3 = (r1 - h2.astype(F32)).astype(BF16)
    return h1, h2, h3


def _cumsum_rows(tri, x):
    return sum(jnp.dot(tri, h, preferred_element_type=F32) for h in _split3(x))


def _cumsum_lanes(x, tri_t):
    return sum(jnp.dot(h, tri_t, preferred_element_type=F32) for h in _split3(x))


def _layer_norm(z, g, b):
    mu = jnp.mean(z, axis=-1, keepdims=True)
    zc = z - mu
    var = jnp.mean(zc * zc, axis=-1, keepdims=True)
    return zc * lax.rsqrt(var + LN_EPS) * g + b


def _proj_kernel(x_ref, w_ref, b_ref, wg_ref, bg_ref, o_ref, g_ref, gt_ref, xb_ref):
    @pl.when(pl.program_id(1) == 0)
    def _():
        xb = x_ref[...].astype(BF16)
        xb_ref[...] = xb
        g = jnp.dot(xb, wg_ref[...], preferred_element_type=F32) + bg_ref[...]
        g_ref[...] = g
        gt_ref[...] = g.T[:GATE_ROWS, :]

    acc = jnp.dot(xb_ref[...], w_ref[...], preferred_element_type=F32)
    o_ref[...] = (acc + b_ref[...]).astype(o_ref.dtype)


def _in_projection(x, w, layer, b, wg, bg):
    m, d = x.shape
    n = w.shape[2]
    bm = _tile(m, PROJ_BM, LANES)
    bn = _tile(n, PROJ_BN, LANES)
    return pl.pallas_call(
        _proj_kernel,
        grid=(m // bm, n // bn),
        in_specs=[
            pl.BlockSpec((bm, d), lambda i, j: (i, 0)),
            pl.BlockSpec((None, d, bn), lambda i, j: (layer, 0, j)),
            pl.BlockSpec((1, bn), lambda i, j: (0, j)),
            pl.BlockSpec((d, LANES), lambda i, j: (0, 0)),
            pl.BlockSpec((1, LANES), lambda i, j: (0, 0)),
        ],
        out_specs=[
            pl.BlockSpec((bm, bn), lambda i, j: (i, j)),
            pl.BlockSpec((bm, LANES), lambda i, j: (i, 0)),
            pl.BlockSpec((GATE_ROWS, bm), lambda i, j: (0, i)),
        ],
        out_shape=[
            jax.ShapeDtypeStruct((m, n), BF16),
            jax.ShapeDtypeStruct((m, LANES), F32),
            jax.ShapeDtypeStruct((GATE_ROWS, m), F32),
        ],
        scratch_shapes=[pltpu.VMEM((bm, d), BF16)],
        compiler_params=_params(("arbitrary", "arbitrary")),
        name="in_projection",
    )(x, w, b, wg, bg)


def _fox_gate_kernel(g_ref, c_ref, carry_ref):
    @pl.when(pl.program_id(1) == 0)
    def _():
        carry_ref[...] = jnp.zeros_like(carry_ref)

    t = g_ref.shape[0]
    row = lax.broadcasted_iota(jnp.int32, (t, t), 0)
    col = lax.broadcasted_iota(jnp.int32, (t, t), 1)
    c = _cumsum_rows((col <= row).astype(BF16), _log_sigmoid(g_ref[...])) + carry_ref[...]
    c_ref[...] = c
    carry_ref[...] = c[t - 1:t, :]


def _fox_gate_cumsum(g, batch, seq):
    t = _tile(seq, CUM_T, LANES)
    ns = seq // t
    return pl.pallas_call(
        _fox_gate_kernel,
        grid=(batch, ns),
        in_specs=[pl.BlockSpec((t, LANES), lambda b, s: (b * ns + s, 0))],
        out_specs=pl.BlockSpec((t, LANES), lambda b, s: (b * ns + s, 0)),
        out_shape=jax.ShapeDtypeStruct((batch * seq, LANES), F32),
        scratch_shapes=[pltpu.VMEM((1, LANES), F32)],
        compiler_params=_params(("arbitrary", "arbitrary")),
        name="fox_gate_cumsum",
    )(g)


def _bias_lanes(c, ones_first):
    n = c.shape[0]
    h1, h2, h3 = (h.astype(F32) for h in _split3(c))
    lane = lax.broadcasted_iota(jnp.int32, (n, HEAD_DIM), 1)
    lo = 3 if ones_first else 0
    split = jnp.where(lane == lo, h1, jnp.where(lane == lo + 1, h2, jnp.where(lane == lo + 2, h3, 0.0)))
    ones = jnp.logical_and(lane >= 3 - lo, lane < 6 - lo)
    return jnp.where(ones, 1.0, split).astype(BF16)


def _fox_kernel(q_ref, k_ref, v_ref, c_ref, o_ref, kb_ref, m_ref, acc_ref, s_ref, *, scale, group, t):
    first_head = pl.program_id(1) * group
    unit = (lax.broadcasted_iota(jnp.int32, (t, HEAD_DIM), 1) == 0).astype(BF16)
    slices = [slice(g * HEAD_DIM, (g + 1) * HEAD_DIM) for g in range(group)]

    def head_lane(c, g):
        lane = lax.broadcasted_iota(jnp.int32, c.shape, 1)
        return jnp.sum(jnp.where(lane == first_head + g, c, 0.0), axis=-1, keepdims=True) * LOG2_E

    for g in range(group):
        kb_ref[g] = _bias_lanes(-head_lane(c_ref[...], g), ones_first=False)

    @pl.loop(0, q_ref.shape[0] // t)
    def _(qi):
        q_rows = pl.ds(pl.multiple_of(qi * t, t), t)
        ct = c_ref[q_rows, :]
        q_ext = []
        for g, sl in enumerate(slices):
            q = (q_ref[q_rows, sl].astype(F32) * (scale * LOG2_E)).astype(BF16)
            q_ext.append(jnp.concatenate([q, _bias_lanes(head_lane(ct, g), ones_first=True)], axis=1))

        def logits(slot, ki):
            k_rows = pl.ds(pl.multiple_of(ki * t, t), t)
            for g, sl in enumerate(slices):
                k_ext = jnp.concatenate([k_ref[k_rows, sl], kb_ref[g, k_rows, :]], axis=1)
                s_ref[slot, g] = lax.dot_general(q_ext[g], k_ext, NT_DIMS, preferred_element_type=F32)

        def absorb(slot, ki, diagonal):
            k_rows = pl.ds(pl.multiple_of(ki * t, t), t)
            for g, sl in enumerate(slices):
                s = s_ref[slot, g]
                if diagonal:
                    row = lax.broadcasted_iota(jnp.int32, (t, t), 0)
                    col = lax.broadcasted_iota(jnp.int32, (t, t), 1)
                    s = jnp.where(col <= row, s, -jnp.inf)
                m = m_ref[g]
                m_new = jnp.maximum(m, jnp.max(s, axis=-1, keepdims=True))
                p = jnp.exp2(s - m_new).astype(BF16)
                v_ext = jnp.concatenate([v_ref[k_rows, sl], unit], axis=1)
                acc_ref[g] = jnp.exp2(m - m_new) * acc_ref[g] + jnp.dot(p, v_ext, preferred_element_type=F32)
                m_ref[g] = m_new

        m_ref[...] = jnp.full(m_ref.shape, -jnp.inf, F32)
        acc_ref[...] = jnp.zeros_like(acc_ref)

        logits(0, 0)

        @pl.loop(0, lax.shift_right_logical(qi, 1))
        def _(pair):
            ki = 2 * pair
            logits(1, ki + 1)
            absorb(0, ki, False)
            logits(0, ki + 2)
            absorb(1, ki + 1, False)

        odd = lax.rem(qi, 2) == 1

        @pl.when(odd)
        def _():
            logits(1, qi)
            absorb(0, qi - 1, False)
            absorb(1, qi, True)

        @pl.when(jnp.logical_not(odd))
        def _():
            absorb(0, qi, True)

        for g, sl in enumerate(slices):
            acc = acc_ref[g]
            o_ref[q_rows, sl] = (acc[:, :HEAD_DIM] / acc[:, HEAD_DIM:HEAD_DIM + 1]).astype(o_ref.dtype)


def _fox_attention(proj, c, batch, seq, heads):
    t = _tile(seq, FOX_T, LANES)
    group = FOX_GROUP
    assert heads % group == 0
    hb = heads // group
    gw = group * HEAD_DIM
    return pl.pallas_call(
        functools.partial(_fox_kernel, scale=HEAD_DIM ** -0.5, group=group, t=t),
        grid=(batch, hb),
        in_specs=[
            pl.BlockSpec((seq, gw), lambda b, h: (b, h)),
            pl.BlockSpec((seq, gw), lambda b, h: (b, hb + h)),
            pl.BlockSpec((seq, gw), lambda b, h: (b, 2 * hb + h)),
            pl.BlockSpec((seq, LANES), lambda b, h: (b, 0)),
        ],
        out_specs=pl.BlockSpec((seq, gw), lambda b, h: (b, h)),
        out_shape=jax.ShapeDtypeStruct((batch * seq, heads * HEAD_DIM), BF16),
        scratch_shapes=[pltpu.VMEM((group, seq, HEAD_DIM), BF16),
                        pltpu.VMEM((group, t, 1), F32),
                        pltpu.VMEM((group, t, 2 * HEAD_DIM), F32),
                        pltpu.VMEM((2, group, t, t), F32)],
        compiler_params=_params(("arbitrary", "arbitrary")),
        name="fox_attention",
    )(proj, proj, proj, c)


def _rope(x, cos, sin_signed):
    xf = x.astype(F32)
    return xf * cos + pltpu.roll(xf, HEAD_DIM // 2, 1) * sin_signed


def _swa_kernel(sink_ref, q_ref, k_ref, v_ref, kp_ref, vp_ref, cos_ref, sin_ref, cosp_ref, sinp_ref, o_ref,
                *, scale, group):
    w = SWA_WINDOW
    first_visible = jnp.where(pl.program_id(1) == 0, w, 0)
    nblk = q_ref.shape[0] // w
    cos, sin = cos_ref[...], sin_ref[...]
    cosp, sinp = cosp_ref[...], sinp_ref[...]
    row = lax.broadcasted_iota(jnp.int32, (w, 2 * w), 0)
    col = lax.broadcasted_iota(jnp.int32, (w, 2 * w), 1)
    valid = jnp.logical_and(col > row, col - w <= row)
    valid_first = jnp.logical_and(valid, col >= first_visible)
    for kv in range(SWA_KV_HEADS):
        ksl = slice(kv * HEAD_DIM, (kv + 1) * HEAD_DIM)
        k_all = jnp.concatenate([_rope(kp_ref[:, ksl], cosp, sinp), _rope(k_ref[:, ksl], cos, sin)],
                                axis=0).astype(BF16)
        v_all = jnp.concatenate([vp_ref[:, ksl], v_ref[:, ksl]], axis=0)
        for g in range(group):
            hq = kv * group + g
            qsl = slice(hq * HEAD_DIM, (hq + 1) * HEAD_DIM)
            q = _rope(q_ref[:, qsl], cos, sin).astype(BF16)
            sink = sink_ref[hq]
            for blk in range(nblk):
                qb = q[blk * w:(blk + 1) * w, :]
                kb = k_all[blk * w:(blk + 2) * w, :]
                vb = v_all[blk * w:(blk + 2) * w, :]
                s = lax.dot_general(qb, kb, NT_DIMS, preferred_element_type=F32) * scale
                s = jnp.where(valid_first if blk == 0 else valid, s, -jnp.inf)
                m = jnp.maximum(jnp.max(s, axis=-1, keepdims=True), sink)
                p = jnp.exp(s - m)
                denom = jnp.sum(p, axis=-1, keepdims=True) + jnp.exp(sink - m)
                o = jnp.dot((p / denom).astype(BF16), vb, preferred_element_type=F32)
                o_ref[blk * w:(blk + 1) * w, qsl] = o.astype(o_ref.dtype)


def _swa_attention(proj, sinks, cos, sin_signed, batch, seq, q_col, k_col, v_col, q_heads):
    t = _tile(seq, SWA_T, SWA_WINDOW)
    nt = seq // t
    per = t // SWA_WINDOW
    qw = q_heads * HEAD_DIM
    kvw = SWA_KV_HEADS * HEAD_DIM
    assert q_col % qw == 0 and k_col % kvw == 0 and v_col % kvw == 0

    def prev(b, i):
        return jnp.maximum((b * nt + i) * per - 1, 0)

    def prev_pos(i):
        return jnp.maximum(i * per - 1, 0)

    return pl.pallas_call(
        functools.partial(_swa_kernel, scale=HEAD_DIM ** -0.5, group=q_heads // SWA_KV_HEADS),
        grid=(batch, nt),
        in_specs=[
            pl.BlockSpec(memory_space=pltpu.SMEM),
            pl.BlockSpec((t, qw), lambda b, i: (b * nt + i, q_col // qw)),
            pl.BlockSpec((t, kvw), lambda b, i: (b * nt + i, k_col // kvw)),
            pl.BlockSpec((t, kvw), lambda b, i: (b * nt + i, v_col // kvw)),
            pl.BlockSpec((SWA_WINDOW, kvw), lambda b, i: (prev(b, i), k_col // kvw)),
            pl.BlockSpec((SWA_WINDOW, kvw), lambda b, i: (prev(b, i), v_col // kvw)),
            pl.BlockSpec((t, HEAD_DIM), lambda b, i: (i, 0)),
            pl.BlockSpec((t, HEAD_DIM), lambda b, i: (i, 0)),
            pl.BlockSpec((SWA_WINDOW, HEAD_DIM), lambda b, i: (prev_pos(i), 0)),
            pl.BlockSpec((SWA_WINDOW, HEAD_DIM), lambda b, i: (prev_pos(i), 0)),
        ],
        out_specs=pl.BlockSpec((t, qw), lambda b, i: (b * nt + i, 0)),
        out_shape=jax.ShapeDtypeStruct((batch * seq, qw), BF16),
        compiler_params=_params(("arbitrary", "arbitrary")),
        name="swa_attention",
    )(sinks, proj, proj, proj, proj, proj, cos, sin_signed, cos, sin_signed)


def _cummax_rows(x):
    n = x.shape[0]
    row = lax.broadcasted_iota(jnp.int32, x.shape, 0)
    shift = 1
    while shift < n:
        x = jnp.maximum(x, jnp.where(row >= shift, pltpu.roll(x, shift, 0), -jnp.inf))
        shift *= 2
    return x


def _mlstm_kernel(q_ref, k_ref, v_ref, og_ref, g_ref, gt_ref, o_ref, state_ref, m_ref, *, dk, dv):
    heads = MLSTM_HEADS

    @pl.when(pl.program_id(1) == 0)
    def _():
        state_ref[...] = jnp.zeros_like(state_ref)
        m_ref[...] = jnp.zeros_like(m_ref)

    n = q_ref.shape[0]
    g = g_ref[...]
    gt = gt_ref[...]
    row = lax.broadcasted_iota(jnp.int32, (n, n), 0)
    col = lax.broadcasted_iota(jnp.int32, (n, n), 1)
    causal = col <= row
    b_row_all = _cumsum_lanes(_log_sigmoid(gt), (row <= col).astype(BF16))
    b_all = pltpu.roll(_cumsum_rows(causal.astype(BF16), _log_sigmoid(g)), LANES - heads, 1)
    kc_all = g - b_all
    kcmax_all = _cummax_rows(kc_all)
    spread = (lax.broadcasted_iota(jnp.int32, (LANES, heads * LANES), 1) // LANES
              == lax.broadcasted_iota(jnp.int32, (LANES, heads * LANES), 0)).astype(BF16)
    b_wide, kc_wide, kcmax_wide = (
        sum(jnp.dot(term, spread, preferred_element_type=F32) for term in _split3(x))
        for x in (b_all, kc_all, kcmax_all))
    ones = jnp.ones((n, LANES), BF16)
    tiles = n // LANES

    def wide(x, reps):
        return jnp.concatenate([x] * reps, axis=1)

    for h in range(heads):
        hs = slice(h * LANES, (h + 1) * LANES)
        b_col, kc_col, kcmax_col = b_wide[:, hs], kc_wide[:, hs], kcmax_wide[:, hs]
        i_row = gt[h:h + 1, :]
        b_row = b_row_all[heads + h:heads + h + 1, :]
        b_last = b_row[:, n - 1:n]
        m_prev = m_ref[h]

        m_inter = b_col + m_prev
        m_t = jnp.maximum(m_inter, b_col + kcmax_col)
        inter = jnp.exp(m_inter - m_t)
        d = jnp.where(causal, (wide(b_col, tiles) - b_row) + i_row, -jnp.inf)
        wgt = jnp.exp(d - wide(m_t, tiles))

        qh = q_ref[:, h * dk:(h + 1) * dk]
        kf = k_ref[:, h * dk:(h + 1) * dk].astype(F32) * (dk ** -0.5)
        kh = kf.astype(BF16)
        v_ext = jnp.concatenate([v_ref[:, h * dv:(h + 1) * dv], ones], axis=1)
        state = state_ref[h]

        sm = wgt * lax.dot_general(qh, kh, NT_DIMS, preferred_element_type=F32)
        tot = wide(inter, dv // LANES + 1) * jnp.dot(qh, state.astype(BF16), preferred_element_type=F32)
        tot = tot + jnp.dot(sm.astype(BF16), v_ext, preferred_element_type=F32)
        den = jnp.maximum(jnp.abs(tot[:, dv:]), jnp.exp(-m_t))
        hid = tot[:, :dv] * wide(1.0 / den, dv // LANES)
        gate = _sigmoid(og_ref[:, h * dv:(h + 1) * dv].astype(F32))
        o_ref[:, h * dv:(h + 1) * dv] = (gate * hid).astype(o_ref.dtype)

        m_new = jnp.maximum(b_last + m_prev, jnp.max((b_last - b_row) + i_row, axis=-1, keepdims=True))
        decay = jnp.exp(b_last + m_prev - m_new)
        w_end = jnp.exp((b_last + kc_col) - m_new)
        kw = (kf * w_end).astype(BF16)
        state_ref[h] = decay * state + lax.dot_general(kw, v_ext, TN_DIMS, preferred_element_type=F32)
        m_ref[h] = m_new


def _mlstm(proj, g, gt, batch, seq, dk, dv):
    heads = MLSTM_HEADS
    n = _tile(seq, MLSTM_L, LANES)
    nc = seq // n
    qk, vw = heads * dk, heads * dv
    assert vw % qk == 0
    r = vw // qk
    return pl.pallas_call(
        functools.partial(_mlstm_kernel, dk=dk, dv=dv),
        grid=(batch, nc),
        in_specs=[
            pl.BlockSpec((n, qk), lambda b, c: (b * nc + c, 0)),
            pl.BlockSpec((n, qk), lambda b, c: (b * nc + c, 1)),
            pl.BlockSpec((n, vw), lambda b, c: (b * nc + c, 2 // r)),
            pl.BlockSpec((n, vw), lambda b, c: (b * nc + c, 2 // r + 1)),
            pl.BlockSpec((n, LANES), lambda b, c: (b * nc + c, 0)),
            pl.BlockSpec((GATE_ROWS, n), lambda b, c: (0, b * nc + c)),
        ],
        out_specs=pl.BlockSpec((n, vw), lambda b, c: (b * nc + c, 0)),
        out_shape=jax.ShapeDtypeStruct((batch * seq, vw), BF16),
        scratch_shapes=[pltpu.VMEM((heads, dk, dv + LANES), F32), pltpu.VMEM((heads, 1, 1), F32)],
        compiler_params=_params(("arbitrary", "arbitrary")),
        name="mlstm",
    )(proj, proj, proj, proj, g, gt)


def _outproj_ln_kernel(*refs, alpha, n_in):
    h_refs, w_refs = refs[:n_in], refs[n_in:2 * n_in]
    x_ref, g_ref, b_ref, o_ref = refs[2 * n_in:]
    for r in range(o_ref.shape[0] // OUT_ROWS):
        rows = slice(r * OUT_ROWS, (r + 1) * OUT_ROWS)
        o_ref[rows, :] = sum(jnp.dot(h[rows, :], w[...], preferred_element_type=F32) for h, w in zip(h_refs, w_refs))
        o_ref[rows, :] = _layer_norm(alpha * x_ref[rows, :] + o_ref[rows, :], g_ref[...], b_ref[...])


def _out_projection_ln(hs, w, layer, x, ln_g, ln_b, alpha):
    m, d = x.shape
    bm = _tile(m, OUT_BM, LANES)
    widths = [h.shape[1] for h in hs]
    assert all(wd == widths[0] for wd in widths) and w.shape[1] == sum(widths)
    in_specs = [pl.BlockSpec((bm, wd), lambda i: (i, 0)) for wd in widths]
    in_specs += [pl.BlockSpec((None, wd, d), lambda i, k=k: (layer, k, 0)) for k, wd in enumerate(widths)]
    in_specs += [pl.BlockSpec((bm, d), lambda i: (i, 0)),
                 pl.BlockSpec((1, d), lambda i: (0, 0)),
                 pl.BlockSpec((1, d), lambda i: (0, 0))]
    return pl.pallas_call(
        functools.partial(_outproj_ln_kernel, alpha=alpha, n_in=len(hs)),
        grid=(m // bm,),
        in_specs=in_specs,
        out_specs=pl.BlockSpec((bm, d), lambda i: (i, 0)),
        out_shape=jax.ShapeDtypeStruct((m, d), F32),
        compiler_params=_params(("arbitrary",)),
        name="out_projection_ln",
    )(*hs, *([w] * len(hs)), x, ln_g, ln_b)


def _causal_conv(u, prev, p, r0):
    w0, w1, w2, b = p[r0:r0 + 1], p[r0 + 1:r0 + 2], p[r0 + 2:r0 + 3], p[r0 + 3:r0 + 4]
    body = b + w0 * pltpu.roll(u, 2, 0) + w1 * pltpu.roll(u, 1, 0) + w2 * u
    top = jnp.concatenate([prev, u[:HEAD_ROWS, :]], axis=0)
    head = b + w0 * pltpu.roll(top, 2, 0) + w1 * pltpu.roll(top, 1, 0) + w2 * top
    return body, head[SUBLANES:, :]


def _ffn_kernel(x_ref, wg_ref, wv_ref, cp_ref, wd_ref, lg_ref, lb_ref, o_ref,
                xb_ref, h0_ref, h1_ref, pg_ref, pv_ref, ug_ref, uv_ref, *, alpha, tiles_per_seq, nf):
    i = pl.program_id(0)
    j = pl.program_id(1)
    bm = x_ref.shape[0]
    h_refs = (h0_ref, h1_ref)

    def up(h_ref, first_step=False):
        kept_rows = jnp.where(i % tiles_per_seq != 0, SUBLANES, 0)
        keep = lax.broadcasted_iota(jnp.int32, (SUBLANES, FFN_CHUNK), 0) < kept_rows
        chunks = [slice(c * FFN_CHUNK, (c + 1) * FFN_CHUNK) for c in range(h_ref.shape[1] // FFN_CHUNK)]
        prev_g = [jnp.where(keep, pg_ref[j, :, cs], 0.0) for cs in chunks]
        prev_v = [jnp.where(keep, pv_ref[j, :, cs], 0.0) for cs in chunks]
        for r in range(bm // FFN_ROWS):
            rows = slice(r * FFN_ROWS, (r + 1) * FFN_ROWS)
            if first_step:
                xb_ref[rows, :] = x_ref[rows, :].astype(BF16)
            xr = xb_ref[rows, :]
            for c, cs in enumerate(chunks):
                slot = (r * len(chunks) + c) % FFN_SLOTS
                ug_ref[slot] = jnp.dot(xr, wg_ref[:, cs], preferred_element_type=F32)
                uv_ref[slot] = jnp.dot(xr, wv_ref[:, cs], preferred_element_type=F32)
                ug, uv = ug_ref[slot], uv_ref[slot]
                cp = cp_ref[j, :, cs]
                cg, cg_top = _causal_conv(ug, prev_g[c], cp, 0)
                cv, cv_top = _causal_conv(uv, prev_v[c], cp, CONV_WIDTH + 1)
                prev_g[c] = ug[FFN_ROWS - SUBLANES:, :]
                prev_v[c] = uv[FFN_ROWS - SUBLANES:, :]
                h_ref[r * FFN_ROWS:(r + 1) * FFN_ROWS, cs] = (cg * _sigmoid(cg) * cv).astype(BF16)
                h_ref[r * FFN_ROWS:r * FFN_ROWS + HEAD_ROWS, cs] = (cg_top * _sigmoid(cg_top) * cv_top).astype(BF16)
        for c, cs in enumerate(chunks):
            pg_ref[j, :, cs] = prev_g[c]
            pv_ref[j, :, cs] = prev_v[c]

    def down(h_ref):
        o_ref[...] += jnp.dot(h_ref[...], wd_ref[...], preferred_element_type=F32)

    @pl.when(j == 0)
    def _():
        o_ref[...] = jnp.zeros_like(o_ref)
        up(h_refs[0], first_step=True)

    for parity in range(2):
        @pl.when(jnp.logical_and(jnp.logical_and(j > 0, j < nf), j % 2 == parity))
        def _():
            up(h_refs[parity])
            down(h_refs[1 - parity])

    @pl.when(j == nf)
    def _():
        h_ref = h_refs[(nf - 1) % 2]
        for r in range(bm // FFN_ROWS):
            rows = slice(r * FFN_ROWS, (r + 1) * FFN_ROWS)
            o_ref[rows, :] += jnp.dot(h_ref[rows, :], wd_ref[...], preferred_element_type=F32)
            o_ref[rows, :] = _layer_norm(alpha * x_ref[rows, :] + o_ref[rows, :], lg_ref[...], lb_ref[...])


def _conv_ffn_ln(x, w_up, w_down, layer, conv_w, conv_b, ln_g, ln_b, alpha, seq):
    m, d = x.shape
    f = w_down.shape[1]
    bm = _tile(seq, FFN_BM, 2 * SUBLANES)
    bf = _tile(f, FFN_BF, LANES)
    nf = f // bf

    conv_p = jnp.concatenate([conv_w[:, :f], conv_b[:, :f], conv_w[:, f:], conv_b[:, f:]], axis=0)
    conv_p = conv_p.reshape(2 * (CONV_WIDTH + 1), nf, bf).transpose(1, 0, 2)

    def up_blk(j):
        return jnp.minimum(j, nf - 1)

    def down_blk(j):
        return jnp.maximum(j - 1, 0)

    return pl.pallas_call(
        functools.partial(_ffn_kernel, alpha=alpha, tiles_per_seq=seq // bm, nf=nf),
        grid=(m // bm, nf + 1),
        in_specs=[
            pl.BlockSpec((bm, d), lambda i, j: (i, 0)),
            pl.BlockSpec((None, d, bf), lambda i, j: (layer, 0, up_blk(j))),
            pl.BlockSpec((None, d, bf), lambda i, j: (layer, 0, nf + up_blk(j))),
            pl.BlockSpec(conv_p.shape, lambda i, j: (0, 0, 0)),
            pl.BlockSpec((None, bf, d), lambda i, j: (layer, down_blk(j), 0)),
            pl.BlockSpec((1, d), lambda i, j: (0, 0)),
            pl.BlockSpec((1, d), lambda i, j: (0, 0)),
        ],
        out_specs=pl.BlockSpec((bm, d), lambda i, j: (i, 0)),
        out_shape=jax.ShapeDtypeStruct((m, d), F32),
        scratch_shapes=[
            pltpu.VMEM((bm, d), BF16),
            pltpu.VMEM((bm, bf), BF16),
            pltpu.VMEM((bm, bf), BF16),
            pltpu.VMEM((nf, SUBLANES, bf), F32),
            pltpu.VMEM((nf, SUBLANES, bf), F32),
            pltpu.VMEM((FFN_SLOTS, FFN_ROWS, FFN_CHUNK), F32),
            pltpu.VMEM((FFN_SLOTS, FFN_ROWS, FFN_CHUNK), F32),
        ],
        compiler_params=_params(("arbitrary", "arbitrary")),
        name="conv_ffn_ln",
    )(x, w_up, w_up, conv_p, w_down, ln_g, ln_b)


def _gate_params(w, b):
    d, n = w.shape
    wg = jnp.zeros((d, LANES), BF16).at[:, :n].set(w.astype(BF16))
    bg = jnp.zeros((1, LANES), F32).at[0, :n].set(b)
    return wg, bg


def _rope_tables(seq):
    half = HEAD_DIM // 2
    inv_freq = jnp.power(ROPE_THETA, -jnp.arange(half, dtype=F32) * (2.0 / HEAD_DIM))
    ang = jnp.arange(seq, dtype=F32)[:, None] * inv_freq[None, :]
    cos, sin = jnp.cos(ang), jnp.sin(ang)
    return jnp.concatenate([cos, cos], axis=-1), jnp.concatenate([-sin, sin], axis=-1)


def kernel(x, attn_w_in, attn_b_in, attn_sinks, attn_w_out, mlstm_w_in, mlstm_b_in, mlstm_w_out, ffn_w_up,
           ffn_conv_w, ffn_conv_b, ffn_w_down, ln1_g, ln1_b, ln2_g, ln2_b):
    batch, seq, d = x.shape
    depth = ln1_g.shape[0]
    alpha = float((2 * depth) ** 0.25)
    fox_heads = d // (2 * HEAD_DIM)
    swa_heads = d // (2 * HEAD_DIM)
    fox_dim = fox_heads * HEAD_DIM
    fox_f_off = 3 * fox_dim
    dk, dv = d // (2 * MLSTM_HEADS), d // MLSTM_HEADS
    mlstm_main = 2 * MLSTM_HEADS * dk + 2 * MLSTM_HEADS * dv
    cos, sin_signed = _rope_tables(seq)

    gate_lo, gate_hi = fox_f_off, fox_f_off + fox_heads
    attn_w_gate = lax.optimization_barrier(attn_w_in[:, :, gate_lo:gate_hi])
    mlstm_w_gate = lax.optimization_barrier(mlstm_w_in[:, :, mlstm_main:])
    attn_w_main = jnp.concatenate([attn_w_in[:, :, :gate_lo], attn_w_in[:, :, gate_hi:]], axis=2).astype(BF16)
    attn_b_main = jnp.concatenate([attn_b_in[:, :gate_lo], attn_b_in[:, gate_hi:]], axis=1)
    attn_w_out_b = attn_w_out.astype(BF16)
    mlstm_w_main = mlstm_w_in[:, :, :mlstm_main].astype(BF16)
    mlstm_w_out_b = mlstm_w_out.astype(BF16)
    ffn_w_up_b = ffn_w_up.astype(BF16)
    ffn_w_down_b = ffn_w_down.astype(BF16)

    h = x.reshape(batch * seq, d)
    for layer in range(depth):
        j = layer // 2
        if layer % 2 == 0:
            gates = _gate_params(attn_w_gate[j], attn_b_in[j][gate_lo:gate_hi])
            proj, g, gt = _in_projection(h, attn_w_main, j, attn_b_main[j][None, :], *gates)
            fox = _fox_attention(proj, _fox_gate_cumsum(g, batch, seq), batch, seq, fox_heads)
            swa_q = 3 * fox_dim
            swa_k = swa_q + swa_heads * HEAD_DIM
            swa_v = swa_k + SWA_KV_HEADS * HEAD_DIM
            swa = _swa_attention(proj, attn_sinks[j], cos, sin_signed, batch, seq, swa_q, swa_k, swa_v, swa_heads)
            h = _out_projection_ln([fox, swa], attn_w_out_b, j, h, ln1_g[layer][None, :], ln1_b[layer][None, :], alpha)
        else:
            gates = _gate_params(mlstm_w_gate[j], mlstm_b_in[j][mlstm_main:])
            proj, g, gt = _in_projection(h, mlstm_w_main, j, mlstm_b_in[j][None, :mlstm_main], *gates)
            mixed = _mlstm(proj, g, gt, batch, seq, dk, dv)
            h = _out_projection_ln([mixed], mlstm_w_out_b, j, h, ln1_g[layer][None, :], ln1_b[layer][None, :], alpha)
        h = _conv_ffn_ln(h, ffn_w_up_b, ffn_w_down_b, layer, ffn_conv_w[layer], ffn_conv_b[layer][None, :],
                         ln2_g[layer][None, :], ln2_b[layer][None, :], alpha, seq)
    return h.reshape(batch, seq, d)
```

```python
import functools

import jax
import jax.numpy as jnp
from jax import lax
from jax.experimental import pallas as pl
from jax.experimental.pallas import tpu as pltpu

F32 = jnp.float32
BF16 = jnp.bfloat16

HEAD_DIM = 128
SWA_KV_HEADS = 2
SWA_WINDOW = 128
ROPE_THETA = 10000.0
MLSTM_HEADS = 8
CONV_WIDTH = 3
LN_EPS = 1e-5
LOG2_E = 1.4426950408889634

LANES = 128
SUBLANES = 8
GATE_ROWS = 16
HEAD_ROWS = 16
VMEM_LIMIT_BYTES = 60 * 1024 * 1024

PROJ_BM = 1024
PROJ_BN = 1536
OUT_BM = 512
OUT_ROWS = 128
FFN_BM = 1024
FFN_BF = 512
FFN_CHUNK = 256
FFN_ROWS = 256
FFN_SLOTS = 2
FOX_TQ = 1024
FOX_GROUP = 2
SWA_T = 512
CUM_T = 512
MLSTM_L = 256

NT_DIMS = (((1,), (1,)), ((), ()))
TN_DIMS = (((0,), (0,)), ((), ()))


def _tile(n, pref, unit):
    t = min(pref, n)
    while n % t or t % unit:
        t -= unit
    assert t > 0, (n, pref, unit)
    return t


def _params(sem):
    return pltpu.CompilerParams(dimension_semantics=sem, vmem_limit_bytes=VMEM_LIMIT_BYTES)


def _log_sigmoid(x):
    return jnp.minimum(x, 0.0) - jnp.log(1.0 + jnp.exp(-jnp.abs(x)))


def _sigmoid(x):
    return 1.0 / (1.0 + jnp.exp(-x))


def _split3(x):
    h1 = x.astype(BF16)
    r1 = x - h1.astype(F32)
    h2 = r1.astype(BF16)
    h3 = (r1 - h2.astype(F32)).astype(BF16)
    return h1, h2, h3


def _cumsum_rows(tri, x):
    return sum(jnp.dot(tri, h, preferred_element_type=F32) for h in _split3(x))


def _cumsum_lanes(x, tri_t):
    return sum(jnp.dot(h, tri_t, preferred_element_type=F32) for h in _split3(x))


def _layer_norm(z, g, b):
    mu = jnp.mean(z, axis=-1, keepdims=True)
    zc = z - mu
    var = jnp.mean(zc * zc, axis=-1, keepdims=True)
    return zc * lax.rsqrt(var + LN_EPS) * g + b


def _proj_kernel(x_ref, w_ref, b_ref, wg_ref, bg_ref, o_ref, g_ref, gt_ref, xb_ref):
    @pl.when(pl.program_id(1) == 0)
    def _():
        xb = x_ref[...].astype(BF16)
        xb_ref[...] = xb
        g = jnp.dot(xb, wg_ref[...], preferred_element_type=F32) + bg_ref[...]
        g_ref[...] = g
        gt_ref[...] = g.T[:GATE_ROWS, :]

    acc = jnp.dot(xb_ref[...], w_ref[...], preferred_element_type=F32)
    o_ref[...] = (acc + b_ref[...]).astype(o_ref.dtype)


def _in_projection(x, w, layer, b, wg, bg):
    m, d = x.shape
    n = w.shape[2]
    bm = _tile(m, PROJ_BM, LANES)
    bn = _tile(n, PROJ_BN, LANES)
    return pl.pallas_call(
        _proj_kernel,
        grid=(m // bm, n // bn),
        in_specs=[
            pl.BlockSpec((bm, d), lambda i, j: (i, 0)),
            pl.BlockSpec((None, d, bn), lambda i, j: (layer, 0, j)),
            pl.BlockSpec((1, bn), lambda i, j: (0, j)),
            pl.BlockSpec((d, LANES), lambda i, j: (0, 0)),
            pl.BlockSpec((1, LANES), lambda i, j: (0, 0)),
        ],
        out_specs=[
            pl.BlockSpec((bm, bn), lambda i, j: (i, j)),
            pl.BlockSpec((bm, LANES), lambda i, j: (i, 0)),
            pl.BlockSpec((GATE_ROWS, bm), lambda i, j: (0, i)),
        ],
        out_shape=[
            jax.ShapeDtypeStruct((m, n), BF16),
            jax.ShapeDtypeStruct((m, LANES), F32),
            jax.ShapeDtypeStruct((GATE_ROWS, m), F32),
        ],
        scratch_shapes=[pltpu.VMEM((bm, d), BF16)],
        compiler_params=_params(("arbitrary", "arbitrary")),
        name="in_projection",
    )(x, w, b, wg, bg)


def _fox_gate_kernel(g_ref, c_ref, carry_ref):
    @pl.when(pl.program_id(1) == 0)
    def _():
        carry_ref[...] = jnp.zeros_like(carry_ref)

    t = g_ref.shape[0]
    row = lax.broadcasted_iota(jnp.int32, (t, t), 0)
    col = lax.broadcasted_iota(jnp.int32, (t, t), 1)
    c = _cumsum_rows((col <= row).astype(BF16), _log_sigmoid(g_ref[...])) + carry_ref[...]
    c_ref[...] = c
    carry_ref[...] = c[t - 1:t, :]


def _fox_gate_cumsum(g, batch, seq):
    t = _tile(seq, CUM_T, LANES)
    ns = seq // t
    return pl.pallas_call(
        _fox_gate_kernel,
        grid=(batch, ns),
        in_specs=[pl.BlockSpec((t, LANES), lambda b, s: (b * ns + s, 0))],
        out_specs=pl.BlockSpec((t, LANES), lambda b, s: (b * ns + s, 0)),
        out_shape=jax.ShapeDtypeStruct((batch * seq, LANES), F32),
        scratch_shapes=[pltpu.VMEM((1, LANES), F32)],
        compiler_params=_params(("arbitrary", "arbitrary")),
        name="fox_gate_cumsum",
    )(g)


def _bias_lanes(c, ones_first):
    n = c.shape[0]
    h1, h2, h3 = (h.astype(F32) for h in _split3(c))
    lane = lax.broadcasted_iota(jnp.int32, (n, HEAD_DIM), 1)
    lo = 3 if ones_first else 0
    split = jnp.where(lane == lo, h1, jnp.where(lane == lo + 1, h2, jnp.where(lane == lo + 2, h3, 0.0)))
    ones = jnp.logical_and(lane >= 3 - lo, lane < 6 - lo)
    return jnp.where(ones, 1.0, split).astype(BF16)


def _fox_kernel(q_ref, k_ref, v_ref, c_ref, o_ref, kb_ref, m_ref, acc_ref, s_ref, *, scale, group, tq, tk):
    first_head = pl.program_id(1) * group
    per = tq // tk
    assert per == 2
    unit = (lax.broadcasted_iota(jnp.int32, (tk, HEAD_DIM), 1) == 0).astype(BF16)
    slices = [slice(g * HEAD_DIM, (g + 1) * HEAD_DIM) for g in range(group)]

    def head_lane(c, g):
        lane = lax.broadcasted_iota(jnp.int32, c.shape, 1)
        return jnp.sum(jnp.where(lane == first_head + g, c, 0.0), axis=-1, keepdims=True) * LOG2_E

    for g in range(group):
        kb_ref[g] = _bias_lanes(-head_lane(c_ref[...], g), ones_first=False)

    @pl.loop(0, q_ref.shape[0] // tq)
    def _(qi):
        q_rows = pl.ds(pl.multiple_of(qi * tq, tq), tq)
        ct = c_ref[q_rows, :]
        q_ext = []
        for g, sl in enumerate(slices):
            q = (q_ref[q_rows, sl].astype(F32) * (scale * LOG2_E)).astype(BF16)
            q_ext.append(jnp.concatenate([q, _bias_lanes(head_lane(ct, g), ones_first=True)], axis=1))

        def logits(slot, ki):
            k_rows = pl.ds(pl.multiple_of(ki * tk, tk), tk)
            for g, sl in enumerate(slices):
                k_ext = jnp.concatenate([k_ref[k_rows, sl], kb_ref[g, k_rows, :]], axis=1)
                s_ref[slot, g] = lax.dot_general(q_ext[g], k_ext, NT_DIMS, preferred_element_type=F32)

        def absorb(slot, ki, band=None):
            k_rows = pl.ds(pl.multiple_of(ki * tk, tk), tk)
            for g, sl in enumerate(slices):
                s = s_ref[slot, g]
                if band is not None:
                    row = lax.broadcasted_iota(jnp.int32, (tq, tk), 0)
                    col = lax.broadcasted_iota(jnp.int32, (tq, tk), 1)
                    s = jnp.where(col + band * tk <= row, s, -jnp.inf)
                m = m_ref[g]
                m_new = jnp.maximum(m, jnp.max(s, axis=-1, keepdims=True))
                p = jnp.exp2(s - m_new).astype(BF16)
                v_ext = jnp.concatenate([v_ref[k_rows, sl], unit], axis=1)
                acc_ref[g] = jnp.exp2(m - m_new) * acc_ref[g] + jnp.dot(p, v_ext, preferred_element_type=F32)
                m_ref[g] = m_new

        m_ref[...] = jnp.full(m_ref.shape, -jnp.inf, F32)
        acc_ref[...] = jnp.zeros_like(acc_ref)

        logits(0, 0)

        @pl.loop(0, qi)
        def _(pair):
            ki = per * pair
            logits(1, ki + 1)
            absorb(0, ki)
            logits(0, ki + 2)
            absorb(1, ki + 1)

        logits(1, per * qi + 1)
        absorb(0, per * qi, band=0)
        absorb(1, per * qi + 1, band=1)

        for g, sl in enumerate(slices):
            acc = acc_ref[g]
            o_ref[q_rows, sl] = (acc[:, :HEAD_DIM] / acc[:, HEAD_DIM:HEAD_DIM + 1]).astype(o_ref.dtype)


def _fox_attention(proj, c, batch, seq, heads):
    tq = _tile(seq, FOX_TQ, LANES)
    tk = tq // 2
    group = FOX_GROUP
    assert heads % group == 0
    hb = heads // group
    gw = group * HEAD_DIM
    return pl.pallas_call(
        functools.partial(_fox_kernel, scale=HEAD_DIM ** -0.5, group=group, tq=tq, tk=tk),
        grid=(batch, hb),
        in_specs=[
            pl.BlockSpec((seq, gw), lambda b, h: (b, h)),
            pl.BlockSpec((seq, gw), lambda b, h: (b, hb + h)),
            pl.BlockSpec((seq, gw), lambda b, h: (b, 2 * hb + h)),
            pl.BlockSpec((seq, LANES), lambda b, h: (b, 0)),
        ],
        out_specs=pl.BlockSpec((seq, gw), lambda b, h: (b, h)),
        out_shape=jax.ShapeDtypeStruct((batch * seq, heads * HEAD_DIM), BF16),
        scratch_shapes=[pltpu.VMEM((group, seq, HEAD_DIM), BF16),
                        pltpu.VMEM((group, tq, 1), F32),
                        pltpu.VMEM((group, tq, 2 * HEAD_DIM), F32),
                        pltpu.VMEM((2, group, tq, tk), F32)],
        compiler_params=_params(("arbitrary", "arbitrary")),
        name="fox_attention",
    )(proj, proj, proj, c)


def _rope(x, cos, sin_signed):
    xf = x.astype(F32)
    return xf * cos + pltpu.roll(xf, HEAD_DIM // 2, 1) * sin_signed


def _swa_kernel(sink_ref, q_ref, k_ref, v_ref, kp_ref, vp_ref, cos_ref, sin_ref, cosp_ref, sinp_ref, o_ref,
                *, scale, group):
    w = SWA_WINDOW
    first_visible = jnp.where(pl.program_id(1) == 0, w, 0)
    nblk = q_ref.shape[0] // w
    cos, sin = cos_ref[...], sin_ref[...]
    cos_q, sin_q = cos * scale, sin * scale
    cosp, sinp = cosp_ref[...], sinp_ref[...]
    row = lax.broadcasted_iota(jnp.int32, (w, 2 * w), 0)
    col = lax.broadcasted_iota(jnp.int32, (w, 2 * w), 1)
    valid = jnp.logical_and(col > row, col - w <= row)
    valid_first = jnp.logical_and(valid, col >= first_visible)
    for kv in range(SWA_KV_HEADS):
        ksl = slice(kv * HEAD_DIM, (kv + 1) * HEAD_DIM)
        k_all = jnp.concatenate([_rope(kp_ref[:, ksl], cosp, sinp), _rope(k_ref[:, ksl], cos, sin)],
                                axis=0).astype(BF16)
        v_all = jnp.concatenate([vp_ref[:, ksl], v_ref[:, ksl]], axis=0)
        for g in range(group):
            hq = kv * group + g
            qsl = slice(hq * HEAD_DIM, (hq + 1) * HEAD_DIM)
            q = _rope(q_ref[:, qsl], cos_q, sin_q).astype(BF16)
            sink = sink_ref[hq]
            for blk in range(nblk):
                qb = q[blk * w:(blk + 1) * w, :]
                kb = k_all[blk * w:(blk + 2) * w, :]
                vb = v_all[blk * w:(blk + 2) * w, :]
                s = lax.dot_general(qb, kb, NT_DIMS, preferred_element_type=F32)
                s = jnp.where(valid_first if blk == 0 else valid, s, -jnp.inf)
                m = jnp.maximum(jnp.max(s, axis=-1, keepdims=True), sink)
                p = jnp.exp(s - m)
                denom = jnp.sum(p, axis=-1, keepdims=True) + jnp.exp(sink - m)
                o = jnp.dot((p * (1.0 / denom)).astype(BF16), vb, preferred_element_type=F32)
                o_ref[blk * w:(blk + 1) * w, qsl] = o.astype(o_ref.dtype)


def _swa_attention(proj, sinks, cos, sin_signed, batch, seq, q_col, k_col, v_col, q_heads):
    t = _tile(seq, SWA_T, SWA_WINDOW)
    nt = seq // t
    per = t // SWA_WINDOW
    qw = q_heads * HEAD_DIM
    kvw = SWA_KV_HEADS * HEAD_DIM
    assert q_col % qw == 0 and k_col % kvw == 0 and v_col % kvw == 0

    def prev(b, i):
        return jnp.maximum((b * nt + i) * per - 1, 0)

    def prev_pos(i):
        return jnp.maximum(i * per - 1, 0)

    return pl.pallas_call(
        functools.partial(_swa_kernel, scale=HEAD_DIM ** -0.5, group=q_heads // SWA_KV_HEADS),
        grid=(batch, nt),
        in_specs=[
            pl.BlockSpec(memory_space=pltpu.SMEM),
            pl.BlockSpec((t, qw), lambda b, i: (b * nt + i, q_col // qw)),
            pl.BlockSpec((t, kvw), lambda b, i: (b * nt + i, k_col // kvw)),
            pl.BlockSpec((t, kvw), lambda b, i: (b * nt + i, v_col // kvw)),
            pl.BlockSpec((SWA_WINDOW, kvw), lambda b, i: (prev(b, i), k_col // kvw)),
            pl.BlockSpec((SWA_WINDOW, kvw), lambda b, i: (prev(b, i), v_col // kvw)),
            pl.BlockSpec((t, HEAD_DIM), lambda b, i: (i, 0)),
            pl.BlockSpec((t, HEAD_DIM), lambda b, i: (i, 0)),
            pl.BlockSpec((SWA_WINDOW, HEAD_DIM), lambda b, i: (prev_pos(i), 0)),
            pl.BlockSpec((SWA_WINDOW, HEAD_DIM), lambda b, i: (prev_pos(i), 0)),
        ],
        out_specs=pl.BlockSpec((t, qw), lambda b, i: (b * nt + i, 0)),
        out_shape=jax.ShapeDtypeStruct((batch * seq, qw), BF16),
        compiler_params=_params(("arbitrary", "arbitrary")),
        name="swa_attention",
    )(sinks, proj, proj, proj, proj, proj, cos, sin_signed, cos, sin_signed)


def _cummax_rows(x):
    n = x.shape[0]
    row = lax.broadcasted_iota(jnp.int32, x.shape, 0)
    shift = 1
    while shift < n:
        x = jnp.maximum(x, jnp.where(row >= shift, pltpu.roll(x, shift, 0), -jnp.inf))
        shift *= 2
    return x


def _mlstm_kernel(q_ref, k_ref, v_ref, og_ref, g_ref, gt_ref, o_ref, state_ref, m_ref, *, dk, dv):
    heads = MLSTM_HEADS

    @pl.when(pl.program_id(1) == 0)
    def _():
        state_ref[...] = jnp.zeros_like(state_ref)
        m_ref[...] = jnp.zeros_like(m_ref)

    n = q_ref.shape[0]
    g = g_ref[...]
    gt = gt_ref[...]
    row = lax.broadcasted_iota(jnp.int32, (n, n), 0)
    col = lax.broadcasted_iota(jnp.int32, (n, n), 1)
    causal = col <= row
    b_row_all = _cumsum_lanes(_log_sigmoid(gt), (row <= col).astype(BF16))
    b_all = pltpu.roll(_cumsum_rows(causal.astype(BF16), _log_sigmoid(g)), LANES - heads, 1)
    kc_all = g - b_all
    kcmax_all = _cummax_rows(kc_all)
    spread = (lax.broadcasted_iota(jnp.int32, (LANES, heads * LANES), 1) // LANES
              == lax.broadcasted_iota(jnp.int32, (LANES, heads * LANES), 0)).astype(BF16)
    b_wide, kc_wide, kcmax_wide = (
        sum(jnp.dot(term, spread, preferred_element_type=F32) for term in _split3(x))
        for x in (b_all, kc_all, kcmax_all))
    ones = jnp.ones((n, LANES), BF16)
    tiles = n // LANES

    def wide(x, reps):
        return jnp.concatenate([x] * reps, axis=1)

    for h in range(heads):
        hs = slice(h * LANES, (h + 1) * LANES)
        b_col, kc_col, kcmax_col = b_wide[:, hs], kc_wide[:, hs], kcmax_wide[:, hs]
        i_row = gt[h:h + 1, :]
        b_row = b_row_all[heads + h:heads + h + 1, :]
        b_last = b_row[:, n - 1:n]
        m_prev = m_ref[h]

        m_inter = b_col + m_prev
        m_t = jnp.maximum(m_inter, b_col + kcmax_col)
        inter = jnp.exp(m_inter - m_t)
        d = jnp.where(causal, (wide(b_col, tiles) - b_row) + i_row, -jnp.inf)
        wgt = jnp.exp(d - wide(m_t, tiles))

        qh = q_ref[:, h * dk:(h + 1) * dk]
        kf = k_ref[:, h * dk:(h + 1) * dk].astype(F32) * (dk ** -0.5)
        kh = kf.astype(BF16)
        v_ext = jnp.concatenate([v_ref[:, h * dv:(h + 1) * dv], ones], axis=1)
        state = state_ref[h]

        sm = wgt * lax.dot_general(qh, kh, NT_DIMS, preferred_element_type=F32)
        tot = wide(inter, dv // LANES + 1) * jnp.dot(qh, state.astype(BF16), preferred_element_type=F32)
        tot = tot + jnp.dot(sm.astype(BF16), v_ext, preferred_element_type=F32)
        den = jnp.maximum(jnp.abs(tot[:, dv:]), jnp.exp(-m_t))
        hid = tot[:, :dv] * wide(1.0 / den, dv // LANES)
        gate = _sigmoid(og_ref[:, h * dv:(h + 1) * dv].astype(F32))
        o_ref[:, h * dv:(h + 1) * dv] = (gate * hid).astype(o_ref.dtype)

        m_new = jnp.maximum(b_last + m_prev, jnp.max((b_last - b_row) + i_row, axis=-1, keepdims=True))
        decay = jnp.exp(b_last + m_prev - m_new)
        w_end = jnp.exp((b_last + kc_col) - m_new)
        kw = (kf * w_end).astype(BF16)
        state_ref[h] = decay * state + lax.dot_general(kw, v_ext, TN_DIMS, preferred_element_type=F32)
        m_ref[h] = m_new


def _mlstm(proj, g, gt, batch, seq, dk, dv):
    heads = MLSTM_HEADS
    n = _tile(seq, MLSTM_L, LANES)
    nc = seq // n
    qk, vw = heads * dk, heads * dv
    assert vw % qk == 0
    r = vw // qk
    return pl.pallas_call(
        functools.partial(_mlstm_kernel, dk=dk, dv=dv),
        grid=(batch, nc),
        in_specs=[
            pl.BlockSpec((n, qk), lambda b, c: (b * nc + c, 0)),
            pl.BlockSpec((n, qk), lambda b, c: (b * nc + c, 1)),
            pl.BlockSpec((n, vw), lambda b, c: (b * nc + c, 2 // r)),
            pl.BlockSpec((n, vw), lambda b, c: (b * nc + c, 2 // r + 1)),
            pl.BlockSpec((n, LANES), lambda b, c: (b * nc + c, 0)),
            pl.BlockSpec((GATE_ROWS, n), lambda b, c: (0, b * nc + c)),
        ],
        out_specs=pl.BlockSpec((n, vw), lambda b, c: (b * nc + c, 0)),
        out_shape=jax.ShapeDtypeStruct((batch * seq, vw), BF16),
        scratch_shapes=[pltpu.VMEM((heads, dk, dv + LANES), F32), pltpu.VMEM((heads, 1, 1), F32)],
        compiler_params=_params(("arbitrary", "arbitrary")),
        name="mlstm",
    )(proj, proj, proj, proj, g, gt)


def _outproj_ln_kernel(*refs, alpha, n_in):
    h_refs, w_refs = refs[:n_in], refs[n_in:2 * n_in]
    x_ref, g_ref, b_ref, o_ref = refs[2 * n_in:]
    for r in range(o_ref.shape[0] // OUT_ROWS):
        rows = slice(r * OUT_ROWS, (r + 1) * OUT_ROWS)
        o_ref[rows, :] = sum(jnp.dot(h[rows, :], w[...], preferred_element_type=F32) for h, w in zip(h_refs, w_refs))
        o_ref[rows, :] = _layer_norm(alpha * x_ref[rows, :] + o_ref[rows, :], g_ref[...], b_ref[...])


def _out_projection_ln(hs, w, layer, x, ln_g, ln_b, alpha):
    m, d = x.shape
    bm = _tile(m, OUT_BM, LANES)
    widths = [h.shape[1] for h in hs]
    assert all(wd == widths[0] for wd in widths) and w.shape[1] == sum(widths)
    in_specs = [pl.BlockSpec((bm, wd), lambda i: (i, 0)) for wd in widths]
    in_specs += [pl.BlockSpec((None, wd, d), lambda i, k=k: (layer, k, 0)) for k, wd in enumerate(widths)]
    in_specs += [pl.BlockSpec((bm, d), lambda i: (i, 0)),
                 pl.BlockSpec((1, d), lambda i: (0, 0)),
                 pl.BlockSpec((1, d), lambda i: (0, 0))]
    return pl.pallas_call(
        functools.partial(_outproj_ln_kernel, alpha=alpha, n_in=len(hs)),
        grid=(m // bm,),
        in_specs=in_specs,
        out_specs=pl.BlockSpec((bm, d), lambda i: (i, 0)),
        out_shape=jax.ShapeDtypeStruct((m, d), F32),
        compiler_params=_params(("arbitrary",)),
        name="out_projection_ln",
    )(*hs, *([w] * len(hs)), x, ln_g, ln_b)


def _causal_conv(u, prev, p, r0):
    w0, w1, w2, b = p[r0:r0 + 1], p[r0 + 1:r0 + 2], p[r0 + 2:r0 + 3], p[r0 + 3:r0 + 4]
    body = b + w0 * pltpu.roll(u, 2, 0) + w1 * pltpu.roll(u, 1, 0) + w2 * u
    top = jnp.concatenate([prev, u[:HEAD_ROWS, :]], axis=0)
    head = b + w0 * pltpu.roll(top, 2, 0) + w1 * pltpu.roll(top, 1, 0) + w2 * top
    return body, head[SUBLANES:, :]


def _ffn_kernel(x_ref, wg_ref, wv_ref, cp_ref, wd_ref, lg_ref, lb_ref, o_ref,
                xb_ref, h0_ref, h1_ref, pg_ref, pv_ref, ug_ref, uv_ref, *, alpha, tiles_per_seq, nf):
    i = pl.program_id(0)
    j = pl.program_id(1)
    bm = x_ref.shape[0]
    h_refs = (h0_ref, h1_ref)

    def up(h_ref, first_step=False):
        kept_rows = jnp.where(i % tiles_per_seq != 0, SUBLANES, 0)
        keep = lax.broadcasted_iota(jnp.int32, (SUBLANES, FFN_CHUNK), 0) < kept_rows
        chunks = [slice(c * FFN_CHUNK, (c + 1) * FFN_CHUNK) for c in range(h_ref.shape[1] // FFN_CHUNK)]
        prev_g = [jnp.where(keep, pg_ref[j, :, cs], 0.0) for cs in chunks]
        prev_v = [jnp.where(keep, pv_ref[j, :, cs], 0.0) for cs in chunks]
        for r in range(bm // FFN_ROWS):
            rows = slice(r * FFN_ROWS, (r + 1) * FFN_ROWS)
            if first_step:
                xb_ref[rows, :] = x_ref[rows, :].astype(BF16)
            xr = xb_ref[rows, :]
            for c, cs in enumerate(chunks):
                slot = (r * len(chunks) + c) % FFN_SLOTS
                ug_ref[slot] = jnp.dot(xr, wg_ref[:, cs], preferred_element_type=F32)
                uv_ref[slot] = jnp.dot(xr, wv_ref[:, cs], preferred_element_type=F32)
                ug, uv = ug_ref[slot], uv_ref[slot]
                cp = cp_ref[j, :, cs]
                cg, cg_top = _causal_conv(ug, prev_g[c], cp, 0)
                cv, cv_top = _causal_conv(uv, prev_v[c], cp, CONV_WIDTH + 1)
                prev_g[c] = ug[FFN_ROWS - SUBLANES:, :]
                prev_v[c] = uv[FFN_ROWS - SUBLANES:, :]
                h_ref[r * FFN_ROWS:(r + 1) * FFN_ROWS, cs] = (cg * _sigmoid(cg) * cv).astype(BF16)
                h_ref[r * FFN_ROWS:r * FFN_ROWS + HEAD_ROWS, cs] = (cg_top * _sigmoid(cg_top) * cv_top).astype(BF16)
        for c, cs in enumerate(chunks):
            pg_ref[j, :, cs] = prev_g[c]
            pv_ref[j, :, cs] = prev_v[c]

    def down(h_ref):
        o_ref[...] += jnp.dot(h_ref[...], wd_ref[...], preferred_element_type=F32)

    @pl.when(j == 0)
    def _():
        o_ref[...] = jnp.zeros_like(o_ref)
        up(h_refs[0], first_step=True)

    for parity in range(2):
        @pl.when(jnp.logical_and(jnp.logical_and(j > 0, j < nf), j % 2 == parity))
        def _():
            up(h_refs[parity])
            down(h_refs[1 - parity])

    @pl.when(j == nf)
    def _():
        h_ref = h_refs[(nf - 1) % 2]
        for r in range(bm // FFN_ROWS):
            rows = slice(r * FFN_ROWS, (r + 1) * FFN_ROWS)
            o_ref[rows, :] += jnp.dot(h_ref[rows, :], wd_ref[...], preferred_element_type=F32)
            o_ref[rows, :] = _layer_norm(alpha * x_ref[rows, :] + o_ref[rows, :], lg_ref[...], lb_ref[...])


def _conv_ffn_ln(x, w_up, w_down, layer, conv_w, conv_b, ln_g, ln_b, alpha, seq):
    m, d = x.shape
    f = w_down.shape[1]
    bm = _tile(seq, FFN_BM, 2 * SUBLANES)
    bf = _tile(f, FFN_BF, LANES)
    nf = f // bf

    conv_p = jnp.concatenate([conv_w[:, :f], conv_b[:, :f], conv_w[:, f:], conv_b[:, f:]], axis=0)
    conv_p = conv_p.reshape(2 * (CONV_WIDTH + 1), nf, bf).transpose(1, 0, 2)

    def up_blk(j):
        return jnp.minimum(j, nf - 1)

    def down_blk(j):
        return jnp.maximum(j - 1, 0)

    return pl.pallas_call(
        functools.partial(_ffn_kernel, alpha=alpha, tiles_per_seq=seq // bm, nf=nf),
        grid=(m // bm, nf + 1),
        in_specs=[
            pl.BlockSpec((bm, d), lambda i, j: (i, 0)),
            pl.BlockSpec((None, d, bf), lambda i, j: (layer, 0, up_blk(j))),
            pl.BlockSpec((None, d, bf), lambda i, j: (layer, 0, nf + up_blk(j))),
            pl.BlockSpec(conv_p.shape, lambda i, j: (0, 0, 0)),
            pl.BlockSpec((None, bf, d), lambda i, j: (layer, down_blk(j), 0)),
            pl.BlockSpec((1, d), lambda i, j: (0, 0)),
            pl.BlockSpec((1, d), lambda i, j: (0, 0)),
        ],
        out_specs=pl.BlockSpec((bm, d), lambda i, j: (i, 0)),
        out_shape=jax.ShapeDtypeStruct((m, d), F32),
        scratch_shapes=[
            pltpu.VMEM((bm, d), BF16),
            pltpu.VMEM((bm, bf), BF16),
            pltpu.VMEM((bm, bf), BF16),
            pltpu.VMEM((nf, SUBLANES, bf), F32),
            pltpu.VMEM((nf, SUBLANES, bf), F32),
            pltpu.VMEM((FFN_SLOTS, FFN_ROWS, FFN_CHUNK), F32),
            pltpu.VMEM((FFN_SLOTS, FFN_ROWS, FFN_CHUNK), F32),
        ],
        compiler_params=_params(("arbitrary", "arbitrary")),
        name="conv_ffn_ln",
    )(x, w_up, w_up, conv_p, w_down, ln_g, ln_b)


def _gate_params(w, b):
    d, n = w.shape
    wg = jnp.zeros((d, LANES), BF16).at[:, :n].set(w.astype(BF16))
    bg = jnp.zeros((1, LANES), F32).at[0, :n].set(b)
    return wg, bg


def _rope_tables(seq):
    half = HEAD_DIM // 2
    inv_freq = jnp.power(ROPE_THETA, -jnp.arange(half, dtype=F32) * (2.0 / HEAD_DIM))
    ang = jnp.arange(seq, dtype=F32)[:, None] * inv_freq[None, :]
    cos, sin = jnp.cos(ang), jnp.sin(ang)
    return jnp.concatenate([cos, cos], axis=-1), jnp.concatenate([-sin, sin], axis=-1)


def kernel(x, attn_w_in, attn_b_in, attn_sinks, attn_w_out, mlstm_w_in, mlstm_b_in, mlstm_w_out, ffn_w_up,
           ffn_conv_w, ffn_conv_b, ffn_w_down, ln1_g, ln1_b, ln2_g, ln2_b):
    batch, seq, d = x.shape
    depth = ln1_g.shape[0]
    alpha = float((2 * depth) ** 0.25)
    fox_heads = d // (2 * HEAD_DIM)
    swa_heads = d // (2 * HEAD_DIM)
    fox_dim = fox_heads * HEAD_DIM
    fox_f_off = 3 * fox_dim
    dk, dv = d // (2 * MLSTM_HEADS), d // MLSTM_HEADS
    mlstm_main = 2 * MLSTM_HEADS * dk + 2 * MLSTM_HEADS * dv
    cos, sin_signed = _rope_tables(seq)

    gate_lo, gate_hi = fox_f_off, fox_f_off + fox_heads
    attn_w_gate = lax.optimization_barrier(attn_w_in[:, :, gate_lo:gate_hi])
    mlstm_w_gate = lax.optimization_barrier(mlstm_w_in[:, :, mlstm_main:])
    attn_w_main = jnp.concatenate([attn_w_in[:, :, :gate_lo], attn_w_in[:, :, gate_hi:]], axis=2).astype(BF16)
    attn_b_main = jnp.concatenate([attn_b_in[:, :gate_lo], attn_b_in[:, gate_hi:]], axis=1)
    attn_w_out_b = attn_w_out.astype(BF16)
    mlstm_w_main = mlstm_w_in[:, :, :mlstm_main].astype(BF16)
    mlstm_w_out_b = mlstm_w_out.astype(BF16)
    ffn_w_up_b = ffn_w_up.astype(BF16)
    ffn_w_down_b = ffn_w_down.astype(BF16)

    h = x.reshape(batch * seq, d)
    for layer in range(depth):
        j = layer // 2
        if layer % 2 == 0:
            gates = _gate_params(attn_w_gate[j], attn_b_in[j][gate_lo:gate_hi])
            proj, g, gt = _in_projection(h, attn_w_main, j, attn_b_main[j][None, :], *gates)
            fox = _fox_attention(proj, _fox_gate_cumsum(g, batch, seq), batch, seq, fox_heads)
            swa_q = 3 * fox_dim
            swa_k = swa_q + swa_heads * HEAD_DIM
            swa_v = swa_k + SWA_KV_HEADS * HEAD_DIM
            swa = _swa_attention(proj, attn_sinks[j], cos, sin_signed, batch, seq, swa_q, swa_k, swa_v, swa_heads)
            h = _out_projection_ln([fox, swa], attn_w_out_b, j, h, ln1_g[layer][None, :], ln1_b[layer][None, :], alpha)
        else:
            gates = _gate_params(mlstm_w_gate[j], mlstm_b_in[j][mlstm_main:])
            proj, g, gt = _in_projection(h, mlstm_w_main, j, mlstm_b_in[j][None, :mlstm_main], *gates)
            mixed = _mlstm(proj, g, gt, batch, seq, dk, dv)
            h = _out_projection_ln([mixed], mlstm_w_out_b, j, h, ln1_g[layer][None, :], ln1_b[layer][None, :], alpha)
        h = _conv_ffn_ln(h, ffn_w_up_b, ffn_w_down_b, layer, ffn_conv_w[layer], ffn_conv_b[layer][None, :],
                         ln2_g[layer][None, :], ln2_b[layer][None, :], alpha, seq)
    return h.reshape(batch, seq, d)
```

```python
import functools

import jax
import jax.numpy as jnp
from jax import lax
from jax.experimental import pallas as pl
from jax.experimental.pallas import tpu as pltpu

F32 = jnp.float32
BF16 = jnp.bfloat16

HEAD_DIM = 128
SWA_KV_HEADS = 2
SWA_WINDOW = 128
ROPE_THETA = 10000.0
MLSTM_HEADS = 8
CONV_WIDTH = 3
LN_EPS = 1e-5
LOG2_E = 1.4426950408889634

LANES = 128
SUBLANES = 8
GATE_ROWS = 16
HEAD_ROWS = 16
VMEM_LIMIT_BYTES = 60 * 1024 * 1024

PROJ_BM = 1024
PROJ_BN = 1536
OUT_BM = 512
OUT_ROWS = 128
FFN_BM = 1024
FFN_BF = 512
FFN_CHUNK = 256
FFN_ROWS = 256
FFN_SLOTS = 2
FOX_TQ = 1024
FOX_GROUP = 2
SWA_T = 512
CUM_T = 512
REGROUP_ROWS = 256
MLSTM_L = 256

NT_DIMS = (((1,), (1,)), ((), ()))
TN_DIMS = (((0,), (0,)), ((), ()))


def _tile(n, pref, unit):
    t = min(pref, n)
    while n % t or t % unit:
        t -= unit
    assert t > 0, (n, pref, unit)
    return t


def _params(sem):
    return pltpu.CompilerParams(dimension_semantics=sem, vmem_limit_bytes=VMEM_LIMIT_BYTES)


def _log_sigmoid(x):
    return jnp.minimum(x, 0.0) - jnp.log(1.0 + jnp.exp(-jnp.abs(x)))


def _sigmoid(x):
    return 1.0 / (1.0 + jnp.exp(-x))


def _split3(x):
    h1 = x.astype(BF16)
    r1 = x - h1.astype(F32)
    h2 = r1.astype(BF16)
    h3 = (r1 - h2.astype(F32)).astype(BF16)
    return h1, h2, h3


def _cumsum_rows(tri, x):
    return sum(jnp.dot(tri, h, preferred_element_type=F32) for h in _split3(x))


def _cumsum_lanes(x, tri_t):
    return sum(jnp.dot(h, tri_t, preferred_element_type=F32) for h in _split3(x))


def _layer_norm(z, g, b):
    mu = jnp.mean(z, axis=-1, keepdims=True)
    zc = z - mu
    var = jnp.mean(zc * zc, axis=-1, keepdims=True)
    return zc * lax.rsqrt(var + LN_EPS) * g + b


def _cast_plan(sources, grid):
    nb = grid[1]
    steps = grid[0] * nb
    strides, in_specs, out_specs, out_shapes, operands = [], [], [], [], []
    for arr, layer in sources:
        _, rows, cols = arr.shape
        chunks = steps
        while steps % chunks or rows % chunks or (rows // chunks) % (2 * SUBLANES):
            chunks -= 1
        stride, r = steps // chunks, rows // chunks
        strides.append(stride)
        in_specs.append(pl.BlockSpec((None, r, cols), lambda a, b, stride=stride, layer=layer:
                                     (layer, (a * nb + b) // stride, 0)))
        out_specs.append(pl.BlockSpec((r, cols), lambda a, b, stride=stride: ((a * nb + b) // stride, 0)))
        out_shapes.append(jax.ShapeDtypeStruct((rows, cols), BF16))
        operands.append(arr)
    return tuple(strides), in_specs, out_specs, out_shapes, operands


def _cast_step(src_refs, dst_refs, strides):
    step = pl.program_id(0) * pl.num_programs(1) + pl.program_id(1)
    for src, dst, stride in zip(src_refs, dst_refs, strides):
        @pl.when(step % stride == 0)
        def _(src=src, dst=dst):
            dst[...] = src[...].astype(BF16)


def _regroup_kernel(head_ref, tail_ref, last_ref, o_ref, *, shift):
    nh = head_ref.shape[1]
    o_ref[:, :nh] = head_ref[...].astype(BF16)
    lane = lax.broadcasted_iota(jnp.int32, (o_ref.shape[0], LANES), 1)
    tiles = tail_ref.shape[1] // LANES
    rolled = [pltpu.roll(tail_ref[:, k * LANES:(k + 1) * LANES], LANES - shift, 1) for k in range(tiles)]
    rolled.append(pltpu.roll(last_ref[...], LANES - shift, 1))
    for k in range(tiles):
        tile = jnp.where(lane < LANES - shift, rolled[k], rolled[k + 1])
        o_ref[:, nh + k * LANES:nh + (k + 1) * LANES] = tile.astype(BF16)


def _regroup_weights(w, lo, hi):
    layers, d, n = w.shape
    shift, tail = hi - lo, n - hi
    assert lo % LANES == 0 and 0 < shift < LANES and tail % LANES == 0 and lo % tail == 0
    rows = _tile(d, REGROUP_ROWS, 2 * SUBLANES)
    return pl.pallas_call(
        functools.partial(_regroup_kernel, shift=shift),
        grid=(layers, d // rows),
        in_specs=[
            pl.BlockSpec((None, rows, lo), lambda l, i: (l, i, 0)),
            pl.BlockSpec((None, rows, tail), lambda l, i: (l, i, lo // tail)),
            pl.BlockSpec((None, rows, LANES), lambda l, i: (l, i, (lo + tail) // LANES)),
        ],
        out_specs=pl.BlockSpec((None, rows, lo + tail), lambda l, i: (l, i, 0)),
        out_shape=jax.ShapeDtypeStruct((layers, d, lo + tail), BF16),
        compiler_params=_params(("arbitrary", "arbitrary")),
        name="regroup_weights",
    )(w, w, w)


def _proj_kernel(x_ref, w_ref, b_ref, wg_ref, bg_ref, o_ref, g_ref, gt_ref, xb_ref):
    @pl.when(pl.program_id(1) == 0)
    def _():
        xb = x_ref[...].astype(BF16)
        xb_ref[...] = xb
        g = jnp.dot(xb, wg_ref[...], preferred_element_type=F32) + bg_ref[...]
        g_ref[...] = g
        gt_ref[...] = g.T[:GATE_ROWS, :]

    acc = jnp.dot(xb_ref[...], w_ref[...], preferred_element_type=F32)
    o_ref[...] = (acc + b_ref[...]).astype(o_ref.dtype)


def _in_projection(x, w, layer, b, wg, bg):
    m, d = x.shape
    n = w.shape[2]
    bm = _tile(m, PROJ_BM, LANES)
    bn = _tile(n, PROJ_BN, LANES)
    return pl.pallas_call(
        _proj_kernel,
        grid=(m // bm, n // bn),
        in_specs=[
            pl.BlockSpec((bm, d), lambda i, j: (i, 0)),
            pl.BlockSpec((None, d, bn), lambda i, j: (layer, 0, j)),
            pl.BlockSpec((1, bn), lambda i, j: (0, j)),
            pl.BlockSpec((d, LANES), lambda i, j: (0, 0)),
            pl.BlockSpec((1, LANES), lambda i, j: (0, 0)),
        ],
        out_specs=[
            pl.BlockSpec((bm, bn), lambda i, j: (i, j)),
            pl.BlockSpec((bm, LANES), lambda i, j: (i, 0)),
            pl.BlockSpec((GATE_ROWS, bm), lambda i, j: (0, i)),
        ],
        out_shape=[
            jax.ShapeDtypeStruct((m, n), BF16),
            jax.ShapeDtypeStruct((m, LANES), F32),
            jax.ShapeDtypeStruct((GATE_ROWS, m), F32),
        ],
        scratch_shapes=[pltpu.VMEM((bm, d), BF16)],
        compiler_params=_params(("arbitrary", "arbitrary")),
        name="in_projection",
    )(x, w, b, wg, bg)


def _fox_gate_kernel(g_ref, c_ref, carry_ref):
    @pl.when(pl.program_id(1) == 0)
    def _():
        carry_ref[...] = jnp.zeros_like(carry_ref)

    t = g_ref.shape[0]
    row = lax.broadcasted_iota(jnp.int32, (t, t), 0)
    col = lax.broadcasted_iota(jnp.int32, (t, t), 1)
    c = _cumsum_rows((col <= row).astype(BF16), _log_sigmoid(g_ref[...])) + carry_ref[...]
    c_ref[...] = c
    carry_ref[...] = c[t - 1:t, :]


def _fox_gate_cumsum(g, batch, seq):
    t = _tile(seq, CUM_T, LANES)
    ns = seq // t
    return pl.pallas_call(
        _fox_gate_kernel,
        grid=(batch, ns),
        in_specs=[pl.BlockSpec((t, LANES), lambda b, s: (b * ns + s, 0))],
        out_specs=pl.BlockSpec((t, LANES), lambda b, s: (b * ns + s, 0)),
        out_shape=jax.ShapeDtypeStruct((batch * seq, LANES), F32),
        scratch_shapes=[pltpu.VMEM((1, LANES), F32)],
        compiler_params=_params(("arbitrary", "arbitrary")),
        name="fox_gate_cumsum",
    )(g)


def _bias_lanes(c, ones_first):
    n = c.shape[0]
    h1, h2, h3 = (h.astype(F32) for h in _split3(c))
    lane = lax.broadcasted_iota(jnp.int32, (n, HEAD_DIM), 1)
    lo = 3 if ones_first else 0
    split = jnp.where(lane == lo, h1, jnp.where(lane == lo + 1, h2, jnp.where(lane == lo + 2, h3, 0.0)))
    ones = jnp.logical_and(lane >= 3 - lo, lane < 6 - lo)
    return jnp.where(ones, 1.0, split).astype(BF16)


def _fox_kernel(q_ref, k_ref, v_ref, c_ref, o_ref, kb_ref, m_ref, acc_ref, s_ref, *, scale, group, tq, tk):
    first_head = pl.program_id(1) * group
    per = tq // tk
    assert per == 2
    unit = (lax.broadcasted_iota(jnp.int32, (tk, HEAD_DIM), 1) == 0).astype(BF16)
    slices = [slice(g * HEAD_DIM, (g + 1) * HEAD_DIM) for g in range(group)]

    def head_lane(c, g):
        lane = lax.broadcasted_iota(jnp.int32, c.shape, 1)
        return jnp.sum(jnp.where(lane == first_head + g, c, 0.0), axis=-1, keepdims=True) * LOG2_E

    for g in range(group):
        kb_ref[g] = _bias_lanes(-head_lane(c_ref[...], g), ones_first=False)

    @pl.loop(0, q_ref.shape[0] // tq)
    def _(qi):
        q_rows = pl.ds(pl.multiple_of(qi * tq, tq), tq)
        ct = c_ref[q_rows, :]
        q_ext = []
        for g, sl in enumerate(slices):
            q = (q_ref[q_rows, sl].astype(F32) * (scale * LOG2_E)).astype(BF16)
            q_ext.append(jnp.concatenate([q, _bias_lanes(head_lane(ct, g), ones_first=True)], axis=1))

        def logits(slot, ki):
            k_rows = pl.ds(pl.multiple_of(ki * tk, tk), tk)
            for g, sl in enumerate(slices):
                k_ext = jnp.concatenate([k_ref[k_rows, sl], kb_ref[g, k_rows, :]], axis=1)
                s_ref[slot, g] = lax.dot_general(q_ext[g], k_ext, NT_DIMS, preferred_element_type=F32)

        def absorb(slot, ki, band=None):
            k_rows = pl.ds(pl.multiple_of(ki * tk, tk), tk)
            for g, sl in enumerate(slices):
                s = s_ref[slot, g]
                if band is not None:
                    row = lax.broadcasted_iota(jnp.int32, (tq, tk), 0)
                    col = lax.broadcasted_iota(jnp.int32, (tq, tk), 1)
                    s = jnp.where(col + band * tk <= row, s, -jnp.inf)
                m = m_ref[g]
                m_new = jnp.maximum(m, jnp.max(s, axis=-1, keepdims=True))
                p = jnp.exp2(s - m_new).astype(BF16)
                v_ext = jnp.concatenate([v_ref[k_rows, sl], unit], axis=1)
                acc_ref[g] = jnp.exp2(m - m_new) * acc_ref[g] + jnp.dot(p, v_ext, preferred_element_type=F32)
                m_ref[g] = m_new

        m_ref[...] = jnp.full(m_ref.shape, -jnp.inf, F32)
        acc_ref[...] = jnp.zeros_like(acc_ref)

        logits(0, 0)

        @pl.loop(0, qi)
        def _(pair):
            ki = per * pair
            logits(1, ki + 1)
            absorb(0, ki)
            logits(0, ki + 2)
            absorb(1, ki + 1)

        logits(1, per * qi + 1)
        absorb(0, per * qi, band=0)
        absorb(1, per * qi + 1, band=1)

        for g, sl in enumerate(slices):
            acc = acc_ref[g]
            o_ref[q_rows, sl] = (acc[:, :HEAD_DIM] / acc[:, HEAD_DIM:HEAD_DIM + 1]).astype(o_ref.dtype)


def _fox_attention(proj, c, batch, seq, heads):
    tq = _tile(seq, FOX_TQ, LANES)
    tk = tq // 2
    group = FOX_GROUP
    assert heads % group == 0
    hb = heads // group
    gw = group * HEAD_DIM
    return pl.pallas_call(
        functools.partial(_fox_kernel, scale=HEAD_DIM ** -0.5, group=group, tq=tq, tk=tk),
        grid=(batch, hb),
        in_specs=[
            pl.BlockSpec((seq, gw), lambda b, h: (b, h)),
            pl.BlockSpec((seq, gw), lambda b, h: (b, hb + h)),
            pl.BlockSpec((seq, gw), lambda b, h: (b, 2 * hb + h)),
            pl.BlockSpec((seq, LANES), lambda b, h: (b, 0)),
        ],
        out_specs=pl.BlockSpec((seq, gw), lambda b, h: (b, h)),
        out_shape=jax.ShapeDtypeStruct((batch * seq, heads * HEAD_DIM), BF16),
        scratch_shapes=[pltpu.VMEM((group, seq, HEAD_DIM), BF16),
                        pltpu.VMEM((group, tq, 1), F32),
                        pltpu.VMEM((group, tq, 2 * HEAD_DIM), F32),
                        pltpu.VMEM((2, group, tq, tk), F32)],
        compiler_params=_params(("arbitrary", "arbitrary")),
        name="fox_attention",
    )(proj, proj, proj, c)


def _rope(x, cos, sin_signed):
    xf = x.astype(F32)
    return xf * cos + pltpu.roll(xf, HEAD_DIM // 2, 1) * sin_signed


def _swa_kernel(sink_ref, q_ref, k_ref, v_ref, kp_ref, vp_ref, cos_ref, sin_ref, cosp_ref, sinp_ref, *rest,
                scale, group, cast_strides):
    nc = len(cast_strides)
    o_ref = rest[nc]
    _cast_step(rest[:nc], rest[nc + 1:], cast_strides)
    w = SWA_WINDOW
    first_visible = jnp.where(pl.program_id(1) == 0, w, 0)
    nblk = q_ref.shape[0] // w
    cos, sin = cos_ref[...], sin_ref[...]
    cos_q, sin_q = cos * scale, sin * scale
    cosp, sinp = cosp_ref[...], sinp_ref[...]
    row = lax.broadcasted_iota(jnp.int32, (w, 2 * w), 0)
    col = lax.broadcasted_iota(jnp.int32, (w, 2 * w), 1)
    valid = jnp.logical_and(col > row, col - w <= row)
    valid_first = jnp.logical_and(valid, col >= first_visible)
    for kv in range(SWA_KV_HEADS):
        ksl = slice(kv * HEAD_DIM, (kv + 1) * HEAD_DIM)
        k_all = jnp.concatenate([_rope(kp_ref[:, ksl], cosp, sinp), _rope(k_ref[:, ksl], cos, sin)],
                                axis=0).astype(BF16)
        v_all = jnp.concatenate([vp_ref[:, ksl], v_ref[:, ksl]], axis=0)
        for g in range(group):
            hq = kv * group + g
            qsl = slice(hq * HEAD_DIM, (hq + 1) * HEAD_DIM)
            q = _rope(q_ref[:, qsl], cos_q, sin_q).astype(BF16)
            sink = sink_ref[hq]
            for blk in range(nblk):
                qb = q[blk * w:(blk + 1) * w, :]
                kb = k_all[blk * w:(blk + 2) * w, :]
                vb = v_all[blk * w:(blk + 2) * w, :]
                s = lax.dot_general(qb, kb, NT_DIMS, preferred_element_type=F32)
                s = jnp.where(valid_first if blk == 0 else valid, s, -jnp.inf)
                m = jnp.maximum(jnp.max(s, axis=-1, keepdims=True), sink)
                p = jnp.exp(s - m)
                denom = jnp.sum(p, axis=-1, keepdims=True) + jnp.exp(sink - m)
                o = jnp.dot((p * (1.0 / denom)).astype(BF16), vb, preferred_element_type=F32)
                o_ref[blk * w:(blk + 1) * w, qsl] = o.astype(o_ref.dtype)


def _swa_attention(proj, sinks, cos, sin_signed, batch, seq, q_col, k_col, v_col, q_heads, cast_sources):
    t = _tile(seq, SWA_T, SWA_WINDOW)
    nt = seq // t
    per = t // SWA_WINDOW
    qw = q_heads * HEAD_DIM
    kvw = SWA_KV_HEADS * HEAD_DIM
    assert q_col % qw == 0 and k_col % kvw == 0 and v_col % kvw == 0

    def prev(b, i):
        return jnp.maximum((b * nt + i) * per - 1, 0)

    def prev_pos(i):
        return jnp.maximum(i * per - 1, 0)

    strides, cast_in, cast_out, cast_shapes, cast_ops = _cast_plan(cast_sources, (batch, nt))
    return pl.pallas_call(
        functools.partial(_swa_kernel, scale=HEAD_DIM ** -0.5, group=q_heads // SWA_KV_HEADS, cast_strides=strides),
        grid=(batch, nt),
        in_specs=[
            pl.BlockSpec(memory_space=pltpu.SMEM),
            pl.BlockSpec((t, qw), lambda b, i: (b * nt + i, q_col // qw)),
            pl.BlockSpec((t, kvw), lambda b, i: (b * nt + i, k_col // kvw)),
            pl.BlockSpec((t, kvw), lambda b, i: (b * nt + i, v_col // kvw)),
            pl.BlockSpec((SWA_WINDOW, kvw), lambda b, i: (prev(b, i), k_col // kvw)),
            pl.BlockSpec((SWA_WINDOW, kvw), lambda b, i: (prev(b, i), v_col // kvw)),
            pl.BlockSpec((t, HEAD_DIM), lambda b, i: (i, 0)),
            pl.BlockSpec((t, HEAD_DIM), lambda b, i: (i, 0)),
            pl.BlockSpec((SWA_WINDOW, HEAD_DIM), lambda b, i: (prev_pos(i), 0)),
            pl.BlockSpec((SWA_WINDOW, HEAD_DIM), lambda b, i: (prev_pos(i), 0)),
        ] + cast_in,
        out_specs=[pl.BlockSpec((t, qw), lambda b, i: (b * nt + i, 0))] + cast_out,
        out_shape=[jax.ShapeDtypeStruct((batch * seq, qw), BF16)] + cast_shapes,
        compiler_params=_params(("arbitrary", "arbitrary")),
        name="swa_attention",
    )(sinks, proj, proj, proj, proj, proj, cos, sin_signed, cos, sin_signed, *cast_ops)


def _cummax_rows(x):
    n = x.shape[0]
    row = lax.broadcasted_iota(jnp.int32, x.shape, 0)
    shift = 1
    while shift < n:
        x = jnp.maximum(x, jnp.where(row >= shift, pltpu.roll(x, shift, 0), -jnp.inf))
        shift *= 2
    return x


def _mlstm_kernel(q_ref, k_ref, v_ref, og_ref, g_ref, gt_ref, *rest, dk, dv, cast_strides):
    heads = MLSTM_HEADS
    nc = len(cast_strides)
    o_ref = rest[nc]
    state_ref, m_ref = rest[2 * nc + 1:]
    _cast_step(rest[:nc], rest[nc + 1:2 * nc + 1], cast_strides)

    @pl.when(pl.program_id(1) == 0)
    def _():
        state_ref[...] = jnp.zeros_like(state_ref)
        m_ref[...] = jnp.zeros_like(m_ref)

    n = q_ref.shape[0]
    g = g_ref[...]
    gt = gt_ref[...]
    row = lax.broadcasted_iota(jnp.int32, (n, n), 0)
    col = lax.broadcasted_iota(jnp.int32, (n, n), 1)
    causal = col <= row
    b_row_all = _cumsum_lanes(_log_sigmoid(gt), (row <= col).astype(BF16))
    b_all = pltpu.roll(_cumsum_rows(causal.astype(BF16), _log_sigmoid(g)), LANES - heads, 1)
    kc_all = g - b_all
    kcmax_all = _cummax_rows(kc_all)
    spread = (lax.broadcasted_iota(jnp.int32, (LANES, heads * LANES), 1) // LANES
              == lax.broadcasted_iota(jnp.int32, (LANES, heads * LANES), 0)).astype(BF16)
    b_wide, kc_wide, kcmax_wide = (
        sum(jnp.dot(term, spread, preferred_element_type=F32) for term in _split3(x))
        for x in (b_all, kc_all, kcmax_all))
    ones = jnp.ones((n, LANES), BF16)
    tiles = n // LANES

    def wide(x, reps):
        return jnp.concatenate([x] * reps, axis=1)

    for h in range(heads):
        hs = slice(h * LANES, (h + 1) * LANES)
        b_col, kc_col, kcmax_col = b_wide[:, hs], kc_wide[:, hs], kcmax_wide[:, hs]
        i_row = gt[h:h + 1, :]
        b_row = b_row_all[heads + h:heads + h + 1, :]
        b_last = b_row[:, n - 1:n]
        m_prev = m_ref[h]

        m_inter = b_col + m_prev
        m_t = jnp.maximum(m_inter, b_col + kcmax_col)
        inter = jnp.exp(m_inter - m_t)
        d = jnp.where(causal, (wide(b_col, tiles) - b_row) + i_row, -jnp.inf)
        wgt = jnp.exp(d - wide(m_t, tiles))

        qh = q_ref[:, h * dk:(h + 1) * dk]
        kf = k_ref[:, h * dk:(h + 1) * dk].astype(F32) * (dk ** -0.5)
        kh = kf.astype(BF16)
        v_ext = jnp.concatenate([v_ref[:, h * dv:(h + 1) * dv], ones], axis=1)
        state = state_ref[h]

        sm = wgt * lax.dot_general(qh, kh, NT_DIMS, preferred_element_type=F32)
        tot = wide(inter, dv // LANES + 1) * jnp.dot(qh, state.astype(BF16), preferred_element_type=F32)
        tot = tot + jnp.dot(sm.astype(BF16), v_ext, preferred_element_type=F32)
        den = jnp.maximum(jnp.abs(tot[:, dv:]), jnp.exp(-m_t))
        hid = tot[:, :dv] * wide(1.0 / den, dv // LANES)
        gate = _sigmoid(og_ref[:, h * dv:(h + 1) * dv].astype(F32))
        o_ref[:, h * dv:(h + 1) * dv] = (gate * hid).astype(o_ref.dtype)

        m_new = jnp.maximum(b_last + m_prev, jnp.max((b_last - b_row) + i_row, axis=-1, keepdims=True))
        decay = jnp.exp(b_last + m_prev - m_new)
        w_end = jnp.exp((b_last + kc_col) - m_new)
        kw = (kf * w_end).astype(BF16)
        state_ref[h] = decay * state + lax.dot_general(kw, v_ext, TN_DIMS, preferred_element_type=F32)
        m_ref[h] = m_new


def _mlstm(proj, g, gt, batch, seq, dk, dv, cast_sources):
    heads = MLSTM_HEADS
    n = _tile(seq, MLSTM_L, LANES)
    nc = seq // n
    qk, vw = heads * dk, heads * dv
    assert vw % qk == 0
    r = vw // qk
    strides, cast_in, cast_out, cast_shapes, cast_ops = _cast_plan(cast_sources, (batch, nc))
    return pl.pallas_call(
        functools.partial(_mlstm_kernel, dk=dk, dv=dv, cast_strides=strides),
        grid=(batch, nc),
        in_specs=[
            pl.BlockSpec((n, qk), lambda b, c: (b * nc + c, 0)),
            pl.BlockSpec((n, qk), lambda b, c: (b * nc + c, 1)),
            pl.BlockSpec((n, vw), lambda b, c: (b * nc + c, 2 // r)),
            pl.BlockSpec((n, vw), lambda b, c: (b * nc + c, 2 // r + 1)),
            pl.BlockSpec((n, LANES), lambda b, c: (b * nc + c, 0)),
            pl.BlockSpec((GATE_ROWS, n), lambda b, c: (0, b * nc + c)),
        ] + cast_in,
        out_specs=[pl.BlockSpec((n, vw), lambda b, c: (b * nc + c, 0))] + cast_out,
        out_shape=[jax.ShapeDtypeStruct((batch * seq, vw), BF16)] + cast_shapes,
        scratch_shapes=[pltpu.VMEM((heads, dk, dv + LANES), F32), pltpu.VMEM((heads, 1, 1), F32)],
        compiler_params=_params(("arbitrary", "arbitrary")),
        name="mlstm",
    )(proj, proj, proj, proj, g, gt, *cast_ops)


def _outproj_ln_kernel(*refs, alpha, n_in):
    h_refs, w_refs = refs[:n_in], refs[n_in:2 * n_in]
    x_ref, g_ref, b_ref, o_ref = refs[2 * n_in:]
    for r in range(o_ref.shape[0] // OUT_ROWS):
        rows = slice(r * OUT_ROWS, (r + 1) * OUT_ROWS)
        o_ref[rows, :] = sum(jnp.dot(h[rows, :], w[...], preferred_element_type=F32) for h, w in zip(h_refs, w_refs))
        o_ref[rows, :] = _layer_norm(alpha * x_ref[rows, :] + o_ref[rows, :], g_ref[...], b_ref[...])


def _out_projection_ln(hs, w, x, ln_g, ln_b, alpha):
    m, d = x.shape
    bm = _tile(m, OUT_BM, LANES)
    widths = [h.shape[1] for h in hs]
    assert all(wd == widths[0] for wd in widths) and w.shape[0] == sum(widths)
    in_specs = [pl.BlockSpec((bm, wd), lambda i: (i, 0)) for wd in widths]
    in_specs += [pl.BlockSpec((wd, d), lambda i, k=k: (k, 0)) for k, wd in enumerate(widths)]
    in_specs += [pl.BlockSpec((bm, d), lambda i: (i, 0)),
                 pl.BlockSpec((1, d), lambda i: (0, 0)),
                 pl.BlockSpec((1, d), lambda i: (0, 0))]
    return pl.pallas_call(
        functools.partial(_outproj_ln_kernel, alpha=alpha, n_in=len(hs)),
        grid=(m // bm,),
        in_specs=in_specs,
        out_specs=pl.BlockSpec((bm, d), lambda i: (i, 0)),
        out_shape=jax.ShapeDtypeStruct((m, d), F32),
        compiler_params=_params(("arbitrary",)),
        name="out_projection_ln",
    )(*hs, *([w] * len(hs)), x, ln_g, ln_b)


def _causal_conv(u, prev, p, r0):
    w0, w1, w2, b = p[r0:r0 + 1], p[r0 + 1:r0 + 2], p[r0 + 2:r0 + 3], p[r0 + 3:r0 + 4]
    body = b + w0 * pltpu.roll(u, 2, 0) + w1 * pltpu.roll(u, 1, 0) + w2 * u
    top = jnp.concatenate([prev, u[:HEAD_ROWS, :]], axis=0)
    head = b + w0 * pltpu.roll(top, 2, 0) + w1 * pltpu.roll(top, 1, 0) + w2 * top
    return body, head[SUBLANES:, :]


def _ffn_kernel(x_ref, wg_ref, wv_ref, cp_ref, wd_ref, lg_ref, lb_ref, o_ref,
                xb_ref, h0_ref, h1_ref, pg_ref, pv_ref, ug_ref, uv_ref, *, alpha, tiles_per_seq, nf):
    i = pl.program_id(0)
    j = pl.program_id(1)
    bm = x_ref.shape[0]
    h_refs = (h0_ref, h1_ref)

    def up(h_ref, first_step=False):
        kept_rows = jnp.where(i % tiles_per_seq != 0, SUBLANES, 0)
        keep = lax.broadcasted_iota(jnp.int32, (SUBLANES, FFN_CHUNK), 0) < kept_rows
        chunks = [slice(c * FFN_CHUNK, (c + 1) * FFN_CHUNK) for c in range(h_ref.shape[1] // FFN_CHUNK)]
        prev_g = [jnp.where(keep, pg_ref[j, :, cs], 0.0) for cs in chunks]
        prev_v = [jnp.where(keep, pv_ref[j, :, cs], 0.0) for cs in chunks]
        for r in range(bm // FFN_ROWS):
            rows = slice(r * FFN_ROWS, (r + 1) * FFN_ROWS)
            if first_step:
                xb_ref[rows, :] = x_ref[rows, :].astype(BF16)
            xr = xb_ref[rows, :]
            for c, cs in enumerate(chunks):
                slot = (r * len(chunks) + c) % FFN_SLOTS
                ug_ref[slot] = jnp.dot(xr, wg_ref[:, cs], preferred_element_type=F32)
                uv_ref[slot] = jnp.dot(xr, wv_ref[:, cs], preferred_element_type=F32)
                ug, uv = ug_ref[slot], uv_ref[slot]
                cp = cp_ref[j, :, cs]
                cg, cg_top = _causal_conv(ug, prev_g[c], cp, 0)
                cv, cv_top = _causal_conv(uv, prev_v[c], cp, CONV_WIDTH + 1)
                prev_g[c] = ug[FFN_ROWS - SUBLANES:, :]
                prev_v[c] = uv[FFN_ROWS - SUBLANES:, :]
                h_ref[r * FFN_ROWS:(r + 1) * FFN_ROWS, cs] = (cg * _sigmoid(cg) * cv).astype(BF16)
                h_ref[r * FFN_ROWS:r * FFN_ROWS + HEAD_ROWS, cs] = (cg_top * _sigmoid(cg_top) * cv_top).astype(BF16)
        for c, cs in enumerate(chunks):
            pg_ref[j, :, cs] = prev_g[c]
            pv_ref[j, :, cs] = prev_v[c]

    def down(h_ref):
        o_ref[...] += jnp.dot(h_ref[...], wd_ref[...], preferred_element_type=F32)

    @pl.when(j == 0)
    def _():
        o_ref[...] = jnp.zeros_like(o_ref)
        up(h_refs[0], first_step=True)

    for parity in range(2):
        @pl.when(jnp.logical_and(jnp.logical_and(j > 0, j < nf), j % 2 == parity))
        def _():
            up(h_refs[parity])
            down(h_refs[1 - parity])

    @pl.when(j == nf)
    def _():
        h_ref = h_refs[(nf - 1) % 2]
        for r in range(bm // FFN_ROWS):
            rows = slice(r * FFN_ROWS, (r + 1) * FFN_ROWS)
            o_ref[rows, :] += jnp.dot(h_ref[rows, :], wd_ref[...], preferred_element_type=F32)
            o_ref[rows, :] = _layer_norm(alpha * x_ref[rows, :] + o_ref[rows, :], lg_ref[...], lb_ref[...])


def _conv_ffn_ln(x, w_up, w_down, conv_w, conv_b, ln_g, ln_b, alpha, seq):
    m, d = x.shape
    f = w_down.shape[0]
    bm = _tile(seq, FFN_BM, 2 * SUBLANES)
    bf = _tile(f, FFN_BF, LANES)
    nf = f // bf

    conv_p = jnp.concatenate([conv_w[:, :f], conv_b[:, :f], conv_w[:, f:], conv_b[:, f:]], axis=0)
    conv_p = conv_p.reshape(2 * (CONV_WIDTH + 1), nf, bf).transpose(1, 0, 2)

    def up_blk(j):
        return jnp.minimum(j, nf - 1)

    def down_blk(j):
        return jnp.maximum(j - 1, 0)

    return pl.pallas_call(
        functools.partial(_ffn_kernel, alpha=alpha, tiles_per_seq=seq // bm, nf=nf),
        grid=(m // bm, nf + 1),
        in_specs=[
            pl.BlockSpec((bm, d), lambda i, j: (i, 0)),
            pl.BlockSpec((d, bf), lambda i, j: (0, up_blk(j))),
            pl.BlockSpec((d, bf), lambda i, j: (0, nf + up_blk(j))),
            pl.BlockSpec(conv_p.shape, lambda i, j: (0, 0, 0)),
            pl.BlockSpec((bf, d), lambda i, j: (down_blk(j), 0)),
            pl.BlockSpec((1, d), lambda i, j: (0, 0)),
            pl.BlockSpec((1, d), lambda i, j: (0, 0)),
        ],
        out_specs=pl.BlockSpec((bm, d), lambda i, j: (i, 0)),
        out_shape=jax.ShapeDtypeStruct((m, d), F32),
        scratch_shapes=[
            pltpu.VMEM((bm, d), BF16),
            pltpu.VMEM((bm, bf), BF16),
            pltpu.VMEM((bm, bf), BF16),
            pltpu.VMEM((nf, SUBLANES, bf), F32),
            pltpu.VMEM((nf, SUBLANES, bf), F32),
            pltpu.VMEM((FFN_SLOTS, FFN_ROWS, FFN_CHUNK), F32),
            pltpu.VMEM((FFN_SLOTS, FFN_ROWS, FFN_CHUNK), F32),
        ],
        compiler_params=_params(("arbitrary", "arbitrary")),
        name="conv_ffn_ln",
    )(x, w_up, w_up, conv_p, w_down, ln_g, ln_b)


def _gate_params(w, b):
    d, n = w.shape
    wg = jnp.zeros((d, LANES), BF16).at[:, :n].set(w.astype(BF16))
    bg = jnp.zeros((1, LANES), F32).at[0, :n].set(b)
    return wg, bg


def _rope_tables(seq):
    half = HEAD_DIM // 2
    inv_freq = jnp.power(ROPE_THETA, -jnp.arange(half, dtype=F32) * (2.0 / HEAD_DIM))
    ang = jnp.arange(seq, dtype=F32)[:, None] * inv_freq[None, :]
    cos, sin = jnp.cos(ang), jnp.sin(ang)
    return jnp.concatenate([cos, cos], axis=-1), jnp.concatenate([-sin, sin], axis=-1)


def kernel(x, attn_w_in, attn_b_in, attn_sinks, attn_w_out, mlstm_w_in, mlstm_b_in, mlstm_w_out, ffn_w_up,
           ffn_conv_w, ffn_conv_b, ffn_w_down, ln1_g, ln1_b, ln2_g, ln2_b):
    batch, seq, d = x.shape
    depth = ln1_g.shape[0]
    alpha = float((2 * depth) ** 0.25)
    fox_heads = d // (2 * HEAD_DIM)
    swa_heads = d // (2 * HEAD_DIM)
    fox_dim = fox_heads * HEAD_DIM
    fox_f_off = 3 * fox_dim
    dk, dv = d // (2 * MLSTM_HEADS), d // MLSTM_HEADS
    mlstm_main = 2 * MLSTM_HEADS * dk + 2 * MLSTM_HEADS * dv
    cos, sin_signed = _rope_tables(seq)

    gate_lo, gate_hi = fox_f_off, fox_f_off + fox_heads
    attn_w_gate = lax.optimization_barrier(attn_w_in[:, :, gate_lo:gate_hi])
    mlstm_w_gate = lax.optimization_barrier(mlstm_w_in[:, :, mlstm_main:])
    attn_w_main = _regroup_weights(attn_w_in, gate_lo, gate_hi)
    attn_b_main = jnp.concatenate([attn_b_in[:, :gate_lo], attn_b_in[:, gate_hi:]], axis=1)
    mlstm_w_main = mlstm_w_in[:, :, :mlstm_main].astype(BF16)

    h = x.reshape(batch * seq, d)
    for layer in range(depth):
        j = layer // 2
        late_w = [(ffn_w_up, layer), (ffn_w_down, layer), (attn_w_out if layer % 2 == 0 else mlstm_w_out, j)]
        if layer % 2 == 0:
            gates = _gate_params(attn_w_gate[j], attn_b_in[j][gate_lo:gate_hi])
            proj, g, gt = _in_projection(h, attn_w_main, j, attn_b_main[j][None, :], *gates)
            fox = _fox_attention(proj, _fox_gate_cumsum(g, batch, seq), batch, seq, fox_heads)
            swa_q = 3 * fox_dim
            swa_k = swa_q + swa_heads * HEAD_DIM
            swa_v = swa_k + SWA_KV_HEADS * HEAD_DIM
            swa, w_up_b, w_down_b, w_out_b = _swa_attention(proj, attn_sinks[j], cos, sin_signed, batch, seq, swa_q,
                                                            swa_k, swa_v, swa_heads, late_w)
            h = _out_projection_ln([fox, swa], w_out_b, h, ln1_g[layer][None, :], ln1_b[layer][None, :], alpha)
        else:
            gates = _gate_params(mlstm_w_gate[j], mlstm_b_in[j][mlstm_main:])
            proj, g, gt = _in_projection(h, mlstm_w_main, j, mlstm_b_in[j][None, :mlstm_main], *gates)
            mixed, w_up_b, w_down_b, w_out_b = _mlstm(proj, g, gt, batch, seq, dk, dv, late_w)
            h = _out_projection_ln([mixed], w_out_b, h, ln1_g[layer][None, :], ln1_b[layer][None, :], alpha)
        h = _conv_ffn_ln(h, w_up_b, w_down_b, ffn_conv_w[layer], ffn_conv_b[layer][None, :],
                         ln2_g[layer][None, :], ln2_b[layer][None, :], alpha, seq)
    return h.reshape(batch, seq, d)
```

```python
import functools

import jax
import jax.numpy as jnp
from jax import lax
from jax.experimental import pallas as pl
from jax.experimental.pallas import tpu as pltpu

F32 = jnp.float32
BF16 = jnp.bfloat16

HEAD_DIM = 128
SWA_KV_HEADS = 2
SWA_WINDOW = 128
ROPE_THETA = 10000.0
MLSTM_HEADS = 8
CONV_WIDTH = 3
LN_EPS = 1e-5
LOG2_E = 1.4426950408889634

LANES = 128
SUBLANES = 8
GATE_ROWS = 16
HEAD_ROWS = 16
VMEM_LIMIT_BYTES = 60 * 1024 * 1024

PROJ_BM = 1024
PROJ_BN = 1536
OUT_BM = 512
OUT_ROWS = 128
FFN_BM = 1024
FFN_BF = 512
FFN_CHUNK = 256
FFN_ROWS = 256
FFN_SLOTS = 2
FOX_TQ = 1024
FOX_GROUP = 2
SWA_T = 512
CUM_T = 512
MLSTM_L = 256

NT_DIMS = (((1,), (1,)), ((), ()))
TN_DIMS = (((0,), (0,)), ((), ()))


def _tile(n, pref, unit):
    t = min(pref, n)
    while n % t or t % unit:
        t -= unit
    assert t > 0, (n, pref, unit)
    return t


def _params(sem):
    return pltpu.CompilerParams(dimension_semantics=sem, vmem_limit_bytes=VMEM_LIMIT_BYTES)


def _log_sigmoid(x):
    return jnp.minimum(x, 0.0) - jnp.log(1.0 + jnp.exp(-jnp.abs(x)))


def _sigmoid(x):
    return 1.0 / (1.0 + jnp.exp(-x))


def _split3(x):
    h1 = x.astype(BF16)
    r1 = x - h1.astype(F32)
    h2 = r1.astype(BF16)
    h3 = (r1 - h2.astype(F32)).astype(BF16)
    return h1, h2, h3


def _cumsum_rows(tri, x):
    return sum(jnp.dot(tri, h, preferred_element_type=F32) for h in _split3(x))


def _cumsum_lanes(x, tri_t):
    return sum(jnp.dot(h, tri_t, preferred_element_type=F32) for h in _split3(x))


def _layer_norm(z, g, b):
    mu = jnp.mean(z, axis=-1, keepdims=True)
    zc = z - mu
    var = jnp.mean(zc * zc, axis=-1, keepdims=True)
    return zc * lax.rsqrt(var + LN_EPS) * g + b


def _cast_plan(sources, grid):
    nb = grid[1]
    steps = grid[0] * nb
    strides, in_specs, out_specs, out_shapes, operands = [], [], [], [], []
    for arr, layer in sources:
        _, rows, cols = arr.shape
        chunks = steps
        while steps % chunks or rows % chunks or (rows // chunks) % (2 * SUBLANES):
            chunks -= 1
        stride, r = steps // chunks, rows // chunks
        strides.append(stride)
        in_specs.append(pl.BlockSpec((None, r, cols), lambda a, b, stride=stride, layer=layer:
                                     (layer, (a * nb + b) // stride, 0)))
        out_specs.append(pl.BlockSpec((r, cols), lambda a, b, stride=stride: ((a * nb + b) // stride, 0)))
        out_shapes.append(jax.ShapeDtypeStruct((rows, cols), BF16))
        operands.append(arr)
    return tuple(strides), in_specs, out_specs, out_shapes, operands


def _cast_step(src_refs, dst_refs, strides):
    step = pl.program_id(0) * pl.num_programs(1) + pl.program_id(1)
    for src, dst, stride in zip(src_refs, dst_refs, strides):
        @pl.when(step % stride == 0)
        def _(src=src, dst=dst):
            dst[...] = src[...].astype(BF16)


def _proj_kernel(x_ref, w_ref, b_ref, wg_ref, bg_ref, o_ref, g_ref, gt_ref, xb_ref):
    @pl.when(pl.program_id(1) == 0)
    def _():
        xb = x_ref[...].astype(BF16)
        xb_ref[...] = xb
        g = lax.dot_general(xb, wg_ref[...], NT_DIMS, preferred_element_type=F32) + bg_ref[...]
        g_ref[...] = g
        gt_ref[...] = g.T[:GATE_ROWS, :]

    acc = lax.dot_general(xb_ref[...], w_ref[...], NT_DIMS, preferred_element_type=F32)
    o_ref[...] = (acc + b_ref[...]).astype(o_ref.dtype)


def _in_projection(x, w, layer, b, wg, bg):
    m, d = x.shape
    n = w.shape[1]
    bm = _tile(m, PROJ_BM, LANES)
    bn = _tile(n, PROJ_BN, LANES)
    return pl.pallas_call(
        _proj_kernel,
        grid=(m // bm, n // bn),
        in_specs=[
            pl.BlockSpec((bm, d), lambda i, j: (i, 0)),
            pl.BlockSpec((None, bn, d), lambda i, j: (layer, j, 0)),
            pl.BlockSpec((1, bn), lambda i, j: (0, j)),
            pl.BlockSpec((LANES, d), lambda i, j: (0, 0)),
            pl.BlockSpec((1, LANES), lambda i, j: (0, 0)),
        ],
        out_specs=[
            pl.BlockSpec((bm, bn), lambda i, j: (i, j)),
            pl.BlockSpec((bm, LANES), lambda i, j: (i, 0)),
            pl.BlockSpec((GATE_ROWS, bm), lambda i, j: (0, i)),
        ],
        out_shape=[
            jax.ShapeDtypeStruct((m, n), BF16),
            jax.ShapeDtypeStruct((m, LANES), F32),
            jax.ShapeDtypeStruct((GATE_ROWS, m), F32),
        ],
        scratch_shapes=[pltpu.VMEM((bm, d), BF16)],
        compiler_params=_params(("arbitrary", "arbitrary")),
        name="in_projection",
    )(x, w, b, wg, bg)


def _fox_gate_kernel(g_ref, c_ref, carry_ref):
    @pl.when(pl.program_id(1) == 0)
    def _():
        carry_ref[...] = jnp.zeros_like(carry_ref)

    t = g_ref.shape[0]
    row = lax.broadcasted_iota(jnp.int32, (t, t), 0)
    col = lax.broadcasted_iota(jnp.int32, (t, t), 1)
    c = _cumsum_rows((col <= row).astype(BF16), _log_sigmoid(g_ref[...])) + carry_ref[...]
    c_ref[...] = c
    carry_ref[...] = c[t - 1:t, :]


def _fox_gate_cumsum(g, batch, seq):
    t = _tile(seq, CUM_T, LANES)
    ns = seq // t
    return pl.pallas_call(
        _fox_gate_kernel,
        grid=(batch, ns),
        in_specs=[pl.BlockSpec((t, LANES), lambda b, s: (b * ns + s, 0))],
        out_specs=pl.BlockSpec((t, LANES), lambda b, s: (b * ns + s, 0)),
        out_shape=jax.ShapeDtypeStruct((batch * seq, LANES), F32),
        scratch_shapes=[pltpu.VMEM((1, LANES), F32)],
        compiler_params=_params(("arbitrary", "arbitrary")),
        name="fox_gate_cumsum",
    )(g)


def _bias_lanes(c, ones_first):
    n = c.shape[0]
    h1, h2, h3 = (h.astype(F32) for h in _split3(c))
    lane = lax.broadcasted_iota(jnp.int32, (n, HEAD_DIM), 1)
    lo = 3 if ones_first else 0
    split = jnp.where(lane == lo, h1, jnp.where(lane == lo + 1, h2, jnp.where(lane == lo + 2, h3, 0.0)))
    ones = jnp.logical_and(lane >= 3 - lo, lane < 6 - lo)
    return jnp.where(ones, 1.0, split).astype(BF16)


def _fox_kernel(q_ref, k_ref, v_ref, c_ref, o_ref, kb_ref, m_ref, acc_ref, s_ref, *, scale, group, tq, tk):
    first_head = pl.program_id(1) * group
    per = tq // tk
    assert per == 2
    unit = (lax.broadcasted_iota(jnp.int32, (tk, HEAD_DIM), 1) == 0).astype(BF16)
    slices = [slice(g * HEAD_DIM, (g + 1) * HEAD_DIM) for g in range(group)]

    def head_lane(c, g):
        lane = lax.broadcasted_iota(jnp.int32, c.shape, 1)
        return jnp.sum(jnp.where(lane == first_head + g, c, 0.0), axis=-1, keepdims=True) * LOG2_E

    for g in range(group):
        kb_ref[g] = _bias_lanes(-head_lane(c_ref[...], g), ones_first=False)

    @pl.loop(0, q_ref.shape[0] // tq)
    def _(qi):
        q_rows = pl.ds(pl.multiple_of(qi * tq, tq), tq)
        ct = c_ref[q_rows, :]
        q_ext = []
        for g, sl in enumerate(slices):
            q = (q_ref[q_rows, sl].astype(F32) * (scale * LOG2_E)).astype(BF16)
            q_ext.append(jnp.concatenate([q, _bias_lanes(head_lane(ct, g), ones_first=True)], axis=1))

        def logits(slot, ki):
            k_rows = pl.ds(pl.multiple_of(ki * tk, tk), tk)
            for g, sl in enumerate(slices):
                k_ext = jnp.concatenate([k_ref[k_rows, sl], kb_ref[g, k_rows, :]], axis=1)
                s_ref[slot, g] = lax.dot_general(q_ext[g], k_ext, NT_DIMS, preferred_element_type=F32)

        def absorb(slot, ki, band=None):
            k_rows = pl.ds(pl.multiple_of(ki * tk, tk), tk)
            for g, sl in enumerate(slices):
                s = s_ref[slot, g]
                if band is not None:
                    row = lax.broadcasted_iota(jnp.int32, (tq, tk), 0)
                    col = lax.broadcasted_iota(jnp.int32, (tq, tk), 1)
                    s = jnp.where(col + band * tk <= row, s, -jnp.inf)
                m = m_ref[g]
                m_new = jnp.maximum(m, jnp.max(s, axis=-1, keepdims=True))
                p = jnp.exp2(s - m_new).astype(BF16)
                v_ext = jnp.concatenate([v_ref[k_rows, sl], unit], axis=1)
                acc_ref[g] = jnp.exp2(m - m_new) * acc_ref[g] + jnp.dot(p, v_ext, preferred_element_type=F32)
                m_ref[g] = m_new

        m_ref[...] = jnp.full(m_ref.shape, -jnp.inf, F32)
        acc_ref[...] = jnp.zeros_like(acc_ref)

        logits(0, 0)

        @pl.loop(0, qi)
        def _(pair):
            ki = per * pair
            logits(1, ki + 1)
            absorb(0, ki)
            logits(0, ki + 2)
            absorb(1, ki + 1)

        logits(1, per * qi + 1)
        absorb(0, per * qi, band=0)
        absorb(1, per * qi + 1, band=1)

        for g, sl in enumerate(slices):
            acc = acc_ref[g]
            o_ref[q_rows, sl] = (acc[:, :HEAD_DIM] / acc[:, HEAD_DIM:HEAD_DIM + 1]).astype(o_ref.dtype)


def _fox_attention(proj, c, batch, seq, heads):
    tq = _tile(seq, FOX_TQ, LANES)
    tk = tq // 2
    group = FOX_GROUP
    assert heads % group == 0
    hb = heads // group
    gw = group * HEAD_DIM
    return pl.pallas_call(
        functools.partial(_fox_kernel, scale=HEAD_DIM ** -0.5, group=group, tq=tq, tk=tk),
        grid=(batch, hb),
        in_specs=[
            pl.BlockSpec((seq, gw), lambda b, h: (b, h)),
            pl.BlockSpec((seq, gw), lambda b, h: (b, hb + h)),
            pl.BlockSpec((seq, gw), lambda b, h: (b, 2 * hb + h)),
            pl.BlockSpec((seq, LANES), lambda b, h: (b, 0)),
        ],
        out_specs=pl.BlockSpec((seq, gw), lambda b, h: (b, h)),
        out_shape=jax.ShapeDtypeStruct((batch * seq, heads * HEAD_DIM), BF16),
        scratch_shapes=[pltpu.VMEM((group, seq, HEAD_DIM), BF16),
                        pltpu.VMEM((group, tq, 1), F32),
                        pltpu.VMEM((group, tq, 2 * HEAD_DIM), F32),
                        pltpu.VMEM((2, group, tq, tk), F32)],
        compiler_params=_params(("arbitrary", "arbitrary")),
        name="fox_attention",
    )(proj, proj, proj, c)


def _rope(x, cos, sin_signed):
    xf = x.astype(F32)
    return xf * cos + pltpu.roll(xf, HEAD_DIM // 2, 1) * sin_signed


def _swa_kernel(sink_ref, q_ref, k_ref, v_ref, kp_ref, vp_ref, cos_ref, sin_ref, cosp_ref, sinp_ref, *rest,
                scale, group, cast_strides):
    nc = len(cast_strides)
    o_ref = rest[nc]
    _cast_step(rest[:nc], rest[nc + 1:], cast_strides)
    w = SWA_WINDOW
    first_visible = jnp.where(pl.program_id(1) == 0, w, 0)
    nblk = q_ref.shape[0] // w
    cos, sin = cos_ref[...], sin_ref[...]
    cos_q, sin_q = cos * scale, sin * scale
    cosp, sinp = cosp_ref[...], sinp_ref[...]
    row = lax.broadcasted_iota(jnp.int32, (w, 2 * w), 0)
    col = lax.broadcasted_iota(jnp.int32, (w, 2 * w), 1)
    valid = jnp.logical_and(col > row, col - w <= row)
    valid_first = jnp.logical_and(valid, col >= first_visible)
    for kv in range(SWA_KV_HEADS):
        ksl = slice(kv * HEAD_DIM, (kv + 1) * HEAD_DIM)
        k_all = jnp.concatenate([_rope(kp_ref[:, ksl], cosp, sinp), _rope(k_ref[:, ksl], cos, sin)],
                                axis=0).astype(BF16)
        v_all = jnp.concatenate([vp_ref[:, ksl], v_ref[:, ksl]], axis=0)
        for g in range(group):
            hq = kv * group + g
            qsl = slice(hq * HEAD_DIM, (hq + 1) * HEAD_DIM)
            q = _rope(q_ref[:, qsl], cos_q, sin_q).astype(BF16)
            sink = sink_ref[hq]
            for blk in range(nblk):
                qb = q[blk * w:(blk + 1) * w, :]
                kb = k_all[blk * w:(blk + 2) * w, :]
                vb = v_all[blk * w:(blk + 2) * w, :]
                s = lax.dot_general(qb, kb, NT_DIMS, preferred_element_type=F32)
                s = jnp.where(valid_first if blk == 0 else valid, s, -jnp.inf)
                m = jnp.maximum(jnp.max(s, axis=-1, keepdims=True), sink)
                p = jnp.exp(s - m)
                denom = jnp.sum(p, axis=-1, keepdims=True) + jnp.exp(sink - m)
                o = jnp.dot((p * (1.0 / denom)).astype(BF16), vb, preferred_element_type=F32)
                o_ref[blk * w:(blk + 1) * w, qsl] = o.astype(o_ref.dtype)


def _swa_attention(proj, sinks, cos, sin_signed, batch, seq, q_col, k_col, v_col, q_heads, cast_sources):
    t = _tile(seq, SWA_T, SWA_WINDOW)
    nt = seq // t
    per = t // SWA_WINDOW
    qw = q_heads * HEAD_DIM
    kvw = SWA_KV_HEADS * HEAD_DIM
    assert q_col % qw == 0 and k_col % kvw == 0 and v_col % kvw == 0

    def prev(b, i):
        return jnp.maximum((b * nt + i) * per - 1, 0)

    def prev_pos(i):
        return jnp.maximum(i * per - 1, 0)

    strides, cast_in, cast_out, cast_shapes, cast_ops = _cast_plan(cast_sources, (batch, nt))
    return pl.pallas_call(
        functools.partial(_swa_kernel, scale=HEAD_DIM ** -0.5, group=q_heads // SWA_KV_HEADS, cast_strides=strides),
        grid=(batch, nt),
        in_specs=[
            pl.BlockSpec(memory_space=pltpu.SMEM),
            pl.BlockSpec((t, qw), lambda b, i: (b * nt + i, q_col // qw)),
            pl.BlockSpec((t, kvw), lambda b, i: (b * nt + i, k_col // kvw)),
            pl.BlockSpec((t, kvw), lambda b, i: (b * nt + i, v_col // kvw)),
            pl.BlockSpec((SWA_WINDOW, kvw), lambda b, i: (prev(b, i), k_col // kvw)),
            pl.BlockSpec((SWA_WINDOW, kvw), lambda b, i: (prev(b, i), v_col // kvw)),
            pl.BlockSpec((t, HEAD_DIM), lambda b, i: (i, 0)),
            pl.BlockSpec((t, HEAD_DIM), lambda b, i: (i, 0)),
            pl.BlockSpec((SWA_WINDOW, HEAD_DIM), lambda b, i: (prev_pos(i), 0)),
            pl.BlockSpec((SWA_WINDOW, HEAD_DIM), lambda b, i: (prev_pos(i), 0)),
        ] + cast_in,
        out_specs=[pl.BlockSpec((t, qw), lambda b, i: (b * nt + i, 0))] + cast_out,
        out_shape=[jax.ShapeDtypeStruct((batch * seq, qw), BF16)] + cast_shapes,
        compiler_params=_params(("arbitrary", "arbitrary")),
        name="swa_attention",
    )(sinks, proj, proj, proj, proj, proj, cos, sin_signed, cos, sin_signed, *cast_ops)


def _cummax_rows(x):
    n = x.shape[0]
    row = lax.broadcasted_iota(jnp.int32, x.shape, 0)
    shift = 1
    while shift < n:
        x = jnp.maximum(x, jnp.where(row >= shift, pltpu.roll(x, shift, 0), -jnp.inf))
        shift *= 2
    return x


def _mlstm_kernel(q_ref, k_ref, v_ref, og_ref, g_ref, gt_ref, *rest, dk, dv, cast_strides):
    heads = MLSTM_HEADS
    nc = len(cast_strides)
    o_ref = rest[nc]
    state_ref, m_ref = rest[2 * nc + 1:]
    _cast_step(rest[:nc], rest[nc + 1:2 * nc + 1], cast_strides)

    @pl.when(pl.program_id(1) == 0)
    def _():
        state_ref[...] = jnp.zeros_like(state_ref)
        m_ref[...] = jnp.zeros_like(m_ref)

    n = q_ref.shape[0]
    g = g_ref[...]
    gt = gt_ref[...]
    row = lax.broadcasted_iota(jnp.int32, (n, n), 0)
    col = lax.broadcasted_iota(jnp.int32, (n, n), 1)
    causal = col <= row
    b_row_all = _cumsum_lanes(_log_sigmoid(gt), (row <= col).astype(BF16))
    b_all = pltpu.roll(_cumsum_rows(causal.astype(BF16), _log_sigmoid(g)), LANES - heads, 1)
    kc_all = g - b_all
    kcmax_all = _cummax_rows(kc_all)
    spread = (lax.broadcasted_iota(jnp.int32, (LANES, heads * LANES), 1) // LANES
              == lax.broadcasted_iota(jnp.int32, (LANES, heads * LANES), 0)).astype(BF16)
    b_wide, kc_wide, kcmax_wide = (
        sum(jnp.dot(term, spread, preferred_element_type=F32) for term in _split3(x))
        for x in (b_all, kc_all, kcmax_all))
    ones = jnp.ones((n, LANES), BF16)
    tiles = n // LANES

    def wide(x, reps):
        return jnp.concatenate([x] * reps, axis=1)

    for h in range(heads):
        hs = slice(h * LANES, (h + 1) * LANES)
        b_col, kc_col, kcmax_col = b_wide[:, hs], kc_wide[:, hs], kcmax_wide[:, hs]
        i_row = gt[h:h + 1, :]
        b_row = b_row_all[heads + h:heads + h + 1, :]
        b_last = b_row[:, n - 1:n]
        m_prev = m_ref[h]

        m_inter = b_col + m_prev
        m_t = jnp.maximum(m_inter, b_col + kcmax_col)
        inter = jnp.exp(m_inter - m_t)
        d = jnp.where(causal, (wide(b_col, tiles) - b_row) + i_row, -jnp.inf)
        wgt = jnp.exp(d - wide(m_t, tiles))

        qh = q_ref[:, h * dk:(h + 1) * dk]
        kf = k_ref[:, h * dk:(h + 1) * dk].astype(F32) * (dk ** -0.5)
        kh = kf.astype(BF16)
        v_ext = jnp.concatenate([v_ref[:, h * dv:(h + 1) * dv], ones], axis=1)
        state = state_ref[h]

        sm = wgt * lax.dot_general(qh, kh, NT_DIMS, preferred_element_type=F32)
        tot = wide(inter, dv // LANES + 1) * jnp.dot(qh, state.astype(BF16), preferred_element_type=F32)
        tot = tot + jnp.dot(sm.astype(BF16), v_ext, preferred_element_type=F32)
        den = jnp.maximum(jnp.abs(tot[:, dv:]), jnp.exp(-m_t))
        hid = tot[:, :dv] * wide(1.0 / den, dv // LANES)
        gate = _sigmoid(og_ref[:, h * dv:(h + 1) * dv].astype(F32))
        o_ref[:, h * dv:(h + 1) * dv] = (gate * hid).astype(o_ref.dtype)

        m_new = jnp.maximum(b_last + m_prev, jnp.max((b_last - b_row) + i_row, axis=-1, keepdims=True))
        decay = jnp.exp(b_last + m_prev - m_new)
        w_end = jnp.exp((b_last + kc_col) - m_new)
        kw = (kf * w_end).astype(BF16)
        state_ref[h] = decay * state + lax.dot_general(kw, v_ext, TN_DIMS, preferred_element_type=F32)
        m_ref[h] = m_new


def _mlstm(proj, g, gt, batch, seq, dk, dv, cast_sources):
    heads = MLSTM_HEADS
    n = _tile(seq, MLSTM_L, LANES)
    nc = seq // n
    qk, vw = heads * dk, heads * dv
    assert vw % qk == 0
    r = vw // qk
    strides, cast_in, cast_out, cast_shapes, cast_ops = _cast_plan(cast_sources, (batch, nc))
    return pl.pallas_call(
        functools.partial(_mlstm_kernel, dk=dk, dv=dv, cast_strides=strides),
        grid=(batch, nc),
        in_specs=[
            pl.BlockSpec((n, qk), lambda b, c: (b * nc + c, 0)),
            pl.BlockSpec((n, qk), lambda b, c: (b * nc + c, 1)),
            pl.BlockSpec((n, vw), lambda b, c: (b * nc + c, 2 // r)),
            pl.BlockSpec((n, vw), lambda b, c: (b * nc + c, 2 // r + 1)),
            pl.BlockSpec((n, LANES), lambda b, c: (b * nc + c, 0)),
            pl.BlockSpec((GATE_ROWS, n), lambda b, c: (0, b * nc + c)),
        ] + cast_in,
        out_specs=[pl.BlockSpec((n, vw), lambda b, c: (b * nc + c, 0))] + cast_out,
        out_shape=[jax.ShapeDtypeStruct((batch * seq, vw), BF16)] + cast_shapes,
        scratch_shapes=[pltpu.VMEM((heads, dk, dv + LANES), F32), pltpu.VMEM((heads, 1, 1), F32)],
        compiler_params=_params(("arbitrary", "arbitrary")),
        name="mlstm",
    )(proj, proj, proj, proj, g, gt, *cast_ops)


def _outproj_ln_kernel(*refs, alpha, n_in):
    h_refs, w_refs = refs[:n_in], refs[n_in:2 * n_in]
    x_ref, g_ref, b_ref, o_ref = refs[2 * n_in:]
    for r in range(o_ref.shape[0] // OUT_ROWS):
        rows = slice(r * OUT_ROWS, (r + 1) * OUT_ROWS)
        o_ref[rows, :] = sum(jnp.dot(h[rows, :], w[...], preferred_element_type=F32) for h, w in zip(h_refs, w_refs))
        o_ref[rows, :] = _layer_norm(alpha * x_ref[rows, :] + o_ref[rows, :], g_ref[...], b_ref[...])


def _out_projection_ln(hs, w, x, ln_g, ln_b, alpha):
    m, d = x.shape
    bm = _tile(m, OUT_BM, LANES)
    widths = [h.shape[1] for h in hs]
    assert all(wd == widths[0] for wd in widths) and w.shape[0] == sum(widths)
    in_specs = [pl.BlockSpec((bm, wd), lambda i: (i, 0)) for wd in widths]
    in_specs += [pl.BlockSpec((wd, d), lambda i, k=k: (k, 0)) for k, wd in enumerate(widths)]
    in_specs += [pl.BlockSpec((bm, d), lambda i: (i, 0)),
                 pl.BlockSpec((1, d), lambda i: (0, 0)),
                 pl.BlockSpec((1, d), lambda i: (0, 0))]
    return pl.pallas_call(
        functools.partial(_outproj_ln_kernel, alpha=alpha, n_in=len(hs)),
        grid=(m // bm,),
        in_specs=in_specs,
        out_specs=pl.BlockSpec((bm, d), lambda i: (i, 0)),
        out_shape=jax.ShapeDtypeStruct((m, d), F32),
        compiler_params=_params(("arbitrary",)),
        name="out_projection_ln",
    )(*hs, *([w] * len(hs)), x, ln_g, ln_b)


def _causal_conv(u, prev, p, r0):
    w0, w1, w2, b = p[r0:r0 + 1], p[r0 + 1:r0 + 2], p[r0 + 2:r0 + 3], p[r0 + 3:r0 + 4]
    body = b + w0 * pltpu.roll(u, 2, 0) + w1 * pltpu.roll(u, 1, 0) + w2 * u
    top = jnp.concatenate([prev, u[:HEAD_ROWS, :]], axis=0)
    head = b + w0 * pltpu.roll(top, 2, 0) + w1 * pltpu.roll(top, 1, 0) + w2 * top
    return body, head[SUBLANES:, :]


def _ffn_kernel(x_ref, wg_ref, wv_ref, cp_ref, wd_ref, lg_ref, lb_ref, o_ref,
                xb_ref, h0_ref, h1_ref, pg_ref, pv_ref, ug_ref, uv_ref, *, alpha, tiles_per_seq, nf):
    i = pl.program_id(0)
    j = pl.program_id(1)
    bm = x_ref.shape[0]
    h_refs = (h0_ref, h1_ref)

    def up(h_ref, first_step=False):
        kept_rows = jnp.where(i % tiles_per_seq != 0, SUBLANES, 0)
        keep = lax.broadcasted_iota(jnp.int32, (SUBLANES, FFN_CHUNK), 0) < kept_rows
        chunks = [slice(c * FFN_CHUNK, (c + 1) * FFN_CHUNK) for c in range(h_ref.shape[1] // FFN_CHUNK)]
        prev_g = [jnp.where(keep, pg_ref[j, :, cs], 0.0) for cs in chunks]
        prev_v = [jnp.where(keep, pv_ref[j, :, cs], 0.0) for cs in chunks]
        for r in range(bm // FFN_ROWS):
            rows = slice(r * FFN_ROWS, (r + 1) * FFN_ROWS)
            if first_step:
                xb_ref[rows, :] = x_ref[rows, :].astype(BF16)
            xr = xb_ref[rows, :]
            for c, cs in enumerate(chunks):
                slot = (r * len(chunks) + c) % FFN_SLOTS
                ug_ref[slot] = jnp.dot(xr, wg_ref[:, cs], preferred_element_type=F32)
                uv_ref[slot] = jnp.dot(xr, wv_ref[:, cs], preferred_element_type=F32)
                ug, uv = ug_ref[slot], uv_ref[slot]
                cp = cp_ref[j, :, cs]
                cg, cg_top = _causal_conv(ug, prev_g[c], cp, 0)
                cv, cv_top = _causal_conv(uv, prev_v[c], cp, CONV_WIDTH + 1)
                prev_g[c] = ug[FFN_ROWS - SUBLANES:, :]
                prev_v[c] = uv[FFN_ROWS - SUBLANES:, :]
                h_ref[r * FFN_ROWS:(r + 1) * FFN_ROWS, cs] = (cg * _sigmoid(cg) * cv).astype(BF16)
                h_ref[r * FFN_ROWS:r * FFN_ROWS + HEAD_ROWS, cs] = (cg_top * _sigmoid(cg_top) * cv_top).astype(BF16)
        for c, cs in enumerate(chunks):
            pg_ref[j, :, cs] = prev_g[c]
            pv_ref[j, :, cs] = prev_v[c]

    def down(h_ref):
        o_ref[...] += jnp.dot(h_ref[...], wd_ref[...], preferred_element_type=F32)

    @pl.when(j == 0)
    def _():
        o_ref[...] = jnp.zeros_like(o_ref)
        up(h_refs[0], first_step=True)

    for parity in range(2):
        @pl.when(jnp.logical_and(jnp.logical_and(j > 0, j < nf), j % 2 == parity))
        def _():
            up(h_refs[parity])
            down(h_refs[1 - parity])

    @pl.when(j == nf)
    def _():
        h_ref = h_refs[(nf - 1) % 2]
        for r in range(bm // FFN_ROWS):
            rows = slice(r * FFN_ROWS, (r + 1) * FFN_ROWS)
            o_ref[rows, :] += jnp.dot(h_ref[rows, :], wd_ref[...], preferred_element_type=F32)
            o_ref[rows, :] = _layer_norm(alpha * x_ref[rows, :] + o_ref[rows, :], lg_ref[...], lb_ref[...])


def _conv_ffn_ln(x, w_up, w_down, conv_w, conv_b, ln_g, ln_b, alpha, seq):
    m, d = x.shape
    f = w_down.shape[0]
    bm = _tile(seq, FFN_BM, 2 * SUBLANES)
    bf = _tile(f, FFN_BF, LANES)
    nf = f // bf

    conv_p = jnp.concatenate([conv_w[:, :f], conv_b[:, :f], conv_w[:, f:], conv_b[:, f:]], axis=0)
    conv_p = conv_p.reshape(2 * (CONV_WIDTH + 1), nf, bf).transpose(1, 0, 2)

    def up_blk(j):
        return jnp.minimum(j, nf - 1)

    def down_blk(j):
        return jnp.maximum(j - 1, 0)

    return pl.pallas_call(
        functools.partial(_ffn_kernel, alpha=alpha, tiles_per_seq=seq // bm, nf=nf),
        grid=(m // bm, nf + 1),
        in_specs=[
            pl.BlockSpec((bm, d), lambda i, j: (i, 0)),
            pl.BlockSpec((d, bf), lambda i, j: (0, up_blk(j))),
            pl.BlockSpec((d, bf), lambda i, j: (0, nf + up_blk(j))),
            pl.BlockSpec(conv_p.shape, lambda i, j: (0, 0, 0)),
            pl.BlockSpec((bf, d), lambda i, j: (down_blk(j), 0)),
            pl.BlockSpec((1, d), lambda i, j: (0, 0)),
            pl.BlockSpec((1, d), lambda i, j: (0, 0)),
        ],
        out_specs=pl.BlockSpec((bm, d), lambda i, j: (i, 0)),
        out_shape=jax.ShapeDtypeStruct((m, d), F32),
        scratch_shapes=[
            pltpu.VMEM((bm, d), BF16),
            pltpu.VMEM((bm, bf), BF16),
            pltpu.VMEM((bm, bf), BF16),
            pltpu.VMEM((nf, SUBLANES, bf), F32),
            pltpu.VMEM((nf, SUBLANES, bf), F32),
            pltpu.VMEM((FFN_SLOTS, FFN_ROWS, FFN_CHUNK), F32),
            pltpu.VMEM((FFN_SLOTS, FFN_ROWS, FFN_CHUNK), F32),
        ],
        compiler_params=_params(("arbitrary", "arbitrary")),
        name="conv_ffn_ln",
    )(x, w_up, w_up, conv_p, w_down, ln_g, ln_b)


def _gate_params(w, b):
    n, d = w.shape
    wg = jnp.zeros((LANES, d), BF16).at[:n, :].set(w.astype(BF16))
    bg = jnp.zeros((1, LANES), F32).at[0, :n].set(b)
    return wg, bg


def _rope_tables(seq):
    half = HEAD_DIM // 2
    inv_freq = jnp.power(ROPE_THETA, -jnp.arange(half, dtype=F32) * (2.0 / HEAD_DIM))
    ang = jnp.arange(seq, dtype=F32)[:, None] * inv_freq[None, :]
    cos, sin = jnp.cos(ang), jnp.sin(ang)
    return jnp.concatenate([cos, cos], axis=-1), jnp.concatenate([-sin, sin], axis=-1)


def kernel(x, attn_w_in, attn_b_in, attn_sinks, attn_w_out, mlstm_w_in, mlstm_b_in, mlstm_w_out, ffn_w_up,
           ffn_conv_w, ffn_conv_b, ffn_w_down, ln1_g, ln1_b, ln2_g, ln2_b):
    batch, seq, d = x.shape
    depth = ln1_g.shape[0]
    alpha = float((2 * depth) ** 0.25)
    fox_heads = d // (2 * HEAD_DIM)
    swa_heads = d // (2 * HEAD_DIM)
    fox_dim = fox_heads * HEAD_DIM
    fox_f_off = 3 * fox_dim
    dk, dv = d // (2 * MLSTM_HEADS), d // MLSTM_HEADS
    mlstm_main = 2 * MLSTM_HEADS * dk + 2 * MLSTM_HEADS * dv
    cos, sin_signed = _rope_tables(seq)

    gate_lo, gate_hi = fox_f_off, fox_f_off + fox_heads
    attn_wt, mlstm_wt = jnp.swapaxes(attn_w_in, 1, 2), jnp.swapaxes(mlstm_w_in, 1, 2)
    attn_w_gate = lax.optimization_barrier(attn_wt[:, gate_lo:gate_hi])
    mlstm_w_gate = lax.optimization_barrier(mlstm_wt[:, mlstm_main:])
    attn_w_main = jnp.concatenate([attn_wt[:, :gate_lo], attn_wt[:, gate_hi:]], axis=1).astype(BF16)
    attn_b_main = jnp.concatenate([attn_b_in[:, :gate_lo], attn_b_in[:, gate_hi:]], axis=1)
    mlstm_w_main = mlstm_wt[:, :mlstm_main].astype(BF16)

    h = x.reshape(batch * seq, d)
    for layer in range(depth):
        j = layer // 2
        late_w = [(ffn_w_up, layer), (ffn_w_down, layer), (attn_w_out if layer % 2 == 0 else mlstm_w_out, j)]
        if layer % 2 == 0:
            gates = _gate_params(attn_w_gate[j], attn_b_in[j][gate_lo:gate_hi])
            proj, g, gt = _in_projection(h, attn_w_main, j, attn_b_main[j][None, :], *gates)
            fox = _fox_attention(proj, _fox_gate_cumsum(g, batch, seq), batch, seq, fox_heads)
            swa_q = 3 * fox_dim
            swa_k = swa_q + swa_heads * HEAD_DIM
            swa_v = swa_k + SWA_KV_HEADS * HEAD_DIM
            swa, w_up_b, w_down_b, w_out_b = _swa_attention(proj, attn_sinks[j], cos, sin_signed, batch, seq, swa_q,
                                                            swa_k, swa_v, swa_heads, late_w)
            h = _out_projection_ln([fox, swa], w_out_b, h, ln1_g[layer][None, :], ln1_b[layer][None, :], alpha)
        else:
            gates = _gate_params(mlstm_w_gate[j], mlstm_b_in[j][mlstm_main:])
            proj, g, gt = _in_projection(h, mlstm_w_main, j, mlstm_b_in[j][None, :mlstm_main], *gates)
            mixed, w_up_b, w_down_b, w_out_b = _mlstm(proj, g, gt, batch, seq, dk, dv, late_w)
            h = _out_projection_ln([mixed], w_out_b, h, ln1_g[layer][None, :], ln1_b[layer][None, :], alpha)
        h = _conv_ffn_ln(h, w_up_b, w_down_b, ffn_conv_w[layer], ffn_conv_b[layer][None, :],
                         ln2_g[layer][None, :], ln2_b[layer][None, :], alpha, seq)
    return h.reshape(batch, seq, d)
```

```python
import functools

import jax
import jax.numpy as jnp
from jax import lax
from jax.experimental import pallas as pl
from jax.experimental.pallas import tpu as pltpu

F32 = jnp.float32
BF16 = jnp.bfloat16

HEAD_DIM = 128
SWA_KV_HEADS = 2
SWA_WINDOW = 128
ROPE_THETA = 10000.0
MLSTM_HEADS = 8
CONV_WIDTH = 3
LN_EPS = 1e-5
LOG2_E = 1.4426950408889634

LANES = 128
SUBLANES = 8
GATE_ROWS = 16
HEAD_ROWS = 16
VMEM_LIMIT_BYTES = 60 * 1024 * 1024

PROJ_BM = 1024
PROJ_BN = 1536
OUT_BM = 512
OUT_ROWS = 128
FFN_BM = 1024
FFN_BF = 512
FFN_CHUNK = 256
FFN_ROWS = 256
FFN_SLOTS = 2
FOX_TQ = 1024
FOX_GROUP = 2
SWA_T = 512
CUM_T = 512
MLSTM_L = 256

NT_DIMS = (((1,), (1,)), ((), ()))
TN_DIMS = (((0,), (0,)), ((), ()))


def _tile(n, pref, unit):
    t = min(pref, n)
    while n % t or t % unit:
        t -= unit
    assert t > 0, (n, pref, unit)
    return t


def _params(sem):
    return pltpu.CompilerParams(dimension_semantics=sem, vmem_limit_bytes=VMEM_LIMIT_BYTES)


def _log_sigmoid(x):
    return jnp.minimum(x, 0.0) - jnp.log(1.0 + jnp.exp(-jnp.abs(x)))


def _sigmoid(x):
    return 1.0 / (1.0 + jnp.exp(-x))


def _split3(x):
    h1 = x.astype(BF16)
    r1 = x - h1.astype(F32)
    h2 = r1.astype(BF16)
    h3 = (r1 - h2.astype(F32)).astype(BF16)
    return h1, h2, h3


def _cumsum_rows(tri, x):
    return sum(jnp.dot(tri, h, preferred_element_type=F32) for h in _split3(x))


def _cumsum_lanes(x, tri_t):
    return sum(jnp.dot(h, tri_t, preferred_element_type=F32) for h in _split3(x))


def _layer_norm(z, g, b):
    mu = jnp.mean(z, axis=-1, keepdims=True)
    zc = z - mu
    var = jnp.mean(zc * zc, axis=-1, keepdims=True)
    return zc * lax.rsqrt(var + LN_EPS) * g + b


def _cast_plan(sources, grid):
    nb = grid[1]
    steps = grid[0] * nb
    strides, in_specs, out_specs, out_shapes, operands = [], [], [], [], []
    for arr, layer in sources:
        _, rows, cols = arr.shape
        chunks = steps
        while steps % chunks or rows % chunks or (rows // chunks) % (2 * SUBLANES):
            chunks -= 1
        stride, r = steps // chunks, rows // chunks
        strides.append(stride)
        in_specs.append(pl.BlockSpec((None, r, cols), lambda a, b, stride=stride, layer=layer:
                                     (layer, (a * nb + b) // stride, 0)))
        out_specs.append(pl.BlockSpec((r, cols), lambda a, b, stride=stride: ((a * nb + b) // stride, 0)))
        out_shapes.append(jax.ShapeDtypeStruct((rows, cols), BF16))
        operands.append(arr)
    return tuple(strides), in_specs, out_specs, out_shapes, operands


def _cast_step(src_refs, dst_refs, strides):
    step = pl.program_id(0) * pl.num_programs(1) + pl.program_id(1)
    for src, dst, stride in zip(src_refs, dst_refs, strides):
        @pl.when(step % stride == 0)
        def _(src=src, dst=dst):
            dst[...] = src[...].astype(BF16)


def _proj_kernel(x_ref, w_ref, b_ref, wg_ref, bg_ref, o_ref, g_ref, gt_ref, xb_ref):
    @pl.when(pl.program_id(1) == 0)
    def _():
        xb = x_ref[...].astype(BF16)
        xb_ref[...] = xb
        g = lax.dot_general(xb, wg_ref[...], NT_DIMS, preferred_element_type=F32) + bg_ref[...]
        g_ref[...] = g
        gt_ref[...] = g.T[:GATE_ROWS, :]

    acc = lax.dot_general(xb_ref[...], w_ref[...], NT_DIMS, preferred_element_type=F32)
    o_ref[...] = (acc + b_ref[...]).astype(o_ref.dtype)


def _in_projection(x, w, layer, b, wg, bg):
    m, d = x.shape
    n = b.shape[1]
    bm = _tile(m, PROJ_BM, LANES)
    bn = _tile(n, PROJ_BN, LANES)
    return pl.pallas_call(
        _proj_kernel,
        grid=(m // bm, n // bn),
        in_specs=[
            pl.BlockSpec((bm, d), lambda i, j: (i, 0)),
            pl.BlockSpec((None, bn, d), lambda i, j: (layer, j, 0)),
            pl.BlockSpec((1, bn), lambda i, j: (0, j)),
            pl.BlockSpec((LANES, d), lambda i, j: (0, 0)),
            pl.BlockSpec((1, LANES), lambda i, j: (0, 0)),
        ],
        out_specs=[
            pl.BlockSpec((bm, bn), lambda i, j: (i, j)),
            pl.BlockSpec((bm, LANES), lambda i, j: (i, 0)),
            pl.BlockSpec((GATE_ROWS, bm), lambda i, j: (0, i)),
        ],
        out_shape=[
            jax.ShapeDtypeStruct((m, n), BF16),
            jax.ShapeDtypeStruct((m, LANES), F32),
            jax.ShapeDtypeStruct((GATE_ROWS, m), F32),
        ],
        scratch_shapes=[pltpu.VMEM((bm, d), BF16)],
        compiler_params=_params(("arbitrary", "arbitrary")),
        name="in_projection",
    )(x, w, b, wg, bg)


def _fox_gate_kernel(g_ref, c_ref, carry_ref):
    @pl.when(pl.program_id(1) == 0)
    def _():
        carry_ref[...] = jnp.zeros_like(carry_ref)

    t = g_ref.shape[0]
    row = lax.broadcasted_iota(jnp.int32, (t, t), 0)
    col = lax.broadcasted_iota(jnp.int32, (t, t), 1)
    c = _cumsum_rows((col <= row).astype(BF16), _log_sigmoid(g_ref[...])) + carry_ref[...]
    c_ref[...] = c
    carry_ref[...] = c[t - 1:t, :]


def _fox_gate_cumsum(g, batch, seq):
    t = _tile(seq, CUM_T, LANES)
    ns = seq // t
    return pl.pallas_call(
        _fox_gate_kernel,
        grid=(batch, ns),
        in_specs=[pl.BlockSpec((t, LANES), lambda b, s: (b * ns + s, 0))],
        out_specs=pl.BlockSpec((t, LANES), lambda b, s: (b * ns + s, 0)),
        out_shape=jax.ShapeDtypeStruct((batch * seq, LANES), F32),
        scratch_shapes=[pltpu.VMEM((1, LANES), F32)],
        compiler_params=_params(("arbitrary", "arbitrary")),
        name="fox_gate_cumsum",
    )(g)


def _bias_lanes(c, ones_first):
    n = c.shape[0]
    h1, h2, h3 = (h.astype(F32) for h in _split3(c))
    lane = lax.broadcasted_iota(jnp.int32, (n, HEAD_DIM), 1)
    lo = 3 if ones_first else 0
    split = jnp.where(lane == lo, h1, jnp.where(lane == lo + 1, h2, jnp.where(lane == lo + 2, h3, 0.0)))
    ones = jnp.logical_and(lane >= 3 - lo, lane < 6 - lo)
    return jnp.where(ones, 1.0, split).astype(BF16)


def _fox_kernel(q_ref, k_ref, v_ref, c_ref, o_ref, kb_ref, m_ref, acc_ref, s_ref, *, scale, group, tq, tk):
    first_head = pl.program_id(1) * group
    per = tq // tk
    assert per == 2
    unit = (lax.broadcasted_iota(jnp.int32, (tk, HEAD_DIM), 1) == 0).astype(BF16)
    slices = [slice(g * HEAD_DIM, (g + 1) * HEAD_DIM) for g in range(group)]

    def head_lane(c, g):
        lane = lax.broadcasted_iota(jnp.int32, c.shape, 1)
        return jnp.sum(jnp.where(lane == first_head + g, c, 0.0), axis=-1, keepdims=True) * LOG2_E

    for g in range(group):
        kb_ref[g] = _bias_lanes(-head_lane(c_ref[...], g), ones_first=False)

    @pl.loop(0, q_ref.shape[0] // tq)
    def _(qi):
        q_rows = pl.ds(pl.multiple_of(qi * tq, tq), tq)
        ct = c_ref[q_rows, :]
        q_ext = []
        for g, sl in enumerate(slices):
            q = (q_ref[q_rows, sl].astype(F32) * (scale * LOG2_E)).astype(BF16)
            q_ext.append(jnp.concatenate([q, _bias_lanes(head_lane(ct, g), ones_first=True)], axis=1))

        def logits(slot, ki):
            k_rows = pl.ds(pl.multiple_of(ki * tk, tk), tk)
            for g, sl in enumerate(slices):
                k_ext = jnp.concatenate([k_ref[k_rows, sl], kb_ref[g, k_rows, :]], axis=1)
                s_ref[slot, g] = lax.dot_general(q_ext[g], k_ext, NT_DIMS, preferred_element_type=F32)

        def absorb(slot, ki, band=None):
            k_rows = pl.ds(pl.multiple_of(ki * tk, tk), tk)
            for g, sl in enumerate(slices):
                s = s_ref[slot, g]
                if band is not None:
                    row = lax.broadcasted_iota(jnp.int32, (tq, tk), 0)
                    col = lax.broadcasted_iota(jnp.int32, (tq, tk), 1)
                    s = jnp.where(col + band * tk <= row, s, -jnp.inf)
                m = m_ref[g]
                m_new = jnp.maximum(m, jnp.max(s, axis=-1, keepdims=True))
                p = jnp.exp2(s - m_new).astype(BF16)
                v_ext = jnp.concatenate([v_ref[k_rows, sl], unit], axis=1)
                acc_ref[g] = jnp.exp2(m - m_new) * acc_ref[g] + jnp.dot(p, v_ext, preferred_element_type=F32)
                m_ref[g] = m_new

        m_ref[...] = jnp.full(m_ref.shape, -jnp.inf, F32)
        acc_ref[...] = jnp.zeros_like(acc_ref)

        logits(0, 0)

        @pl.loop(0, qi)
        def _(pair):
            ki = per * pair
            logits(1, ki + 1)
            absorb(0, ki)
            logits(0, ki + 2)
            absorb(1, ki + 1)

        logits(1, per * qi + 1)
        absorb(0, per * qi, band=0)
        absorb(1, per * qi + 1, band=1)

        for g, sl in enumerate(slices):
            acc = acc_ref[g]
            o_ref[q_rows, sl] = (acc[:, :HEAD_DIM] / acc[:, HEAD_DIM:HEAD_DIM + 1]).astype(o_ref.dtype)


def _fox_attention(proj, c, batch, seq, heads):
    tq = _tile(seq, FOX_TQ, LANES)
    tk = tq // 2
    group = FOX_GROUP
    assert heads % group == 0
    hb = heads // group
    gw = group * HEAD_DIM
    return pl.pallas_call(
        functools.partial(_fox_kernel, scale=HEAD_DIM ** -0.5, group=group, tq=tq, tk=tk),
        grid=(batch, hb),
        in_specs=[
            pl.BlockSpec((seq, gw), lambda b, h: (b, h)),
            pl.BlockSpec((seq, gw), lambda b, h: (b, hb + h)),
            pl.BlockSpec((seq, gw), lambda b, h: (b, 2 * hb + h)),
            pl.BlockSpec((seq, LANES), lambda b, h: (b, 0)),
        ],
        out_specs=pl.BlockSpec((seq, gw), lambda b, h: (b, h)),
        out_shape=jax.ShapeDtypeStruct((batch * seq, heads * HEAD_DIM), BF16),
        scratch_shapes=[pltpu.VMEM((group, seq, HEAD_DIM), BF16),
                        pltpu.VMEM((group, tq, 1), F32),
                        pltpu.VMEM((group, tq, 2 * HEAD_DIM), F32),
                        pltpu.VMEM((2, group, tq, tk), F32)],
        compiler_params=_params(("arbitrary", "arbitrary")),
        name="fox_attention",
    )(proj, proj, proj, c)


def _rope(x, cos, sin_signed):
    xf = x.astype(F32)
    return xf * cos + pltpu.roll(xf, HEAD_DIM // 2, 1) * sin_signed


def _swa_kernel(sink_ref, q_ref, k_ref, v_ref, kp_ref, vp_ref, cos_ref, sin_ref, cosp_ref, sinp_ref, *rest,
                scale, group, cast_strides):
    nc = len(cast_strides)
    o_ref = rest[nc]
    _cast_step(rest[:nc], rest[nc + 1:], cast_strides)
    w = SWA_WINDOW
    first_visible = jnp.where(pl.program_id(1) == 0, w, 0)
    nblk = q_ref.shape[0] // w
    cos, sin = cos_ref[...], sin_ref[...]
    cos_q, sin_q = cos * scale, sin * scale
    cosp, sinp = cosp_ref[...], sinp_ref[...]
    row = lax.broadcasted_iota(jnp.int32, (w, 2 * w), 0)
    col = lax.broadcasted_iota(jnp.int32, (w, 2 * w), 1)
    valid = jnp.logical_and(col > row, col - w <= row)
    valid_first = jnp.logical_and(valid, col >= first_visible)
    for kv in range(SWA_KV_HEADS):
        ksl = slice(kv * HEAD_DIM, (kv + 1) * HEAD_DIM)
        k_all = jnp.concatenate([_rope(kp_ref[:, ksl], cosp, sinp), _rope(k_ref[:, ksl], cos, sin)],
                                axis=0).astype(BF16)
        v_all = jnp.concatenate([vp_ref[:, ksl], v_ref[:, ksl]], axis=0)
        for g in range(group):
            hq = kv * group + g
            qsl = slice(hq * HEAD_DIM, (hq + 1) * HEAD_DIM)
            q = _rope(q_ref[:, qsl], cos_q, sin_q).astype(BF16)
            sink = sink_ref[hq]
            for blk in range(nblk):
                qb = q[blk * w:(blk + 1) * w, :]
                kb = k_all[blk * w:(blk + 2) * w, :]
                vb = v_all[blk * w:(blk + 2) * w, :]
                s = lax.dot_general(qb, kb, NT_DIMS, preferred_element_type=F32)
                s = jnp.where(valid_first if blk == 0 else valid, s, -jnp.inf)
                m = jnp.maximum(jnp.max(s, axis=-1, keepdims=True), sink)
                p = jnp.exp(s - m)
                denom = jnp.sum(p, axis=-1, keepdims=True) + jnp.exp(sink - m)
                o = jnp.dot((p * (1.0 / denom)).astype(BF16), vb, preferred_element_type=F32)
                o_ref[blk * w:(blk + 1) * w, qsl] = o.astype(o_ref.dtype)


def _swa_attention(proj, sinks, cos, sin_signed, batch, seq, q_col, k_col, v_col, q_heads, cast_sources):
    t = _tile(seq, SWA_T, SWA_WINDOW)
    nt = seq // t
    per = t // SWA_WINDOW
    qw = q_heads * HEAD_DIM
    kvw = SWA_KV_HEADS * HEAD_DIM
    assert q_col % qw == 0 and k_col % kvw == 0 and v_col % kvw == 0

    def prev(b, i):
        return jnp.maximum((b * nt + i) * per - 1, 0)

    def prev_pos(i):
        return jnp.maximum(i * per - 1, 0)

    strides, cast_in, cast_out, cast_shapes, cast_ops = _cast_plan(cast_sources, (batch, nt))
    return pl.pallas_call(
        functools.partial(_swa_kernel, scale=HEAD_DIM ** -0.5, group=q_heads // SWA_KV_HEADS, cast_strides=strides),
        grid=(batch, nt),
        in_specs=[
            pl.BlockSpec(memory_space=pltpu.SMEM),
            pl.BlockSpec((t, qw), lambda b, i: (b * nt + i, q_col // qw)),
            pl.BlockSpec((t, kvw), lambda b, i: (b * nt + i, k_col // kvw)),
            pl.BlockSpec((t, kvw), lambda b, i: (b * nt + i, v_col // kvw)),
            pl.BlockSpec((SWA_WINDOW, kvw), lambda b, i: (prev(b, i), k_col // kvw)),
            pl.BlockSpec((SWA_WINDOW, kvw), lambda b, i: (prev(b, i), v_col // kvw)),
            pl.BlockSpec((t, HEAD_DIM), lambda b, i: (i, 0)),
            pl.BlockSpec((t, HEAD_DIM), lambda b, i: (i, 0)),
            pl.BlockSpec((SWA_WINDOW, HEAD_DIM), lambda b, i: (prev_pos(i), 0)),
            pl.BlockSpec((SWA_WINDOW, HEAD_DIM), lambda b, i: (prev_pos(i), 0)),
        ] + cast_in,
        out_specs=[pl.BlockSpec((t, qw), lambda b, i: (b * nt + i, 0))] + cast_out,
        out_shape=[jax.ShapeDtypeStruct((batch * seq, qw), BF16)] + cast_shapes,
        compiler_params=_params(("arbitrary", "arbitrary")),
        name="swa_attention",
    )(sinks, proj, proj, proj, proj, proj, cos, sin_signed, cos, sin_signed, *cast_ops)


def _cummax_rows(x):
    n = x.shape[0]
    row = lax.broadcasted_iota(jnp.int32, x.shape, 0)
    shift = 1
    while shift < n:
        x = jnp.maximum(x, jnp.where(row >= shift, pltpu.roll(x, shift, 0), -jnp.inf))
        shift *= 2
    return x


def _mlstm_kernel(q_ref, k_ref, v_ref, og_ref, g_ref, gt_ref, *rest, dk, dv, cast_strides):
    heads = MLSTM_HEADS
    nc = len(cast_strides)
    o_ref = rest[nc]
    state_ref, m_ref = rest[2 * nc + 1:]
    _cast_step(rest[:nc], rest[nc + 1:2 * nc + 1], cast_strides)

    @pl.when(pl.program_id(1) == 0)
    def _():
        state_ref[...] = jnp.zeros_like(state_ref)
        m_ref[...] = jnp.zeros_like(m_ref)

    n = q_ref.shape[0]
    g = g_ref[...]
    gt = gt_ref[...]
    row = lax.broadcasted_iota(jnp.int32, (n, n), 0)
    col = lax.broadcasted_iota(jnp.int32, (n, n), 1)
    causal = col <= row
    b_row_all = _cumsum_lanes(_log_sigmoid(gt), (row <= col).astype(BF16))
    b_all = pltpu.roll(_cumsum_rows(causal.astype(BF16), _log_sigmoid(g)), LANES - heads, 1)
    kc_all = g - b_all
    kcmax_all = _cummax_rows(kc_all)
    spread = (lax.broadcasted_iota(jnp.int32, (LANES, heads * LANES), 1) // LANES
              == lax.broadcasted_iota(jnp.int32, (LANES, heads * LANES), 0)).astype(BF16)
    b_wide, kc_wide, kcmax_wide = (
        sum(jnp.dot(term, spread, preferred_element_type=F32) for term in _split3(x))
        for x in (b_all, kc_all, kcmax_all))
    ones = jnp.ones((n, LANES), BF16)
    tiles = n // LANES

    def wide(x, reps):
        return jnp.concatenate([x] * reps, axis=1)

    for h in range(heads):
        hs = slice(h * LANES, (h + 1) * LANES)
        b_col, kc_col, kcmax_col = b_wide[:, hs], kc_wide[:, hs], kcmax_wide[:, hs]
        i_row = gt[h:h + 1, :]
        b_row = b_row_all[heads + h:heads + h + 1, :]
        b_last = b_row[:, n - 1:n]
        m_prev = m_ref[h]

        m_inter = b_col + m_prev
        m_t = jnp.maximum(m_inter, b_col + kcmax_col)
        inter = jnp.exp(m_inter - m_t)
        d = jnp.where(causal, (wide(b_col, tiles) - b_row) + i_row, -jnp.inf)
        wgt = jnp.exp(d - wide(m_t, tiles))

        qh = q_ref[:, h * dk:(h + 1) * dk]
        kf = k_ref[:, h * dk:(h + 1) * dk].astype(F32) * (dk ** -0.5)
        kh = kf.astype(BF16)
        v_ext = jnp.concatenate([v_ref[:, h * dv:(h + 1) * dv], ones], axis=1)
        state = state_ref[h]

        sm = wgt * lax.dot_general(qh, kh, NT_DIMS, preferred_element_type=F32)
        tot = wide(inter, dv // LANES + 1) * jnp.dot(qh, state.astype(BF16), preferred_element_type=F32)
        tot = tot + jnp.dot(sm.astype(BF16), v_ext, preferred_element_type=F32)
        den = jnp.maximum(jnp.abs(tot[:, dv:]), jnp.exp(-m_t))
        hid = tot[:, :dv] * wide(1.0 / den, dv // LANES)
        gate = _sigmoid(og_ref[:, h * dv:(h + 1) * dv].astype(F32))
        o_ref[:, h * dv:(h + 1) * dv] = (gate * hid).astype(o_ref.dtype)

        m_new = jnp.maximum(b_last + m_prev, jnp.max((b_last - b_row) + i_row, axis=-1, keepdims=True))
        decay = jnp.exp(b_last + m_prev - m_new)
        w_end = jnp.exp((b_last + kc_col) - m_new)
        kw = (kf * w_end).astype(BF16)
        state_ref[h] = decay * state + lax.dot_general(kw, v_ext, TN_DIMS, preferred_element_type=F32)
        m_ref[h] = m_new


def _mlstm(proj, g, gt, batch, seq, dk, dv, cast_sources):
    heads = MLSTM_HEADS
    n = _tile(seq, MLSTM_L, LANES)
    nc = seq // n
    qk, vw = heads * dk, heads * dv
    assert vw % qk == 0
    r = vw // qk
    strides, cast_in, cast_out, cast_shapes, cast_ops = _cast_plan(cast_sources, (batch, nc))
    return pl.pallas_call(
        functools.partial(_mlstm_kernel, dk=dk, dv=dv, cast_strides=strides),
        grid=(batch, nc),
        in_specs=[
            pl.BlockSpec((n, qk), lambda b, c: (b * nc + c, 0)),
            pl.BlockSpec((n, qk), lambda b, c: (b * nc + c, 1)),
            pl.BlockSpec((n, vw), lambda b, c: (b * nc + c, 2 // r)),
            pl.BlockSpec((n, vw), lambda b, c: (b * nc + c, 2 // r + 1)),
            pl.BlockSpec((n, LANES), lambda b, c: (b * nc + c, 0)),
            pl.BlockSpec((GATE_ROWS, n), lambda b, c: (0, b * nc + c)),
        ] + cast_in,
        out_specs=[pl.BlockSpec((n, vw), lambda b, c: (b * nc + c, 0))] + cast_out,
        out_shape=[jax.ShapeDtypeStruct((batch * seq, vw), BF16)] + cast_shapes,
        scratch_shapes=[pltpu.VMEM((heads, dk, dv + LANES), F32), pltpu.VMEM((heads, 1, 1), F32)],
        compiler_params=_params(("arbitrary", "arbitrary")),
        name="mlstm",
    )(proj, proj, proj, proj, g, gt, *cast_ops)


def _outproj_ln_kernel(*refs, alpha, n_in):
    h_refs, w_refs = refs[:n_in], refs[n_in:2 * n_in]
    x_ref, g_ref, b_ref, o_ref = refs[2 * n_in:]
    for r in range(o_ref.shape[0] // OUT_ROWS):
        rows = slice(r * OUT_ROWS, (r + 1) * OUT_ROWS)
        o_ref[rows, :] = sum(jnp.dot(h[rows, :], w[...], preferred_element_type=F32) for h, w in zip(h_refs, w_refs))
        o_ref[rows, :] = _layer_norm(alpha * x_ref[rows, :] + o_ref[rows, :], g_ref[...], b_ref[...])


def _out_projection_ln(hs, w, x, ln_g, ln_b, alpha):
    m, d = x.shape
    bm = _tile(m, OUT_BM, LANES)
    widths = [h.shape[1] for h in hs]
    assert all(wd == widths[0] for wd in widths) and w.shape[0] == sum(widths)
    in_specs = [pl.BlockSpec((bm, wd), lambda i: (i, 0)) for wd in widths]
    in_specs += [pl.BlockSpec((wd, d), lambda i, k=k: (k, 0)) for k, wd in enumerate(widths)]
    in_specs += [pl.BlockSpec((bm, d), lambda i: (i, 0)),
                 pl.BlockSpec((1, d), lambda i: (0, 0)),
                 pl.BlockSpec((1, d), lambda i: (0, 0))]
    return pl.pallas_call(
        functools.partial(_outproj_ln_kernel, alpha=alpha, n_in=len(hs)),
        grid=(m // bm,),
        in_specs=in_specs,
        out_specs=pl.BlockSpec((bm, d), lambda i: (i, 0)),
        out_shape=jax.ShapeDtypeStruct((m, d), F32),
        compiler_params=_params(("arbitrary",)),
        name="out_projection_ln",
    )(*hs, *([w] * len(hs)), x, ln_g, ln_b)


def _causal_conv(u, prev, p, r0):
    w0, w1, w2, b = p[r0:r0 + 1], p[r0 + 1:r0 + 2], p[r0 + 2:r0 + 3], p[r0 + 3:r0 + 4]
    body = b + w0 * pltpu.roll(u, 2, 0) + w1 * pltpu.roll(u, 1, 0) + w2 * u
    top = jnp.concatenate([prev, u[:HEAD_ROWS, :]], axis=0)
    head = b + w0 * pltpu.roll(top, 2, 0) + w1 * pltpu.roll(top, 1, 0) + w2 * top
    return body, head[SUBLANES:, :]


def _ffn_kernel(x_ref, wg_ref, wv_ref, cp_ref, wd_ref, lg_ref, lb_ref, o_ref,
                xb_ref, h0_ref, h1_ref, pg_ref, pv_ref, ug_ref, uv_ref, *, alpha, tiles_per_seq, nf):
    i = pl.program_id(0)
    j = pl.program_id(1)
    bm = x_ref.shape[0]
    h_refs = (h0_ref, h1_ref)

    def up(h_ref, first_step=False):
        kept_rows = jnp.where(i % tiles_per_seq != 0, SUBLANES, 0)
        keep = lax.broadcasted_iota(jnp.int32, (SUBLANES, FFN_CHUNK), 0) < kept_rows
        chunks = [slice(c * FFN_CHUNK, (c + 1) * FFN_CHUNK) for c in range(h_ref.shape[1] // FFN_CHUNK)]
        prev_g = [jnp.where(keep, pg_ref[j, :, cs], 0.0) for cs in chunks]
        prev_v = [jnp.where(keep, pv_ref[j, :, cs], 0.0) for cs in chunks]
        for r in range(bm // FFN_ROWS):
            rows = slice(r * FFN_ROWS, (r + 1) * FFN_ROWS)
            if first_step:
                xb_ref[rows, :] = x_ref[rows, :].astype(BF16)
            xr = xb_ref[rows, :]
            for c, cs in enumerate(chunks):
                slot = (r * len(chunks) + c) % FFN_SLOTS
                ug_ref[slot] = jnp.dot(xr, wg_ref[:, cs], preferred_element_type=F32)
                uv_ref[slot] = jnp.dot(xr, wv_ref[:, cs], preferred_element_type=F32)
                ug, uv = ug_ref[slot], uv_ref[slot]
                cp = cp_ref[j, :, cs]
                cg, cg_top = _causal_conv(ug, prev_g[c], cp, 0)
                cv, cv_top = _causal_conv(uv, prev_v[c], cp, CONV_WIDTH + 1)
                prev_g[c] = ug[FFN_ROWS - SUBLANES:, :]
                prev_v[c] = uv[FFN_ROWS - SUBLANES:, :]
                h_ref[r * FFN_ROWS:(r + 1) * FFN_ROWS, cs] = (cg * _sigmoid(cg) * cv).astype(BF16)
                h_ref[r * FFN_ROWS:r * FFN_ROWS + HEAD_ROWS, cs] = (cg_top * _sigmoid(cg_top) * cv_top).astype(BF16)
        for c, cs in enumerate(chunks):
            pg_ref[j, :, cs] = prev_g[c]
            pv_ref[j, :, cs] = prev_v[c]

    def down(h_ref):
        o_ref[...] += jnp.dot(h_ref[...], wd_ref[...], preferred_element_type=F32)

    @pl.when(j == 0)
    def _():
        o_ref[...] = jnp.zeros_like(o_ref)
        up(h_refs[0], first_step=True)

    for parity in range(2):
        @pl.when(jnp.logical_and(jnp.logical_and(j > 0, j < nf), j % 2 == parity))
        def _():
            up(h_refs[parity])
            down(h_refs[1 - parity])

    @pl.when(j == nf)
    def _():
        h_ref = h_refs[(nf - 1) % 2]
        for r in range(bm // FFN_ROWS):
            rows = slice(r * FFN_ROWS, (r + 1) * FFN_ROWS)
            o_ref[rows, :] += jnp.dot(h_ref[rows, :], wd_ref[...], preferred_element_type=F32)
            o_ref[rows, :] = _layer_norm(alpha * x_ref[rows, :] + o_ref[rows, :], lg_ref[...], lb_ref[...])


def _conv_ffn_ln(x, w_up, w_down, conv_w, conv_b, ln_g, ln_b, alpha, seq):
    m, d = x.shape
    f = w_down.shape[0]
    bm = _tile(seq, FFN_BM, 2 * SUBLANES)
    bf = _tile(f, FFN_BF, LANES)
    nf = f // bf

    conv_p = jnp.concatenate([conv_w[:, :f], conv_b[:, :f], conv_w[:, f:], conv_b[:, f:]], axis=0)
    conv_p = conv_p.reshape(2 * (CONV_WIDTH + 1), nf, bf).transpose(1, 0, 2)

    def up_blk(j):
        return jnp.minimum(j, nf - 1)

    def down_blk(j):
        return jnp.maximum(j - 1, 0)

    return pl.pallas_call(
        functools.partial(_ffn_kernel, alpha=alpha, tiles_per_seq=seq // bm, nf=nf),
        grid=(m // bm, nf + 1),
        in_specs=[
            pl.BlockSpec((bm, d), lambda i, j: (i, 0)),
            pl.BlockSpec((d, bf), lambda i, j: (0, up_blk(j))),
            pl.BlockSpec((d, bf), lambda i, j: (0, nf + up_blk(j))),
            pl.BlockSpec(conv_p.shape, lambda i, j: (0, 0, 0)),
            pl.BlockSpec((bf, d), lambda i, j: (down_blk(j), 0)),
            pl.BlockSpec((1, d), lambda i, j: (0, 0)),
            pl.BlockSpec((1, d), lambda i, j: (0, 0)),
        ],
        out_specs=pl.BlockSpec((bm, d), lambda i, j: (i, 0)),
        out_shape=jax.ShapeDtypeStruct((m, d), F32),
        scratch_shapes=[
            pltpu.VMEM((bm, d), BF16),
            pltpu.VMEM((bm, bf), BF16),
            pltpu.VMEM((bm, bf), BF16),
            pltpu.VMEM((nf, SUBLANES, bf), F32),
            pltpu.VMEM((nf, SUBLANES, bf), F32),
            pltpu.VMEM((FFN_SLOTS, FFN_ROWS, FFN_CHUNK), F32),
            pltpu.VMEM((FFN_SLOTS, FFN_ROWS, FFN_CHUNK), F32),
        ],
        compiler_params=_params(("arbitrary", "arbitrary")),
        name="conv_ffn_ln",
    )(x, w_up, w_up, conv_p, w_down, ln_g, ln_b)


def _gate_params(w, b):
    n, d = w.shape
    wg = jnp.zeros((LANES, d), BF16).at[:n, :].set(w.astype(BF16))
    bg = jnp.zeros((1, LANES), F32).at[0, :n].set(b)
    return wg, bg


def _rope_tables(seq):
    half = HEAD_DIM // 2
    inv_freq = jnp.power(ROPE_THETA, -jnp.arange(half, dtype=F32) * (2.0 / HEAD_DIM))
    ang = jnp.arange(seq, dtype=F32)[:, None] * inv_freq[None, :]
    cos, sin = jnp.cos(ang), jnp.sin(ang)
    return jnp.concatenate([cos, cos], axis=-1), jnp.concatenate([-sin, sin], axis=-1)


def kernel(x, attn_w_in, attn_b_in, attn_sinks, attn_w_out, mlstm_w_in, mlstm_b_in, mlstm_w_out, ffn_w_up,
           ffn_conv_w, ffn_conv_b, ffn_w_down, ln1_g, ln1_b, ln2_g, ln2_b):
    batch, seq, d = x.shape
    depth = ln1_g.shape[0]
    alpha = float((2 * depth) ** 0.25)
    fox_heads = d // (2 * HEAD_DIM)
    swa_heads = d // (2 * HEAD_DIM)
    fox_dim = fox_heads * HEAD_DIM
    fox_f_off = 3 * fox_dim
    dk, dv = d // (2 * MLSTM_HEADS), d // MLSTM_HEADS
    mlstm_main = 2 * MLSTM_HEADS * dk + 2 * MLSTM_HEADS * dv
    cos, sin_signed = _rope_tables(seq)

    gate_lo, gate_hi = fox_f_off, fox_f_off + fox_heads
    attn_wt, mlstm_wt = jnp.swapaxes(attn_w_in, 1, 2), jnp.swapaxes(mlstm_w_in, 1, 2)
    attn_w_gate = lax.optimization_barrier(attn_wt[:, gate_lo:gate_hi])
    mlstm_w_gate = lax.optimization_barrier(mlstm_wt[:, mlstm_main:])
    attn_w_main = jnp.concatenate([attn_wt[:, :gate_lo], attn_wt[:, gate_hi:]], axis=1).astype(BF16)
    attn_b_main = jnp.concatenate([attn_b_in[:, :gate_lo], attn_b_in[:, gate_hi:]], axis=1)
    mlstm_w_main = mlstm_wt.astype(BF16)

    h = x.reshape(batch * seq, d)
    for layer in range(depth):
        j = layer // 2
        late_w = [(ffn_w_up, layer), (ffn_w_down, layer), (attn_w_out if layer % 2 == 0 else mlstm_w_out, j)]
        if layer % 2 == 0:
            gates = _gate_params(attn_w_gate[j], attn_b_in[j][gate_lo:gate_hi])
            proj, g, gt = _in_projection(h, attn_w_main, j, attn_b_main[j][None, :], *gates)
            fox = _fox_attention(proj, _fox_gate_cumsum(g, batch, seq), batch, seq, fox_heads)
            swa_q = 3 * fox_dim
            swa_k = swa_q + swa_heads * HEAD_DIM
            swa_v = swa_k + SWA_KV_HEADS * HEAD_DIM
            swa, w_up_b, w_down_b, w_out_b = _swa_attention(proj, attn_sinks[j], cos, sin_signed, batch, seq, swa_q,
                                                            swa_k, swa_v, swa_heads, late_w)
            h = _out_projection_ln([fox, swa], w_out_b, h, ln1_g[layer][None, :], ln1_b[layer][None, :], alpha)
        else:
            gates = _gate_params(mlstm_w_gate[j], mlstm_b_in[j][mlstm_main:])
            proj, g, gt = _in_projection(h, mlstm_w_main, j, mlstm_b_in[j][None, :mlstm_main], *gates)
            mixed, w_up_b, w_down_b, w_out_b = _mlstm(proj, g, gt, batch, seq, dk, dv, late_w)
            h = _out_projection_ln([mixed], w_out_b, h, ln1_g[layer][None, :], ln1_b[layer][None, :], alpha)
        h = _conv_ffn_ln(h, w_up_b, w_down_b, ffn_conv_w[layer], ffn_conv_b[layer][None, :],
                         ln2_g[layer][None, :], ln2_b[layer][None, :], alpha, seq)
    return h.reshape(batch, seq, d)
```

```python
import functools

import jax
import jax.numpy as jnp
from jax import lax
from jax.experimental import pallas as pl
from jax.experimental.pallas import tpu as pltpu

F32 = jnp.float32
BF16 = jnp.bfloat16

HEAD_DIM = 128
SWA_KV_HEADS = 2
SWA_WINDOW = 128
ROPE_THETA = 10000.0
MLSTM_HEADS = 8
CONV_WIDTH = 3
LN_EPS = 1e-5
LOG2_E = 1.4426950408889634

LANES = 128
SUBLANES = 8
GATE_ROWS = 16
HEAD_ROWS = 16
VMEM_LIMIT_BYTES = 60 * 1024 * 1024

PROJ_BM = 1024
PROJ_BN = 1536
OUT_BM = 512
OUT_ROWS = 128
FFN_BM = 1024
FFN_BF = 512
FFN_CHUNK = 256
FFN_ROWS = 256
FFN_SLOTS = 2
FOX_TQ = 1024
FOX_GROUP = 2
SWA_T = 512
CUM_T = 512
MLSTM_L = 256

NT_DIMS = (((1,), (1,)), ((), ()))
TN_DIMS = (((0,), (0,)), ((), ()))


def _tile(n, pref, unit):
    t = min(pref, n)
    while n % t or t % unit:
        t -= unit
    assert t > 0, (n, pref, unit)
    return t


def _params(sem):
    return pltpu.CompilerParams(dimension_semantics=sem, vmem_limit_bytes=VMEM_LIMIT_BYTES)


def _log_sigmoid(x):
    return jnp.minimum(x, 0.0) - jnp.log(1.0 + jnp.exp(-jnp.abs(x)))


def _sigmoid(x):
    return 1.0 / (1.0 + jnp.exp(-x))


def _split3(x):
    h1 = x.astype(BF16)
    r1 = x - h1.astype(F32)
    h2 = r1.astype(BF16)
    h3 = (r1 - h2.astype(F32)).astype(BF16)
    return h1, h2, h3


def _cumsum_rows(tri, x):
    return sum(jnp.dot(tri, h, preferred_element_type=F32) for h in _split3(x))


def _cumsum_lanes(x, tri_t):
    return sum(jnp.dot(h, tri_t, preferred_element_type=F32) for h in _split3(x))


def _layer_norm(z, g, b):
    mu = jnp.mean(z, axis=-1, keepdims=True)
    zc = z - mu
    var = jnp.mean(zc * zc, axis=-1, keepdims=True)
    return zc * lax.rsqrt(var + LN_EPS) * g + b


def _cast_plan(sources, grid):
    nb = grid[1]
    steps = grid[0] * nb
    strides, in_specs, out_specs, out_shapes, operands = [], [], [], [], []
    for arr, layer in sources:
        _, rows, cols = arr.shape
        chunks = steps
        while steps % chunks or rows % chunks or (rows // chunks) % (2 * SUBLANES):
            chunks -= 1
        stride, r = steps // chunks, rows // chunks
        strides.append(stride)
        in_specs.append(pl.BlockSpec((None, r, cols), lambda a, b, stride=stride, layer=layer:
                                     (layer, (a * nb + b) // stride, 0)))
        out_specs.append(pl.BlockSpec((r, cols), lambda a, b, stride=stride: ((a * nb + b) // stride, 0)))
        out_shapes.append(jax.ShapeDtypeStruct((rows, cols), BF16))
        operands.append(arr)
    return tuple(strides), in_specs, out_specs, out_shapes, operands


def _cast_step(src_refs, dst_refs, strides):
    step = pl.program_id(0) * pl.num_programs(1) + pl.program_id(1)
    for src, dst, stride in zip(src_refs, dst_refs, strides):
        @pl.when(step % stride == 0)
        def _(src=src, dst=dst):
            dst[...] = src[...].astype(BF16)


def _proj_kernel(x_ref, w_ref, b_ref, wg_ref, bg_ref, o_ref, g_ref, gt_ref, xb_ref):
    @pl.when(pl.program_id(1) == 0)
    def _():
        xb = x_ref[...].astype(BF16)
        xb_ref[...] = xb
        g = lax.dot_general(xb, wg_ref[...], NT_DIMS, preferred_element_type=F32) + bg_ref[...]
        g_ref[...] = g
        gt_ref[...] = g.T[:GATE_ROWS, :]

    acc = lax.dot_general(xb_ref[...], w_ref[...], NT_DIMS, preferred_element_type=F32)
    o_ref[...] = (acc + b_ref[...]).astype(o_ref.dtype)


def _in_projection(x, w, layer, b, wg, bg):
    m, d = x.shape
    n = b.shape[1]
    bm = _tile(m, PROJ_BM, LANES)
    bn = _tile(n, PROJ_BN, LANES)
    return pl.pallas_call(
        _proj_kernel,
        grid=(m // bm, n // bn),
        in_specs=[
            pl.BlockSpec((bm, d), lambda i, j: (i, 0)),
            pl.BlockSpec((None, bn, d), lambda i, j: (layer, j, 0)),
            pl.BlockSpec((1, bn), lambda i, j: (0, j)),
            pl.BlockSpec((LANES, d), lambda i, j: (0, 0)),
            pl.BlockSpec((1, LANES), lambda i, j: (0, 0)),
        ],
        out_specs=[
            pl.BlockSpec((bm, bn), lambda i, j: (i, j)),
            pl.BlockSpec((bm, LANES), lambda i, j: (i, 0)),
            pl.BlockSpec((GATE_ROWS, bm), lambda i, j: (0, i)),
        ],
        out_shape=[
            jax.ShapeDtypeStruct((m, n), BF16),
            jax.ShapeDtypeStruct((m, LANES), F32),
            jax.ShapeDtypeStruct((GATE_ROWS, m), F32),
        ],
        scratch_shapes=[pltpu.VMEM((bm, d), BF16)],
        compiler_params=_params(("arbitrary", "arbitrary")),
        name="in_projection",
    )(x, w, b, wg, bg)


def _fox_gate_kernel(g_ref, c_ref, carry_ref):
    @pl.when(pl.program_id(1) == 0)
    def _():
        carry_ref[...] = jnp.zeros_like(carry_ref)

    t = g_ref.shape[0]
    row = lax.broadcasted_iota(jnp.int32, (t, t), 0)
    col = lax.broadcasted_iota(jnp.int32, (t, t), 1)
    c = _cumsum_rows((col <= row).astype(BF16), _log_sigmoid(g_ref[...])) + carry_ref[...]
    c_ref[...] = c
    carry_ref[...] = c[t - 1:t, :]


def _fox_gate_cumsum(g, batch, seq):
    t = _tile(seq, CUM_T, LANES)
    ns = seq // t
    return pl.pallas_call(
        _fox_gate_kernel,
        grid=(batch, ns),
        in_specs=[pl.BlockSpec((t, LANES), lambda b, s: (b * ns + s, 0))],
        out_specs=pl.BlockSpec((t, LANES), lambda b, s: (b * ns + s, 0)),
        out_shape=jax.ShapeDtypeStruct((batch * seq, LANES), F32),
        scratch_shapes=[pltpu.VMEM((1, LANES), F32)],
        compiler_params=_params(("arbitrary", "arbitrary")),
        name="fox_gate_cumsum",
    )(g)


def _bias_lanes(c, ones_first):
    n = c.shape[0]
    terms = [h.astype(F32) for h in _split3(c)]
    k = len(terms)
    lane = lax.broadcasted_iota(jnp.int32, (n, HEAD_DIM), 1)
    first_split, first_one = (k, 0) if ones_first else (0, k)
    out = jnp.where(jnp.logical_and(lane >= first_one, lane < first_one + k), 1.0, 0.0)
    for i, term in enumerate(terms):
        out = jnp.where(lane == first_split + i, term, out)
    return out.astype(BF16)


def _fox_kernel(q_ref, k_ref, v_ref, c_ref, o_ref, kb_ref, m_ref, acc_ref, s_ref, *, scale, group, tq, tk):
    first_head = pl.program_id(1) * group
    per = tq // tk
    assert per == 2
    unit = (lax.broadcasted_iota(jnp.int32, (tk, HEAD_DIM), 1) == 0).astype(BF16)
    slices = [slice(g * HEAD_DIM, (g + 1) * HEAD_DIM) for g in range(group)]

    def head_lane(c, g):
        lane = lax.broadcasted_iota(jnp.int32, c.shape, 1)
        return jnp.sum(jnp.where(lane == first_head + g, c, 0.0), axis=-1, keepdims=True) * LOG2_E

    for g in range(group):
        kb_ref[g] = _bias_lanes(-head_lane(c_ref[...], g), ones_first=False)

    @pl.loop(0, q_ref.shape[0] // tq)
    def _(qi):
        q_rows = pl.ds(pl.multiple_of(qi * tq, tq), tq)
        ct = c_ref[q_rows, :]
        q_ext = []
        for g, sl in enumerate(slices):
            q = (q_ref[q_rows, sl].astype(F32) * (scale * LOG2_E)).astype(BF16)
            q_ext.append(jnp.concatenate([q, _bias_lanes(head_lane(ct, g), ones_first=True)], axis=1))

        def logits(slot, ki):
            k_rows = pl.ds(pl.multiple_of(ki * tk, tk), tk)
            for g, sl in enumerate(slices):
                k_ext = jnp.concatenate([k_ref[k_rows, sl], kb_ref[g, k_rows, :]], axis=1)
                s_ref[slot, g] = lax.dot_general(q_ext[g], k_ext, NT_DIMS, preferred_element_type=F32)

        def absorb(slot, ki, band=None):
            k_rows = pl.ds(pl.multiple_of(ki * tk, tk), tk)
            for g, sl in enumerate(slices):
                s = s_ref[slot, g]
                if band is not None:
                    row = lax.broadcasted_iota(jnp.int32, (tq, tk), 0)
                    col = lax.broadcasted_iota(jnp.int32, (tq, tk), 1)
                    s = jnp.where(col + band * tk <= row, s, -jnp.inf)
                m = m_ref[g]
                m_new = jnp.maximum(m, jnp.max(s, axis=-1, keepdims=True))
                p = jnp.exp2(s - m_new).astype(BF16)
                v_ext = jnp.concatenate([v_ref[k_rows, sl], unit], axis=1)
                acc_ref[g] = jnp.exp2(m - m_new) * acc_ref[g] + jnp.dot(p, v_ext, preferred_element_type=F32)
                m_ref[g] = m_new

        m_ref[...] = jnp.full(m_ref.shape, -jnp.inf, F32)
        acc_ref[...] = jnp.zeros_like(acc_ref)

        logits(0, 0)

        @pl.loop(0, qi)
        def _(pair):
            ki = per * pair
            logits(1, ki + 1)
            absorb(0, ki)
            logits(0, ki + 2)
            absorb(1, ki + 1)

        logits(1, per * qi + 1)
        absorb(0, per * qi, band=0)
        absorb(1, per * qi + 1, band=1)

        for g, sl in enumerate(slices):
            acc = acc_ref[g]
            o_ref[q_rows, sl] = (acc[:, :HEAD_DIM] / acc[:, HEAD_DIM:HEAD_DIM + 1]).astype(o_ref.dtype)


def _fox_attention(proj, c, batch, seq, heads):
    tq = _tile(seq, FOX_TQ, LANES)
    tk = tq // 2
    group = FOX_GROUP
    assert heads % group == 0
    hb = heads // group
    gw = group * HEAD_DIM
    return pl.pallas_call(
        functools.partial(_fox_kernel, scale=HEAD_DIM ** -0.5, group=group, tq=tq, tk=tk),
        grid=(batch, hb),
        in_specs=[
            pl.BlockSpec((seq, gw), lambda b, h: (b, h)),
            pl.BlockSpec((seq, gw), lambda b, h: (b, hb + h)),
            pl.BlockSpec((seq, gw), lambda b, h: (b, 2 * hb + h)),
            pl.BlockSpec((seq, LANES), lambda b, h: (b, 0)),
        ],
        out_specs=pl.BlockSpec((seq, gw), lambda b, h: (b, h)),
        out_shape=jax.ShapeDtypeStruct((batch * seq, heads * HEAD_DIM), BF16),
        scratch_shapes=[pltpu.VMEM((group, seq, HEAD_DIM), BF16),
                        pltpu.VMEM((group, tq, 1), F32),
                        pltpu.VMEM((group, tq, 2 * HEAD_DIM), F32),
                        pltpu.VMEM((2, group, tq, tk), F32)],
        compiler_params=_params(("arbitrary", "arbitrary")),
        name="fox_attention",
    )(proj, proj, proj, c)


def _rope(x, cos, sin_signed):
    xf = x.astype(F32)
    return xf * cos + pltpu.roll(xf, HEAD_DIM // 2, 1) * sin_signed


def _swa_kernel(sink_ref, q_ref, k_ref, v_ref, kp_ref, vp_ref, cos_ref, sin_ref, cosp_ref, sinp_ref, *rest,
                scale, group, cast_strides):
    nc = len(cast_strides)
    o_ref = rest[nc]
    _cast_step(rest[:nc], rest[nc + 1:], cast_strides)
    w = SWA_WINDOW
    first_visible = jnp.where(pl.program_id(1) == 0, w, 0)
    nblk = q_ref.shape[0] // w
    cos, sin = cos_ref[...], sin_ref[...]
    cos_q, sin_q = cos * scale, sin * scale
    cosp, sinp = cosp_ref[...], sinp_ref[...]
    row = lax.broadcasted_iota(jnp.int32, (w, 2 * w), 0)
    col = lax.broadcasted_iota(jnp.int32, (w, 2 * w), 1)
    valid = jnp.logical_and(col > row, col - w <= row)
    valid_first = jnp.logical_and(valid, col >= first_visible)
    for kv in range(SWA_KV_HEADS):
        ksl = slice(kv * HEAD_DIM, (kv + 1) * HEAD_DIM)
        k_all = jnp.concatenate([_rope(kp_ref[:, ksl], cosp, sinp), _rope(k_ref[:, ksl], cos, sin)],
                                axis=0).astype(BF16)
        v_all = jnp.concatenate([vp_ref[:, ksl], v_ref[:, ksl]], axis=0)
        for g in range(group):
            hq = kv * group + g
            qsl = slice(hq * HEAD_DIM, (hq + 1) * HEAD_DIM)
            q = _rope(q_ref[:, qsl], cos_q, sin_q).astype(BF16)
            sink = sink_ref[hq]
            for blk in range(nblk):
                qb = q[blk * w:(blk + 1) * w, :]
                kb = k_all[blk * w:(blk + 2) * w, :]
                vb = v_all[blk * w:(blk + 2) * w, :]
                s = lax.dot_general(qb, kb, NT_DIMS, preferred_element_type=F32)
                s = jnp.where(valid_first if blk == 0 else valid, s, -jnp.inf)
                m = jnp.maximum(jnp.max(s, axis=-1, keepdims=True), sink)
                p = jnp.exp(s - m)
                denom = jnp.sum(p, axis=-1, keepdims=True) + jnp.exp(sink - m)
                o = jnp.dot((p * (1.0 / denom)).astype(BF16), vb, preferred_element_type=F32)
                o_ref[blk * w:(blk + 1) * w, qsl] = o.astype(o_ref.dtype)


def _swa_attention(proj, sinks, cos, sin_signed, batch, seq, q_col, k_col, v_col, q_heads, cast_sources):
    t = _tile(seq, SWA_T, SWA_WINDOW)
    nt = seq // t
    per = t // SWA_WINDOW
    qw = q_heads * HEAD_DIM
    kvw = SWA_KV_HEADS * HEAD_DIM
    assert q_col % qw == 0 and k_col % kvw == 0 and v_col % kvw == 0

    def prev(b, i):
        return jnp.maximum((b * nt + i) * per - 1, 0)

    def prev_pos(i):
        return jnp.maximum(i * per - 1, 0)

    strides, cast_in, cast_out, cast_shapes, cast_ops = _cast_plan(cast_sources, (batch, nt))
    return pl.pallas_call(
        functools.partial(_swa_kernel, scale=HEAD_DIM ** -0.5, group=q_heads // SWA_KV_HEADS, cast_strides=strides),
        grid=(batch, nt),
        in_specs=[
            pl.BlockSpec(memory_space=pltpu.SMEM),
            pl.BlockSpec((t, qw), lambda b, i: (b * nt + i, q_col // qw)),
            pl.BlockSpec((t, kvw), lambda b, i: (b * nt + i, k_col // kvw)),
            pl.BlockSpec((t, kvw), lambda b, i: (b * nt + i, v_col // kvw)),
            pl.BlockSpec((SWA_WINDOW, kvw), lambda b, i: (prev(b, i), k_col // kvw)),
            pl.BlockSpec((SWA_WINDOW, kvw), lambda b, i: (prev(b, i), v_col // kvw)),
            pl.BlockSpec((t, HEAD_DIM), lambda b, i: (i, 0)),
            pl.BlockSpec((t, HEAD_DIM), lambda b, i: (i, 0)),
            pl.BlockSpec((SWA_WINDOW, HEAD_DIM), lambda b, i: (prev_pos(i), 0)),
            pl.BlockSpec((SWA_WINDOW, HEAD_DIM), lambda b, i: (prev_pos(i), 0)),
        ] + cast_in,
        out_specs=[pl.BlockSpec((t, qw), lambda b, i: (b * nt + i, 0))] + cast_out,
        out_shape=[jax.ShapeDtypeStruct((batch * seq, qw), BF16)] + cast_shapes,
        compiler_params=_params(("arbitrary", "arbitrary")),
        name="swa_attention",
    )(sinks, proj, proj, proj, proj, proj, cos, sin_signed, cos, sin_signed, *cast_ops)


def _cummax_rows(x):
    n = x.shape[0]
    row = lax.broadcasted_iota(jnp.int32, x.shape, 0)
    shift = 1
    while shift < n:
        x = jnp.maximum(x, jnp.where(row >= shift, pltpu.roll(x, shift, 0), -jnp.inf))
        shift *= 2
    return x


def _mlstm_kernel(q_ref, k_ref, v_ref, og_ref, g_ref, gt_ref, *rest, dk, dv, cast_strides):
    heads = MLSTM_HEADS
    nc = len(cast_strides)
    o_ref = rest[nc]
    state_ref, m_ref = rest[2 * nc + 1:]
    _cast_step(rest[:nc], rest[nc + 1:2 * nc + 1], cast_strides)

    @pl.when(pl.program_id(1) == 0)
    def _():
        state_ref[...] = jnp.zeros_like(state_ref)
        m_ref[...] = jnp.zeros_like(m_ref)

    n = q_ref.shape[0]
    g = g_ref[...]
    gt = gt_ref[...]
    row = lax.broadcasted_iota(jnp.int32, (n, n), 0)
    col = lax.broadcasted_iota(jnp.int32, (n, n), 1)
    causal = col <= row
    b_row_all = _cumsum_lanes(_log_sigmoid(gt), (row <= col).astype(BF16))
    b_all = pltpu.roll(_cumsum_rows(causal.astype(BF16), _log_sigmoid(g)), LANES - heads, 1)
    kc_all = g - b_all
    kcmax_all = _cummax_rows(kc_all)
    spread = (lax.broadcasted_iota(jnp.int32, (LANES, heads * LANES), 1) // LANES
              == lax.broadcasted_iota(jnp.int32, (LANES, heads * LANES), 0)).astype(BF16)
    b_wide, kc_wide, kcmax_wide = (
        sum(jnp.dot(term, spread, preferred_element_type=F32) for term in _split3(x))
        for x in (b_all, kc_all, kcmax_all))
    ones = jnp.ones((n, LANES), BF16)
    tiles = n // LANES

    def wide(x, reps):
        return jnp.concatenate([x] * reps, axis=1)

    for h in range(heads):
        hs = slice(h * LANES, (h + 1) * LANES)
        b_col, kc_col, kcmax_col = b_wide[:, hs], kc_wide[:, hs], kcmax_wide[:, hs]
        i_row = gt[h:h + 1, :]
        b_row = b_row_all[heads + h:heads + h + 1, :]
        b_last = b_row[:, n - 1:n]
        m_prev = m_ref[h]

        m_inter = b_col + m_prev
        m_t = jnp.maximum(m_inter, b_col + kcmax_col)
        inter = jnp.exp(m_inter - m_t)
        d = jnp.where(causal, (wide(b_col, tiles) - b_row) + i_row, -jnp.inf)
        wgt = jnp.exp(d - wide(m_t, tiles))

        qh = q_ref[:, h * dk:(h + 1) * dk]
        kf = k_ref[:, h * dk:(h + 1) * dk].astype(F32) * (dk ** -0.5)
        kh = kf.astype(BF16)
        v_ext = jnp.concatenate([v_ref[:, h * dv:(h + 1) * dv], ones], axis=1)
        state = state_ref[h]

        sm = wgt * lax.dot_general(qh, kh, NT_DIMS, preferred_element_type=F32)
        tot = wide(inter, dv // LANES + 1) * jnp.dot(qh, state.astype(BF16), preferred_element_type=F32)
        tot = tot + jnp.dot(sm.astype(BF16), v_ext, preferred_element_type=F32)
        den = jnp.maximum(jnp.abs(tot[:, dv:]), jnp.exp(-m_t))
        hid = tot[:, :dv] * wide(1.0 / den, dv // LANES)
        gate = _sigmoid(og_ref[:, h * dv:(h + 1) * dv].astype(F32))
        o_ref[:, h * dv:(h + 1) * dv] = (gate * hid).astype(o_ref.dtype)

        m_new = jnp.maximum(b_last + m_prev, jnp.max((b_last - b_row) + i_row, axis=-1, keepdims=True))
        decay = jnp.exp(b_last + m_prev - m_new)
        w_end = jnp.exp((b_last + kc_col) - m_new)
        kw = (kf * w_end).astype(BF16)
        state_ref[h] = decay * state + lax.dot_general(kw, v_ext, TN_DIMS, preferred_element_type=F32)
        m_ref[h] = m_new


def _mlstm(proj, g, gt, batch, seq, dk, dv, cast_sources):
    heads = MLSTM_HEADS
    n = _tile(seq, MLSTM_L, LANES)
    nc = seq // n
    qk, vw = heads * dk, heads * dv
    assert vw % qk == 0
    r = vw // qk
    strides, cast_in, cast_out, cast_shapes, cast_ops = _cast_plan(cast_sources, (batch, nc))
    return pl.pallas_call(
        functools.partial(_mlstm_kernel, dk=dk, dv=dv, cast_strides=strides),
        grid=(batch, nc),
        in_specs=[
            pl.BlockSpec((n, qk), lambda b, c: (b * nc + c, 0)),
            pl.BlockSpec((n, qk), lambda b, c: (b * nc + c, 1)),
            pl.BlockSpec((n, vw), lambda b, c: (b * nc + c, 2 // r)),
            pl.BlockSpec((n, vw), lambda b, c: (b * nc + c, 2 // r + 1)),
            pl.BlockSpec((n, LANES), lambda b, c: (b * nc + c, 0)),
            pl.BlockSpec((GATE_ROWS, n), lambda b, c: (0, b * nc + c)),
        ] + cast_in,
        out_specs=[pl.BlockSpec((n, vw), lambda b, c: (b * nc + c, 0))] + cast_out,
        out_shape=[jax.ShapeDtypeStruct((batch * seq, vw), BF16)] + cast_shapes,
        scratch_shapes=[pltpu.VMEM((heads, dk, dv + LANES), F32), pltpu.VMEM((heads, 1, 1), F32)],
        compiler_params=_params(("arbitrary", "arbitrary")),
        name="mlstm",
    )(proj, proj, proj, proj, g, gt, *cast_ops)


def _outproj_ln_kernel(*refs, alpha, n_in):
    h_refs, w_refs = refs[:n_in], refs[n_in:2 * n_in]
    x_ref, g_ref, b_ref, o_ref = refs[2 * n_in:]
    for r in range(o_ref.shape[0] // OUT_ROWS):
        rows = slice(r * OUT_ROWS, (r + 1) * OUT_ROWS)
        o_ref[rows, :] = sum(jnp.dot(h[rows, :], w[...], preferred_element_type=F32) for h, w in zip(h_refs, w_refs))
        o_ref[rows, :] = _layer_norm(alpha * x_ref[rows, :] + o_ref[rows, :], g_ref[...], b_ref[...])


def _out_projection_ln(hs, w, x, ln_g, ln_b, alpha):
    m, d = x.shape
    bm = _tile(m, OUT_BM, LANES)
    widths = [h.shape[1] for h in hs]
    assert all(wd == widths[0] for wd in widths) and w.shape[0] == sum(widths)
    in_specs = [pl.BlockSpec((bm, wd), lambda i: (i, 0)) for wd in widths]
    in_specs += [pl.BlockSpec((wd, d), lambda i, k=k: (k, 0)) for k, wd in enumerate(widths)]
    in_specs += [pl.BlockSpec((bm, d), lambda i: (i, 0)),
                 pl.BlockSpec((1, d), lambda i: (0, 0)),
                 pl.BlockSpec((1, d), lambda i: (0, 0))]
    return pl.pallas_call(
        functools.partial(_outproj_ln_kernel, alpha=alpha, n_in=len(hs)),
        grid=(m // bm,),
        in_specs=in_specs,
        out_specs=pl.BlockSpec((bm, d), lambda i: (i, 0)),
        out_shape=jax.ShapeDtypeStruct((m, d), F32),
        compiler_params=_params(("arbitrary",)),
        name="out_projection_ln",
    )(*hs, *([w] * len(hs)), x, ln_g, ln_b)


def _causal_conv(u, prev, p, r0):
    w0, w1, w2, b = p[r0:r0 + 1], p[r0 + 1:r0 + 2], p[r0 + 2:r0 + 3], p[r0 + 3:r0 + 4]
    body = b + w0 * pltpu.roll(u, 2, 0) + w1 * pltpu.roll(u, 1, 0) + w2 * u
    top = jnp.concatenate([prev, u[:HEAD_ROWS, :]], axis=0)
    head = b + w0 * pltpu.roll(top, 2, 0) + w1 * pltpu.roll(top, 1, 0) + w2 * top
    return body, head[SUBLANES:, :]


def _ffn_kernel(x_ref, wg_ref, wv_ref, cp_ref, wd_ref, lg_ref, lb_ref, o_ref,
                xb_ref, h0_ref, h1_ref, pg_ref, pv_ref, ug_ref, uv_ref, *, alpha, tiles_per_seq, nf):
    i = pl.program_id(0)
    j = pl.program_id(1)
    bm = x_ref.shape[0]
    h_refs = (h0_ref, h1_ref)

    def up(h_ref, first_step=False):
        kept_rows = jnp.where(i % tiles_per_seq != 0, SUBLANES, 0)
        keep = lax.broadcasted_iota(jnp.int32, (SUBLANES, FFN_CHUNK), 0) < kept_rows
        chunks = [slice(c * FFN_CHUNK, (c + 1) * FFN_CHUNK) for c in range(h_ref.shape[1] // FFN_CHUNK)]
        prev_g = [jnp.where(keep, pg_ref[j, :, cs], 0.0) for cs in chunks]
        prev_v = [jnp.where(keep, pv_ref[j, :, cs], 0.0) for cs in chunks]
        for r in range(bm // FFN_ROWS):
            rows = slice(r * FFN_ROWS, (r + 1) * FFN_ROWS)
            if first_step:
                xb_ref[rows, :] = x_ref[rows, :].astype(BF16)
            xr = xb_ref[rows, :]
            for c, cs in enumerate(chunks):
                slot = (r * len(chunks) + c) % FFN_SLOTS
                ug_ref[slot] = jnp.dot(xr, wg_ref[:, cs], preferred_element_type=F32)
                uv_ref[slot] = jnp.dot(xr, wv_ref[:, cs], preferred_element_type=F32)
                ug, uv = ug_ref[slot], uv_ref[slot]
                cp = cp_ref[j, :, cs]
                cg, cg_top = _causal_conv(ug, prev_g[c], cp, 0)
                cv, cv_top = _causal_conv(uv, prev_v[c], cp, CONV_WIDTH + 1)
                prev_g[c] = ug[FFN_ROWS - SUBLANES:, :]
                prev_v[c] = uv[FFN_ROWS - SUBLANES:, :]
                h_ref[r * FFN_ROWS:(r + 1) * FFN_ROWS, cs] = (cg * _sigmoid(cg) * cv).astype(BF16)
                h_ref[r * FFN_ROWS:r * FFN_ROWS + HEAD_ROWS, cs] = (cg_top * _sigmoid(cg_top) * cv_top).astype(BF16)
        for c, cs in enumerate(chunks):
            pg_ref[j, :, cs] = prev_g[c]
            pv_ref[j, :, cs] = prev_v[c]

    def down(h_ref):
        o_ref[...] += jnp.dot(h_ref[...], wd_ref[...], preferred_element_type=F32)

    @pl.when(j == 0)
    def _():
        o_ref[...] = jnp.zeros_like(o_ref)
        up(h_refs[0], first_step=True)

    for parity in range(2):
        @pl.when(jnp.logical_and(jnp.logical_and(j > 0, j < nf), j % 2 == parity))
        def _():
            up(h_refs[parity])
            down(h_refs[1 - parity])

    @pl.when(j == nf)
    def _():
        h_ref = h_refs[(nf - 1) % 2]
        for r in range(bm // FFN_ROWS):
            rows = slice(r * FFN_ROWS, (r + 1) * FFN_ROWS)
            o_ref[rows, :] += jnp.dot(h_ref[rows, :], wd_ref[...], preferred_element_type=F32)
            o_ref[rows, :] = _layer_norm(alpha * x_ref[rows, :] + o_ref[rows, :], lg_ref[...], lb_ref[...])


def _conv_ffn_ln(x, w_up, w_down, conv_w, conv_b, ln_g, ln_b, alpha, seq):
    m, d = x.shape
    f = w_down.shape[0]
    bm = _tile(seq, FFN_BM, 2 * SUBLANES)
    bf = _tile(f, FFN_BF, LANES)
    nf = f // bf

    conv_p = jnp.concatenate([conv_w[:, :f], conv_b[:, :f], conv_w[:, f:], conv_b[:, f:]], axis=0)
    conv_p = conv_p.reshape(2 * (CONV_WIDTH + 1), nf, bf).transpose(1, 0, 2)

    def up_blk(j):
        return jnp.minimum(j, nf - 1)

    def down_blk(j):
        return jnp.maximum(j - 1, 0)

    return pl.pallas_call(
        functools.partial(_ffn_kernel, alpha=alpha, tiles_per_seq=seq // bm, nf=nf),
        grid=(m // bm, nf + 1),
        in_specs=[
            pl.BlockSpec((bm, d), lambda i, j: (i, 0)),
            pl.BlockSpec((d, bf), lambda i, j: (0, up_blk(j))),
            pl.BlockSpec((d, bf), lambda i, j: (0, nf + up_blk(j))),
            pl.BlockSpec(conv_p.shape, lambda i, j: (0, 0, 0)),
            pl.BlockSpec((bf, d), lambda i, j: (down_blk(j), 0)),
            pl.BlockSpec((1, d), lambda i, j: (0, 0)),
            pl.BlockSpec((1, d), lambda i, j: (0, 0)),
        ],
        out_specs=pl.BlockSpec((bm, d), lambda i, j: (i, 0)),
        out_shape=jax.ShapeDtypeStruct((m, d), F32),
        scratch_shapes=[
            pltpu.VMEM((bm, d), BF16),
            pltpu.VMEM((bm, bf), BF16),
            pltpu.VMEM((bm, bf), BF16),
            pltpu.VMEM((nf, SUBLANES, bf), F32),
            pltpu.VMEM((nf, SUBLANES, bf), F32),
            pltpu.VMEM((FFN_SLOTS, FFN_ROWS, FFN_CHUNK), F32),
            pltpu.VMEM((FFN_SLOTS, FFN_ROWS, FFN_CHUNK), F32),
        ],
        compiler_params=_params(("arbitrary", "arbitrary")),
        name="conv_ffn_ln",
    )(x, w_up, w_up, conv_p, w_down, ln_g, ln_b)


def _gate_params(w, b):
    n, d = w.shape
    wg = jnp.zeros((LANES, d), BF16).at[:n, :].set(w.astype(BF16))
    bg = jnp.zeros((1, LANES), F32).at[0, :n].set(b)
    return wg, bg


def _rope_tables(seq):
    half = HEAD_DIM // 2
    inv_freq = jnp.power(ROPE_THETA, -jnp.arange(half, dtype=F32) * (2.0 / HEAD_DIM))
    ang = jnp.arange(seq, dtype=F32)[:, None] * inv_freq[None, :]
    cos, sin = jnp.cos(ang), jnp.sin(ang)
    return jnp.concatenate([cos, cos], axis=-1), jnp.concatenate([-sin, sin], axis=-1)


def kernel(x, attn_w_in, attn_b_in, attn_sinks, attn_w_out, mlstm_w_in, mlstm_b_in, mlstm_w_out, ffn_w_up,
           ffn_conv_w, ffn_conv_b, ffn_w_down, ln1_g, ln1_b, ln2_g, ln2_b):
    batch, seq, d = x.shape
    depth = ln1_g.shape[0]
    alpha = float((2 * depth) ** 0.25)
    fox_heads = d // (2 * HEAD_DIM)
    swa_heads = d // (2 * HEAD_DIM)
    fox_dim = fox_heads * HEAD_DIM
    fox_f_off = 3 * fox_dim
    dk, dv = d // (2 * MLSTM_HEADS), d // MLSTM_HEADS
    mlstm_main = 2 * MLSTM_HEADS * dk + 2 * MLSTM_HEADS * dv
    cos, sin_signed = _rope_tables(seq)

    gate_lo, gate_hi = fox_f_off, fox_f_off + fox_heads
    attn_wt, mlstm_wt = jnp.swapaxes(attn_w_in, 1, 2), jnp.swapaxes(mlstm_w_in, 1, 2)
    attn_w_gate = lax.optimization_barrier(attn_wt[:, gate_lo:gate_hi])
    mlstm_w_gate = lax.optimization_barrier(mlstm_wt[:, mlstm_main:])
    attn_w_main = jnp.concatenate([attn_wt[:, :gate_lo], attn_wt[:, gate_hi:]], axis=1).astype(BF16)
    attn_b_main = jnp.concatenate([attn_b_in[:, :gate_lo], attn_b_in[:, gate_hi:]], axis=1)
    mlstm_w_main = mlstm_wt.astype(BF16)

    h = x.reshape(batch * seq, d)
    for layer in range(depth):
        j = layer // 2
        late_w = [(ffn_w_up, layer), (ffn_w_down, layer), (attn_w_out if layer % 2 == 0 else mlstm_w_out, j)]
        if layer % 2 == 0:
            gates = _gate_params(attn_w_gate[j], attn_b_in[j][gate_lo:gate_hi])
            proj, g, gt = _in_projection(h, attn_w_main, j, attn_b_main[j][None, :], *gates)
            fox = _fox_attention(proj, _fox_gate_cumsum(g, batch, seq), batch, seq, fox_heads)
            swa_q = 3 * fox_dim
            swa_k = swa_q + swa_heads * HEAD_DIM
            swa_v = swa_k + SWA_KV_HEADS * HEAD_DIM
            swa, w_up_b, w_down_b, w_out_b = _swa_attention(proj, attn_sinks[j], cos, sin_signed, batch, seq, swa_q,
                                                            swa_k, swa_v, swa_heads, late_w)
            h = _out_projection_ln([fox, swa], w_out_b, h, ln1_g[layer][None, :], ln1_b[layer][None, :], alpha)
        else:
            gates = _gate_params(mlstm_w_gate[j], mlstm_b_in[j][mlstm_main:])
            proj, g, gt = _in_projection(h, mlstm_w_main, j, mlstm_b_in[j][None, :mlstm_main], *gates)
            mixed, w_up_b, w_down_b, w_out_b = _mlstm(proj, g, gt, batch, seq, dk, dv, late_w)
            h = _out_projection_ln([mixed], w_out_b, h, ln1_g[layer][None, :], ln1_b[layer][None, :], alpha)
        h = _conv_ffn_ln(h, w_up_b, w_down_b, ffn_conv_w[layer], ffn_conv_b[layer][None, :],
                         ln2_g[layer][None, :], ln2_b[layer][None, :], alpha, seq)
    return h.reshape(batch, seq, d)
```

```python
import functools

import jax
import jax.numpy as jnp
from jax import lax
from jax.experimental import pallas as pl
from jax.experimental.pallas import tpu as pltpu

F32 = jnp.float32
BF16 = jnp.bfloat16

HEAD_DIM = 128
SWA_KV_HEADS = 2
SWA_WINDOW = 128
ROPE_THETA = 10000.0
MLSTM_HEADS = 8
CONV_WIDTH = 3
LN_EPS = 1e-5
LOG2_E = 1.4426950408889634

LANES = 128
SUBLANES = 8
GATE_ROWS = 16
HEAD_ROWS = 16
VMEM_LIMIT_BYTES = 60 * 1024 * 1024

PROJ_BM = 1024
PROJ_BN = 1536
OUT_BM = 512
OUT_ROWS = 128
FFN_BM = 1024
FFN_BF = 512
FFN_CHUNK = 256
FFN_ROWS = 256
FFN_SLOTS = 2
FOX_TQ = 1024
FOX_GROUP = 2
SWA_T = 512
CUM_T = 512
MLSTM_L = 256

NT_DIMS = (((1,), (1,)), ((), ()))
TN_DIMS = (((0,), (0,)), ((), ()))


def _tile(n, pref, unit):
    t = min(pref, n)
    while n % t or t % unit:
        t -= unit
    assert t > 0, (n, pref, unit)
    return t


def _params(sem):
    return pltpu.CompilerParams(dimension_semantics=sem, vmem_limit_bytes=VMEM_LIMIT_BYTES)


def _log_sigmoid(x):
    return jnp.minimum(x, 0.0) - jnp.log(1.0 + jnp.exp(-jnp.abs(x)))


def _sigmoid(x):
    return 1.0 / (1.0 + jnp.exp(-x))


def _split3(x):
    h1 = x.astype(BF16)
    r1 = x - h1.astype(F32)
    h2 = r1.astype(BF16)
    h3 = (r1 - h2.astype(F32)).astype(BF16)
    return h1, h2, h3


def _cumsum_rows(tri, x):
    return sum(jnp.dot(tri, h, preferred_element_type=F32) for h in _split3(x))


def _cumsum_lanes(x, tri_t):
    return sum(jnp.dot(h, tri_t, preferred_element_type=F32) for h in _split3(x))


def _layer_norm(z, g, b):
    mu = jnp.mean(z, axis=-1, keepdims=True)
    zc = z - mu
    var = jnp.mean(zc * zc, axis=-1, keepdims=True)
    return zc * lax.rsqrt(var + LN_EPS) * g + b


def _cast_plan(sources, grid):
    nb = grid[1]
    steps = grid[0] * nb
    strides, in_specs, out_specs, out_shapes, operands = [], [], [], [], []
    for arr, layer, rows in sources:
        cols = arr.shape[2]
        chunks = steps
        while steps % chunks or rows % chunks or (rows // chunks) % (2 * SUBLANES):
            chunks -= 1
        stride, r = steps // chunks, rows // chunks
        strides.append(stride)
        in_specs.append(pl.BlockSpec((None, r, cols), lambda a, b, stride=stride, layer=layer:
                                     (layer, (a * nb + b) // stride, 0)))
        out_specs.append(pl.BlockSpec((r, cols), lambda a, b, stride=stride: ((a * nb + b) // stride, 0)))
        out_shapes.append(jax.ShapeDtypeStruct((rows, cols), BF16))
        operands.append(arr)
    return tuple(strides), in_specs, out_specs, out_shapes, operands


def _cast_step(src_refs, dst_refs, strides):
    step = pl.program_id(0) * pl.num_programs(1) + pl.program_id(1)
    for src, dst, stride in zip(src_refs, dst_refs, strides):
        @pl.when(step % stride == 0)
        def _(src=src, dst=dst):
            dst[...] = src[...].astype(BF16)


def _proj_kernel(x_ref, w_ref, b_ref, wg_ref, bg_ref, o_ref, g_ref, gt_ref, xb_ref):
    @pl.when(pl.program_id(1) == 0)
    def _():
        xb = x_ref[...].astype(BF16)
        xb_ref[...] = xb
        g = lax.dot_general(xb, wg_ref[...], NT_DIMS, preferred_element_type=F32) + bg_ref[...]
        g_ref[...] = g
        gt_ref[...] = g.T[:GATE_ROWS, :]

    acc = lax.dot_general(xb_ref[...], w_ref[...], NT_DIMS, preferred_element_type=F32)
    o_ref[...] = (acc + b_ref[...]).astype(o_ref.dtype)


def _in_projection(x, w, layer, b, wg, bg):
    m, d = x.shape
    n = b.shape[1]
    bm = _tile(m, PROJ_BM, LANES)
    bn = _tile(n, PROJ_BN, LANES)
    if layer is None:
        w_spec = pl.BlockSpec((bn, d), lambda i, j: (j, 0))
    else:
        w_spec = pl.BlockSpec((None, bn, d), lambda i, j: (layer, j, 0))
    return pl.pallas_call(
        _proj_kernel,
        grid=(m // bm, n // bn),
        in_specs=[
            pl.BlockSpec((bm, d), lambda i, j: (i, 0)),
            w_spec,
            pl.BlockSpec((1, bn), lambda i, j: (0, j)),
            pl.BlockSpec((LANES, d), lambda i, j: (0, 0)),
            pl.BlockSpec((1, LANES), lambda i, j: (0, 0)),
        ],
        out_specs=[
            pl.BlockSpec((bm, bn), lambda i, j: (i, j)),
            pl.BlockSpec((bm, LANES), lambda i, j: (i, 0)),
            pl.BlockSpec((GATE_ROWS, bm), lambda i, j: (0, i)),
        ],
        out_shape=[
            jax.ShapeDtypeStruct((m, n), BF16),
            jax.ShapeDtypeStruct((m, LANES), F32),
            jax.ShapeDtypeStruct((GATE_ROWS, m), F32),
        ],
        scratch_shapes=[pltpu.VMEM((bm, d), BF16)],
        compiler_params=_params(("arbitrary", "arbitrary")),
        name="in_projection",
    )(x, w, b, wg, bg)


def _fox_gate_kernel(g_ref, c_ref, carry_ref):
    @pl.when(pl.program_id(1) == 0)
    def _():
        carry_ref[...] = jnp.zeros_like(carry_ref)

    t = g_ref.shape[0]
    row = lax.broadcasted_iota(jnp.int32, (t, t), 0)
    col = lax.broadcasted_iota(jnp.int32, (t, t), 1)
    c = _cumsum_rows((col <= row).astype(BF16), _log_sigmoid(g_ref[...])) + carry_ref[...]
    c_ref[...] = c
    carry_ref[...] = c[t - 1:t, :]


def _fox_gate_cumsum(g, batch, seq):
    t = _tile(seq, CUM_T, LANES)
    ns = seq // t
    return pl.pallas_call(
        _fox_gate_kernel,
        grid=(batch, ns),
        in_specs=[pl.BlockSpec((t, LANES), lambda b, s: (b * ns + s, 0))],
        out_specs=pl.BlockSpec((t, LANES), lambda b, s: (b * ns + s, 0)),
        out_shape=jax.ShapeDtypeStruct((batch * seq, LANES), F32),
        scratch_shapes=[pltpu.VMEM((1, LANES), F32)],
        compiler_params=_params(("arbitrary", "arbitrary")),
        name="fox_gate_cumsum",
    )(g)


def _bias_lanes(c, ones_first):
    n = c.shape[0]
    terms = [h.astype(F32) for h in _split3(c)]
    k = len(terms)
    lane = lax.broadcasted_iota(jnp.int32, (n, HEAD_DIM), 1)
    first_split, first_one = (k, 0) if ones_first else (0, k)
    out = jnp.where(jnp.logical_and(lane >= first_one, lane < first_one + k), 1.0, 0.0)
    for i, term in enumerate(terms):
        out = jnp.where(lane == first_split + i, term, out)
    return out.astype(BF16)


def _fox_kernel(q_ref, k_ref, v_ref, c_ref, o_ref, kb_ref, m_ref, acc_ref, s_ref, *, scale, group, tq, tk):
    first_head = pl.program_id(1) * group
    per = tq // tk
    assert per == 2
    unit = (lax.broadcasted_iota(jnp.int32, (tk, HEAD_DIM), 1) == 0).astype(BF16)
    slices = [slice(g * HEAD_DIM, (g + 1) * HEAD_DIM) for g in range(group)]

    def head_lane(c, g):
        lane = lax.broadcasted_iota(jnp.int32, c.shape, 1)
        return jnp.sum(jnp.where(lane == first_head + g, c, 0.0), axis=-1, keepdims=True) * LOG2_E

    for g in range(group):
        kb_ref[g] = _bias_lanes(-head_lane(c_ref[...], g), ones_first=False)

    @pl.loop(0, q_ref.shape[0] // tq)
    def _(qi):
        q_rows = pl.ds(pl.multiple_of(qi * tq, tq), tq)
        ct = c_ref[q_rows, :]
        q_ext = []
        for g, sl in enumerate(slices):
            q = (q_ref[q_rows, sl].astype(F32) * (scale * LOG2_E)).astype(BF16)
            q_ext.append(jnp.concatenate([q, _bias_lanes(head_lane(ct, g), ones_first=True)], axis=1))

        def logits(slot, ki):
            k_rows = pl.ds(pl.multiple_of(ki * tk, tk), tk)
            for g, sl in enumerate(slices):
                k_ext = jnp.concatenate([k_ref[k_rows, sl], kb_ref[g, k_rows, :]], axis=1)
                s_ref[slot, g] = lax.dot_general(q_ext[g], k_ext, NT_DIMS, preferred_element_type=F32)

        def absorb(slot, ki, band=None):
            k_rows = pl.ds(pl.multiple_of(ki * tk, tk), tk)
            for g, sl in enumerate(slices):
                s = s_ref[slot, g]
                if band is not None:
                    row = lax.broadcasted_iota(jnp.int32, (tq, tk), 0)
                    col = lax.broadcasted_iota(jnp.int32, (tq, tk), 1)
                    s = jnp.where(col + band * tk <= row, s, -jnp.inf)
                m = m_ref[g]
                m_new = jnp.maximum(m, jnp.max(s, axis=-1, keepdims=True))
                p = jnp.exp2(s - m_new).astype(BF16)
                v_ext = jnp.concatenate([v_ref[k_rows, sl], unit], axis=1)
                acc_ref[g] = jnp.exp2(m - m_new) * acc_ref[g] + jnp.dot(p, v_ext, preferred_element_type=F32)
                m_ref[g] = m_new

        m_ref[...] = jnp.full(m_ref.shape, -jnp.inf, F32)
        acc_ref[...] = jnp.zeros_like(acc_ref)

        logits(0, 0)

        @pl.loop(0, qi)
        def _(pair):
            ki = per * pair
            logits(1, ki + 1)
            absorb(0, ki)
            logits(0, ki + 2)
            absorb(1, ki + 1)

        logits(1, per * qi + 1)
        absorb(0, per * qi, band=0)
        absorb(1, per * qi + 1, band=1)

        for g, sl in enumerate(slices):
            acc = acc_ref[g]
            o_ref[q_rows, sl] = (acc[:, :HEAD_DIM] / acc[:, HEAD_DIM:HEAD_DIM + 1]).astype(o_ref.dtype)


def _fox_attention(proj, c, batch, seq, heads):
    tq = _tile(seq, FOX_TQ, LANES)
    tk = tq // 2
    group = FOX_GROUP
    assert heads % group == 0
    hb = heads // group
    gw = group * HEAD_DIM
    return pl.pallas_call(
        functools.partial(_fox_kernel, scale=HEAD_DIM ** -0.5, group=group, tq=tq, tk=tk),
        grid=(batch, hb),
        in_specs=[
            pl.BlockSpec((seq, gw), lambda b, h: (b, h)),
            pl.BlockSpec((seq, gw), lambda b, h: (b, hb + h)),
            pl.BlockSpec((seq, gw), lambda b, h: (b, 2 * hb + h)),
            pl.BlockSpec((seq, LANES), lambda b, h: (b, 0)),
        ],
        out_specs=pl.BlockSpec((seq, gw), lambda b, h: (b, h)),
        out_shape=jax.ShapeDtypeStruct((batch * seq, heads * HEAD_DIM), BF16),
        scratch_shapes=[pltpu.VMEM((group, seq, HEAD_DIM), BF16),
                        pltpu.VMEM((group, tq, 1), F32),
                        pltpu.VMEM((group, tq, 2 * HEAD_DIM), F32),
                        pltpu.VMEM((2, group, tq, tk), F32)],
        compiler_params=_params(("arbitrary", "arbitrary")),
        name="fox_attention",
    )(proj, proj, proj, c)


def _rope(x, cos, sin_signed):
    xf = x.astype(F32)
    return xf * cos + pltpu.roll(xf, HEAD_DIM // 2, 1) * sin_signed


def _swa_kernel(sink_ref, q_ref, k_ref, v_ref, kp_ref, vp_ref, cos_ref, sin_ref, cosp_ref, sinp_ref, *rest,
                scale, group, cast_strides):
    nc = len(cast_strides)
    o_ref = rest[nc]
    _cast_step(rest[:nc], rest[nc + 1:], cast_strides)
    w = SWA_WINDOW
    first_visible = jnp.where(pl.program_id(1) == 0, w, 0)
    nblk = q_ref.shape[0] // w
    cos, sin = cos_ref[...], sin_ref[...]
    cos_q, sin_q = cos * scale, sin * scale
    cosp, sinp = cosp_ref[...], sinp_ref[...]
    row = lax.broadcasted_iota(jnp.int32, (w, 2 * w), 0)
    col = lax.broadcasted_iota(jnp.int32, (w, 2 * w), 1)
    valid = jnp.logical_and(col > row, col - w <= row)
    valid_first = jnp.logical_and(valid, col >= first_visible)
    for kv in range(SWA_KV_HEADS):
        ksl = slice(kv * HEAD_DIM, (kv + 1) * HEAD_DIM)
        k_all = jnp.concatenate([_rope(kp_ref[:, ksl], cosp, sinp), _rope(k_ref[:, ksl], cos, sin)],
                                axis=0).astype(BF16)
        v_all = jnp.concatenate([vp_ref[:, ksl], v_ref[:, ksl]], axis=0)
        for g in range(group):
            hq = kv * group + g
            qsl = slice(hq * HEAD_DIM, (hq + 1) * HEAD_DIM)
            q = _rope(q_ref[:, qsl], cos_q, sin_q).astype(BF16)
            sink = sink_ref[hq]
            for blk in range(nblk):
                qb = q[blk * w:(blk + 1) * w, :]
                kb = k_all[blk * w:(blk + 2) * w, :]
                vb = v_all[blk * w:(blk + 2) * w, :]
                s = lax.dot_general(qb, kb, NT_DIMS, preferred_element_type=F32)
                s = jnp.where(valid_first if blk == 0 else valid, s, -jnp.inf)
                m = jnp.maximum(jnp.max(s, axis=-1, keepdims=True), sink)
                p = jnp.exp(s - m)
                denom = jnp.sum(p, axis=-1, keepdims=True) + jnp.exp(sink - m)
                o = jnp.dot((p * (1.0 / denom)).astype(BF16), vb, preferred_element_type=F32)
                o_ref[blk * w:(blk + 1) * w, qsl] = o.astype(o_ref.dtype)


def _swa_attention(proj, sinks, cos, sin_signed, batch, seq, q_col, k_col, v_col, q_heads, cast_sources):
    t = _tile(seq, SWA_T, SWA_WINDOW)
    nt = seq // t
    per = t // SWA_WINDOW
    qw = q_heads * HEAD_DIM
    kvw = SWA_KV_HEADS * HEAD_DIM
    assert q_col % qw == 0 and k_col % kvw == 0 and v_col % kvw == 0

    def prev(b, i):
        return jnp.maximum((b * nt + i) * per - 1, 0)

    def prev_pos(i):
        return jnp.maximum(i * per - 1, 0)

    strides, cast_in, cast_out, cast_shapes, cast_ops = _cast_plan(cast_sources, (batch, nt))
    return pl.pallas_call(
        functools.partial(_swa_kernel, scale=HEAD_DIM ** -0.5, group=q_heads // SWA_KV_HEADS, cast_strides=strides),
        grid=(batch, nt),
        in_specs=[
            pl.BlockSpec(memory_space=pltpu.SMEM),
            pl.BlockSpec((t, qw), lambda b, i: (b * nt + i, q_col // qw)),
            pl.BlockSpec((t, kvw), lambda b, i: (b * nt + i, k_col // kvw)),
            pl.BlockSpec((t, kvw), lambda b, i: (b * nt + i, v_col // kvw)),
            pl.BlockSpec((SWA_WINDOW, kvw), lambda b, i: (prev(b, i), k_col // kvw)),
            pl.BlockSpec((SWA_WINDOW, kvw), lambda b, i: (prev(b, i), v_col // kvw)),
            pl.BlockSpec((t, HEAD_DIM), lambda b, i: (i, 0)),
            pl.BlockSpec((t, HEAD_DIM), lambda b, i: (i, 0)),
            pl.BlockSpec((SWA_WINDOW, HEAD_DIM), lambda b, i: (prev_pos(i), 0)),
            pl.BlockSpec((SWA_WINDOW, HEAD_DIM), lambda b, i: (prev_pos(i), 0)),
        ] + cast_in,
        out_specs=[pl.BlockSpec((t, qw), lambda b, i: (b * nt + i, 0))] + cast_out,
        out_shape=[jax.ShapeDtypeStruct((batch * seq, qw), BF16)] + cast_shapes,
        compiler_params=_params(("arbitrary", "arbitrary")),
        name="swa_attention",
    )(sinks, proj, proj, proj, proj, proj, cos, sin_signed, cos, sin_signed, *cast_ops)


def _cummax_rows(x):
    n = x.shape[0]
    row = lax.broadcasted_iota(jnp.int32, x.shape, 0)
    shift = 1
    while shift < n:
        x = jnp.maximum(x, jnp.where(row >= shift, pltpu.roll(x, shift, 0), -jnp.inf))
        shift *= 2
    return x


def _mlstm_kernel(q_ref, k_ref, v_ref, og_ref, g_ref, gt_ref, *rest, dk, dv, cast_strides):
    heads = MLSTM_HEADS
    nc = len(cast_strides)
    o_ref = rest[nc]
    state_ref, m_ref = rest[2 * nc + 1:]
    _cast_step(rest[:nc], rest[nc + 1:2 * nc + 1], cast_strides)

    @pl.when(pl.program_id(1) == 0)
    def _():
        state_ref[...] = jnp.zeros_like(state_ref)
        m_ref[...] = jnp.zeros_like(m_ref)

    n = q_ref.shape[0]
    g = g_ref[...]
    gt = gt_ref[...]
    row = lax.broadcasted_iota(jnp.int32, (n, n), 0)
    col = lax.broadcasted_iota(jnp.int32, (n, n), 1)
    causal = col <= row
    b_row_all = _cumsum_lanes(_log_sigmoid(gt), (row <= col).astype(BF16))
    b_all = pltpu.roll(_cumsum_rows(causal.astype(BF16), _log_sigmoid(g)), LANES - heads, 1)
    kc_all = g - b_all
    kcmax_all = _cummax_rows(kc_all)
    spread = (lax.broadcasted_iota(jnp.int32, (LANES, heads * LANES), 1) // LANES
              == lax.broadcasted_iota(jnp.int32, (LANES, heads * LANES), 0)).astype(BF16)
    b_wide, kc_wide, kcmax_wide = (
        sum(jnp.dot(term, spread, preferred_element_type=F32) for term in _split3(x))
        for x in (b_all, kc_all, kcmax_all))
    ones = jnp.ones((n, LANES), BF16)
    tiles = n // LANES

    def wide(x, reps):
        return jnp.concatenate([x] * reps, axis=1)

    for h in range(heads):
        hs = slice(h * LANES, (h + 1) * LANES)
        b_col, kc_col, kcmax_col = b_wide[:, hs], kc_wide[:, hs], kcmax_wide[:, hs]
        i_row = gt[h:h + 1, :]
        b_row = b_row_all[heads + h:heads + h + 1, :]
        b_last = b_row[:, n - 1:n]
        m_prev = m_ref[h]

        m_inter = b_col + m_prev
        m_t = jnp.maximum(m_inter, b_col + kcmax_col)
        inter = jnp.exp(m_inter - m_t)
        d = jnp.where(causal, (wide(b_col, tiles) - b_row) + i_row, -jnp.inf)
        wgt = jnp.exp(d - wide(m_t, tiles))

        qh = q_ref[:, h * dk:(h + 1) * dk]
        kf = k_ref[:, h * dk:(h + 1) * dk].astype(F32) * (dk ** -0.5)
        kh = kf.astype(BF16)
        v_ext = jnp.concatenate([v_ref[:, h * dv:(h + 1) * dv], ones], axis=1)
        state = state_ref[h]

        sm = wgt * lax.dot_general(qh, kh, NT_DIMS, preferred_element_type=F32)
        tot = wide(inter, dv // LANES + 1) * jnp.dot(qh, state.astype(BF16), preferred_element_type=F32)
        tot = tot + jnp.dot(sm.astype(BF16), v_ext, preferred_element_type=F32)
        den = jnp.maximum(jnp.abs(tot[:, dv:]), jnp.exp(-m_t))
        hid = tot[:, :dv] * wide(1.0 / den, dv // LANES)
        gate = _sigmoid(og_ref[:, h * dv:(h + 1) * dv].astype(F32))
        o_ref[:, h * dv:(h + 1) * dv] = (gate * hid).astype(o_ref.dtype)

        m_new = jnp.maximum(b_last + m_prev, jnp.max((b_last - b_row) + i_row, axis=-1, keepdims=True))
        decay = jnp.exp(b_last + m_prev - m_new)
        w_end = jnp.exp((b_last + kc_col) - m_new)
        kw = (kf * w_end).astype(BF16)
        state_ref[h] = decay * state + lax.dot_general(kw, v_ext, TN_DIMS, preferred_element_type=F32)
        m_ref[h] = m_new


def _mlstm(proj, g, gt, batch, seq, dk, dv, cast_sources):
    heads = MLSTM_HEADS
    n = _tile(seq, MLSTM_L, LANES)
    nc = seq // n
    qk, vw = heads * dk, heads * dv
    assert vw % qk == 0
    r = vw // qk
    strides, cast_in, cast_out, cast_shapes, cast_ops = _cast_plan(cast_sources, (batch, nc))
    return pl.pallas_call(
        functools.partial(_mlstm_kernel, dk=dk, dv=dv, cast_strides=strides),
        grid=(batch, nc),
        in_specs=[
            pl.BlockSpec((n, qk), lambda b, c: (b * nc + c, 0)),
            pl.BlockSpec((n, qk), lambda b, c: (b * nc + c, 1)),
            pl.BlockSpec((n, vw), lambda b, c: (b * nc + c, 2 // r)),
            pl.BlockSpec((n, vw), lambda b, c: (b * nc + c, 2 // r + 1)),
            pl.BlockSpec((n, LANES), lambda b, c: (b * nc + c, 0)),
            pl.BlockSpec((GATE_ROWS, n), lambda b, c: (0, b * nc + c)),
        ] + cast_in,
        out_specs=[pl.BlockSpec((n, vw), lambda b, c: (b * nc + c, 0))] + cast_out,
        out_shape=[jax.ShapeDtypeStruct((batch * seq, vw), BF16)] + cast_shapes,
        scratch_shapes=[pltpu.VMEM((heads, dk, dv + LANES), F32), pltpu.VMEM((heads, 1, 1), F32)],
        compiler_params=_params(("arbitrary", "arbitrary")),
        name="mlstm",
    )(proj, proj, proj, proj, g, gt, *cast_ops)


def _outproj_ln_kernel(*refs, alpha, n_in):
    h_refs, w_refs = refs[:n_in], refs[n_in:2 * n_in]
    x_ref, g_ref, b_ref, o_ref = refs[2 * n_in:]
    for r in range(o_ref.shape[0] // OUT_ROWS):
        rows = slice(r * OUT_ROWS, (r + 1) * OUT_ROWS)
        o_ref[rows, :] = sum(jnp.dot(h[rows, :], w[...], preferred_element_type=F32) for h, w in zip(h_refs, w_refs))
        o_ref[rows, :] = _layer_norm(alpha * x_ref[rows, :] + o_ref[rows, :], g_ref[...], b_ref[...])


def _out_projection_ln(hs, w, x, ln_g, ln_b, alpha):
    m, d = x.shape
    bm = _tile(m, OUT_BM, LANES)
    widths = [h.shape[1] for h in hs]
    assert all(wd == widths[0] for wd in widths) and w.shape[0] == sum(widths)
    in_specs = [pl.BlockSpec((bm, wd), lambda i: (i, 0)) for wd in widths]
    in_specs += [pl.BlockSpec((wd, d), lambda i, k=k: (k, 0)) for k, wd in enumerate(widths)]
    in_specs += [pl.BlockSpec((bm, d), lambda i: (i, 0)),
                 pl.BlockSpec((1, d), lambda i: (0, 0)),
                 pl.BlockSpec((1, d), lambda i: (0, 0))]
    return pl.pallas_call(
        functools.partial(_outproj_ln_kernel, alpha=alpha, n_in=len(hs)),
        grid=(m // bm,),
        in_specs=in_specs,
        out_specs=pl.BlockSpec((bm, d), lambda i: (i, 0)),
        out_shape=jax.ShapeDtypeStruct((m, d), F32),
        compiler_params=_params(("arbitrary",)),
        name="out_projection_ln",
    )(*hs, *([w] * len(hs)), x, ln_g, ln_b)


def _causal_conv(u, prev, p, r0):
    w0, w1, w2, b = p[r0:r0 + 1], p[r0 + 1:r0 + 2], p[r0 + 2:r0 + 3], p[r0 + 3:r0 + 4]
    body = b + w0 * pltpu.roll(u, 2, 0) + w1 * pltpu.roll(u, 1, 0) + w2 * u
    top = jnp.concatenate([prev, u[:HEAD_ROWS, :]], axis=0)
    head = b + w0 * pltpu.roll(top, 2, 0) + w1 * pltpu.roll(top, 1, 0) + w2 * top
    return body, head[SUBLANES:, :]


def _ffn_kernel(x_ref, wg_ref, wv_ref, cp_ref, wd_ref, lg_ref, lb_ref, o_ref,
                xb_ref, h0_ref, h1_ref, pg_ref, pv_ref, ug_ref, uv_ref, *, alpha, tiles_per_seq, nf):
    i = pl.program_id(0)
    j = pl.program_id(1)
    bm = x_ref.shape[0]
    h_refs = (h0_ref, h1_ref)

    def up(h_ref, first_step=False):
        kept_rows = jnp.where(i % tiles_per_seq != 0, SUBLANES, 0)
        keep = lax.broadcasted_iota(jnp.int32, (SUBLANES, FFN_CHUNK), 0) < kept_rows
        chunks = [slice(c * FFN_CHUNK, (c + 1) * FFN_CHUNK) for c in range(h_ref.shape[1] // FFN_CHUNK)]
        prev_g = [jnp.where(keep, pg_ref[j, :, cs], 0.0) for cs in chunks]
        prev_v = [jnp.where(keep, pv_ref[j, :, cs], 0.0) for cs in chunks]
        for r in range(bm // FFN_ROWS):
            rows = slice(r * FFN_ROWS, (r + 1) * FFN_ROWS)
            if first_step:
                xb_ref[rows, :] = x_ref[rows, :].astype(BF16)
            xr = xb_ref[rows, :]
            for c, cs in enumerate(chunks):
                slot = (r * len(chunks) + c) % FFN_SLOTS
                ug_ref[slot] = jnp.dot(xr, wg_ref[:, cs], preferred_element_type=F32)
                uv_ref[slot] = jnp.dot(xr, wv_ref[:, cs], preferred_element_type=F32)
                ug, uv = ug_ref[slot], uv_ref[slot]
                cp = cp_ref[j, :, cs]
                cg, cg_top = _causal_conv(ug, prev_g[c], cp, 0)
                cv, cv_top = _causal_conv(uv, prev_v[c], cp, CONV_WIDTH + 1)
                prev_g[c] = ug[FFN_ROWS - SUBLANES:, :]
                prev_v[c] = uv[FFN_ROWS - SUBLANES:, :]
                h_ref[r * FFN_ROWS:(r + 1) * FFN_ROWS, cs] = (cg * _sigmoid(cg) * cv).astype(BF16)
                h_ref[r * FFN_ROWS:r * FFN_ROWS + HEAD_ROWS, cs] = (cg_top * _sigmoid(cg_top) * cv_top).astype(BF16)
        for c, cs in enumerate(chunks):
            pg_ref[j, :, cs] = prev_g[c]
            pv_ref[j, :, cs] = prev_v[c]

    def down(h_ref):
        o_ref[...] += jnp.dot(h_ref[...], wd_ref[...], preferred_element_type=F32)

    @pl.when(j == 0)
    def _():
        o_ref[...] = jnp.zeros_like(o_ref)
        up(h_refs[0], first_step=True)

    for parity in range(2):
        @pl.when(jnp.logical_and(jnp.logical_and(j > 0, j < nf), j % 2 == parity))
        def _():
            up(h_refs[parity])
            down(h_refs[1 - parity])

    @pl.when(j == nf)
    def _():
        h_ref = h_refs[(nf - 1) % 2]
        for r in range(bm // FFN_ROWS):
            rows = slice(r * FFN_ROWS, (r + 1) * FFN_ROWS)
            o_ref[rows, :] += jnp.dot(h_ref[rows, :], wd_ref[...], preferred_element_type=F32)
            o_ref[rows, :] = _layer_norm(alpha * x_ref[rows, :] + o_ref[rows, :], lg_ref[...], lb_ref[...])


def _conv_ffn_ln(x, w_up, w_down, conv_w, conv_b, ln_g, ln_b, alpha, seq):
    m, d = x.shape
    f = w_down.shape[0]
    bm = _tile(seq, FFN_BM, 2 * SUBLANES)
    bf = _tile(f, FFN_BF, LANES)
    nf = f // bf

    conv_p = jnp.concatenate([conv_w[:, :f], conv_b[:, :f], conv_w[:, f:], conv_b[:, f:]], axis=0)
    conv_p = conv_p.reshape(2 * (CONV_WIDTH + 1), nf, bf).transpose(1, 0, 2)

    def up_blk(j):
        return jnp.minimum(j, nf - 1)

    def down_blk(j):
        return jnp.maximum(j - 1, 0)

    return pl.pallas_call(
        functools.partial(_ffn_kernel, alpha=alpha, tiles_per_seq=seq // bm, nf=nf),
        grid=(m // bm, nf + 1),
        in_specs=[
            pl.BlockSpec((bm, d), lambda i, j: (i, 0)),
            pl.BlockSpec((d, bf), lambda i, j: (0, up_blk(j))),
            pl.BlockSpec((d, bf), lambda i, j: (0, nf + up_blk(j))),
            pl.BlockSpec(conv_p.shape, lambda i, j: (0, 0, 0)),
            pl.BlockSpec((bf, d), lambda i, j: (down_blk(j), 0)),
            pl.BlockSpec((1, d), lambda i, j: (0, 0)),
            pl.BlockSpec((1, d), lambda i, j: (0, 0)),
        ],
        out_specs=pl.BlockSpec((bm, d), lambda i, j: (i, 0)),
        out_shape=jax.ShapeDtypeStruct((m, d), F32),
        scratch_shapes=[
            pltpu.VMEM((bm, d), BF16),
            pltpu.VMEM((bm, bf), BF16),
            pltpu.VMEM((bm, bf), BF16),
            pltpu.VMEM((nf, SUBLANES, bf), F32),
            pltpu.VMEM((nf, SUBLANES, bf), F32),
            pltpu.VMEM((FFN_SLOTS, FFN_ROWS, FFN_CHUNK), F32),
            pltpu.VMEM((FFN_SLOTS, FFN_ROWS, FFN_CHUNK), F32),
        ],
        compiler_params=_params(("arbitrary", "arbitrary")),
        name="conv_ffn_ln",
    )(x, w_up, w_up, conv_p, w_down, ln_g, ln_b)


def _gate_params(w, b):
    n, d = w.shape
    wg = jnp.zeros((LANES, d), BF16).at[:n, :].set(w.astype(BF16))
    bg = jnp.zeros((1, LANES), F32).at[0, :n].set(b)
    return wg, bg


def _rope_tables(seq):
    half = HEAD_DIM // 2
    inv_freq = jnp.power(ROPE_THETA, -jnp.arange(half, dtype=F32) * (2.0 / HEAD_DIM))
    ang = jnp.arange(seq, dtype=F32)[:, None] * inv_freq[None, :]
    cos, sin = jnp.cos(ang), jnp.sin(ang)
    return jnp.concatenate([cos, cos], axis=-1), jnp.concatenate([-sin, sin], axis=-1)


def kernel(x, attn_w_in, attn_b_in, attn_sinks, attn_w_out, mlstm_w_in, mlstm_b_in, mlstm_w_out, ffn_w_up,
           ffn_conv_w, ffn_conv_b, ffn_w_down, ln1_g, ln1_b, ln2_g, ln2_b):
    batch, seq, d = x.shape
    depth = ln1_g.shape[0]
    alpha = float((2 * depth) ** 0.25)
    fox_heads = d // (2 * HEAD_DIM)
    swa_heads = d // (2 * HEAD_DIM)
    fox_dim = fox_heads * HEAD_DIM
    fox_f_off = 3 * fox_dim
    dk, dv = d // (2 * MLSTM_HEADS), d // MLSTM_HEADS
    mlstm_main = 2 * MLSTM_HEADS * dk + 2 * MLSTM_HEADS * dv
    cos, sin_signed = _rope_tables(seq)

    gate_lo, gate_hi = fox_f_off, fox_f_off + fox_heads
    attn_wt, mlstm_wt = jnp.swapaxes(attn_w_in, 1, 2), jnp.swapaxes(mlstm_w_in, 1, 2)
    attn_w_gate = lax.optimization_barrier(attn_wt[:, gate_lo:gate_hi])
    mlstm_w_gate = lax.optimization_barrier(mlstm_wt[:, mlstm_main:])
    attn_w_main = jnp.concatenate([attn_wt[:, :gate_lo], attn_wt[:, gate_hi:]], axis=1).astype(BF16)
    attn_b_main = jnp.concatenate([attn_b_in[:, :gate_lo], attn_b_in[:, gate_hi:]], axis=1)
    mlstm_w_next = None

    h = x.reshape(batch * seq, d)
    for layer in range(depth):
        j = layer // 2
        w_out = attn_w_out if layer % 2 == 0 else mlstm_w_out
        late_w = [(ffn_w_up, layer, ffn_w_up.shape[1]), (ffn_w_down, layer, ffn_w_down.shape[1]),
                  (w_out, j, w_out.shape[1])]
        if layer % 2 == 0:
            gates = _gate_params(attn_w_gate[j], attn_b_in[j][gate_lo:gate_hi])
            proj, g, gt = _in_projection(h, attn_w_main, j, attn_b_main[j][None, :], *gates)
            fox = _fox_attention(proj, _fox_gate_cumsum(g, batch, seq), batch, seq, fox_heads)
            swa_q = 3 * fox_dim
            swa_k = swa_q + swa_heads * HEAD_DIM
            swa_v = swa_k + SWA_KV_HEADS * HEAD_DIM
            if layer + 1 < depth:
                late_w.append((mlstm_wt, j, mlstm_main))
            swa, w_up_b, w_down_b, w_out_b, *rest = _swa_attention(proj, attn_sinks[j], cos, sin_signed, batch, seq,
                                                                   swa_q, swa_k, swa_v, swa_heads, late_w)
            mlstm_w_next = rest[0] if rest else None
            h = _out_projection_ln([fox, swa], w_out_b, h, ln1_g[layer][None, :], ln1_b[layer][None, :], alpha)
        else:
            gates = _gate_params(mlstm_w_gate[j], mlstm_b_in[j][mlstm_main:])
            proj, g, gt = _in_projection(h, mlstm_w_next, None, mlstm_b_in[j][None, :mlstm_main], *gates)
            mixed, w_up_b, w_down_b, w_out_b = _mlstm(proj, g, gt, batch, seq, dk, dv, late_w)
            h = _out_projection_ln([mixed], w_out_b, h, ln1_g[layer][None, :], ln1_b[layer][None, :], alpha)
        h = _conv_ffn_ln(h, w_up_b, w_down_b, ffn_conv_w[layer], ffn_conv_b[layer][None, :],
                         ln2_g[layer][None, :], ln2_b[layer][None, :], alpha, seq)
    return h.reshape(batch, seq, d)
```

```python
import functools

import jax
import jax.numpy as jnp
from jax import lax
from jax.experimental import pallas as pl
from jax.experimental.pallas import tpu as pltpu

F32 = jnp.float32
BF16 = jnp.bfloat16

HEAD_DIM = 128
SWA_KV_HEADS = 2
SWA_WINDOW = 128
ROPE_THETA = 10000.0
MLSTM_HEADS = 8
CONV_WIDTH = 3
LN_EPS = 1e-5
LOG2_E = 1.4426950408889634

LANES = 128
SUBLANES = 8
GATE_ROWS = 16
HEAD_ROWS = 16
VMEM_LIMIT_BYTES = 60 * 1024 * 1024

PROJ_BM = 1024
PROJ_BN = 1536
OUT_BM = 512
OUT_ROWS = 128
FFN_BM = 1024
FFN_BF = 512
FFN_CHUNK = 256
FFN_ROWS = 256
FFN_SLOTS = 2
FOX_TQ = 1024
FOX_GROUP = 2
SWA_T = 512
CUM_T = 512
MLSTM_L = 256

NT_DIMS = (((1,), (1,)), ((), ()))
TN_DIMS = (((0,), (0,)), ((), ()))


def _tile(n, pref, unit):
    t = min(pref, n)
    while n % t or t % unit:
        t -= unit
    assert t > 0, (n, pref, unit)
    return t


def _params(sem):
    return pltpu.CompilerParams(dimension_semantics=sem, vmem_limit_bytes=VMEM_LIMIT_BYTES)


def _log_sigmoid(x):
    return jnp.minimum(x, 0.0) - jnp.log(1.0 + jnp.exp(-jnp.abs(x)))


def _sigmoid(x):
    return 1.0 / (1.0 + jnp.exp(-x))


def _split3(x):
    h1 = x.astype(BF16)
    r1 = x - h1.astype(F32)
    h2 = r1.astype(BF16)
    h3 = (r1 - h2.astype(F32)).astype(BF16)
    return h1, h2, h3


def _cumsum_rows(tri, x):
    return sum(jnp.dot(tri, h, preferred_element_type=F32) for h in _split3(x))


def _cumsum_lanes(x, tri_t):
    return sum(jnp.dot(h, tri_t, preferred_element_type=F32) for h in _split3(x))


def _layer_norm(z, g, b):
    mu = jnp.mean(z, axis=-1, keepdims=True)
    zc = z - mu
    var = jnp.mean(zc * zc, axis=-1, keepdims=True)
    return zc * lax.rsqrt(var + LN_EPS) * g + b


def _cast_plan(sources, grid):
    nb = grid[1]
    steps = grid[0] * nb
    strides, in_specs, out_specs, out_shapes, operands = [], [], [], [], []
    for arr, layer, rows in sources:
        cols = arr.shape[2]
        chunks = steps
        while steps % chunks or rows % chunks or (rows // chunks) % (2 * SUBLANES):
            chunks -= 1
        stride, r = steps // chunks, rows // chunks
        strides.append(stride)
        in_specs.append(pl.BlockSpec((None, r, cols), lambda a, b, stride=stride, layer=layer:
                                     (layer, (a * nb + b) // stride, 0)))
        out_specs.append(pl.BlockSpec((r, cols), lambda a, b, stride=stride: ((a * nb + b) // stride, 0)))
        out_shapes.append(jax.ShapeDtypeStruct((rows, cols), BF16))
        operands.append(arr)
    return tuple(strides), in_specs, out_specs, out_shapes, operands


def _cast_step(src_refs, dst_refs, strides):
    step = pl.program_id(0) * pl.num_programs(1) + pl.program_id(1)
    for src, dst, stride in zip(src_refs, dst_refs, strides):
        @pl.when(step % stride == 0)
        def _(src=src, dst=dst):
            dst[...] = src[...].astype(BF16)


def _proj_kernel(x_ref, w_ref, b_ref, wg_ref, bg_ref, o_ref, g_ref, gt_ref, xb_ref):
    @pl.when(pl.program_id(1) == 0)
    def _():
        xb = x_ref[...].astype(BF16)
        xb_ref[...] = xb
        g = lax.dot_general(xb, wg_ref[...], NT_DIMS, preferred_element_type=F32) + bg_ref[...]
        g_ref[...] = g
        gt_ref[...] = g.T[:GATE_ROWS, :]

    acc = lax.dot_general(xb_ref[...], w_ref[...], NT_DIMS, preferred_element_type=F32)
    o_ref[...] = (acc + b_ref[...]).astype(o_ref.dtype)


def _in_projection(x, w, layer, b, wg, bg):
    m, d = x.shape
    n = b.shape[1]
    bm = _tile(m, PROJ_BM, LANES)
    bn = _tile(n, PROJ_BN, LANES)
    if layer is None:
        w_spec = pl.BlockSpec((bn, d), lambda i, j: (j, 0))
    else:
        w_spec = pl.BlockSpec((None, bn, d), lambda i, j: (layer, j, 0))
    return pl.pallas_call(
        _proj_kernel,
        grid=(m // bm, n // bn),
        in_specs=[
            pl.BlockSpec((bm, d), lambda i, j: (i, 0)),
            w_spec,
            pl.BlockSpec((1, bn), lambda i, j: (0, j)),
            pl.BlockSpec((LANES, d), lambda i, j: (0, 0)),
            pl.BlockSpec((1, LANES), lambda i, j: (0, 0)),
        ],
        out_specs=[
            pl.BlockSpec((bm, bn), lambda i, j: (i, j)),
            pl.BlockSpec((bm, LANES), lambda i, j: (i, 0)),
            pl.BlockSpec((GATE_ROWS, bm), lambda i, j: (0, i)),
        ],
        out_shape=[
            jax.ShapeDtypeStruct((m, n), BF16),
            jax.ShapeDtypeStruct((m, LANES), F32),
            jax.ShapeDtypeStruct((GATE_ROWS, m), F32),
        ],
        scratch_shapes=[pltpu.VMEM((bm, d), BF16)],
        compiler_params=_params(("arbitrary", "arbitrary")),
        name="in_projection",
    )(x, w, b, wg, bg)


def _fox_gate_kernel(g_ref, c_ref, carry_ref):
    @pl.when(pl.program_id(1) == 0)
    def _():
        carry_ref[...] = jnp.zeros_like(carry_ref)

    t = g_ref.shape[0]
    row = lax.broadcasted_iota(jnp.int32, (t, t), 0)
    col = lax.broadcasted_iota(jnp.int32, (t, t), 1)
    c = _cumsum_rows((col <= row).astype(BF16), _log_sigmoid(g_ref[...])) + carry_ref[...]
    c_ref[...] = c
    carry_ref[...] = c[t - 1:t, :]


def _fox_gate_cumsum(g, batch, seq):
    t = _tile(seq, CUM_T, LANES)
    ns = seq // t
    return pl.pallas_call(
        _fox_gate_kernel,
        grid=(batch, ns),
        in_specs=[pl.BlockSpec((t, LANES), lambda b, s: (b * ns + s, 0))],
        out_specs=pl.BlockSpec((t, LANES), lambda b, s: (b * ns + s, 0)),
        out_shape=jax.ShapeDtypeStruct((batch * seq, LANES), F32),
        scratch_shapes=[pltpu.VMEM((1, LANES), F32)],
        compiler_params=_params(("arbitrary", "arbitrary")),
        name="fox_gate_cumsum",
    )(g)


def _bias_lanes(c, ones_first):
    n = c.shape[0]
    terms = [h.astype(F32) for h in _split3(c)]
    k = len(terms)
    lane = lax.broadcasted_iota(jnp.int32, (n, HEAD_DIM), 1)
    first_split, first_one = (k, 0) if ones_first else (0, k)
    out = jnp.where(jnp.logical_and(lane >= first_one, lane < first_one + k), 1.0, 0.0)
    for i, term in enumerate(terms):
        out = jnp.where(lane == first_split + i, term, out)
    return out.astype(BF16)


def _fox_kernel(q_ref, k_ref, v_ref, c_ref, *rest, scale, group, tq, tk, cast_strides):
    nc = len(cast_strides)
    o_ref = rest[nc]
    kb_ref, m_ref, acc_ref, s_ref = rest[2 * nc + 1:]
    _cast_step(rest[:nc], rest[nc + 1:2 * nc + 1], cast_strides)
    first_head = pl.program_id(1) * group
    per = tq // tk
    assert per == 2
    unit = (lax.broadcasted_iota(jnp.int32, (tk, HEAD_DIM), 1) == 0).astype(BF16)
    slices = [slice(g * HEAD_DIM, (g + 1) * HEAD_DIM) for g in range(group)]

    def head_lane(c, g):
        lane = lax.broadcasted_iota(jnp.int32, c.shape, 1)
        return jnp.sum(jnp.where(lane == first_head + g, c, 0.0), axis=-1, keepdims=True) * LOG2_E

    for g in range(group):
        kb_ref[g] = _bias_lanes(-head_lane(c_ref[...], g), ones_first=False)

    @pl.loop(0, q_ref.shape[0] // tq)
    def _(qi):
        q_rows = pl.ds(pl.multiple_of(qi * tq, tq), tq)
        ct = c_ref[q_rows, :]
        q_ext = []
        for g, sl in enumerate(slices):
            q = (q_ref[q_rows, sl].astype(F32) * (scale * LOG2_E)).astype(BF16)
            q_ext.append(jnp.concatenate([q, _bias_lanes(head_lane(ct, g), ones_first=True)], axis=1))

        def logits(slot, ki):
            k_rows = pl.ds(pl.multiple_of(ki * tk, tk), tk)
            for g, sl in enumerate(slices):
                k_ext = jnp.concatenate([k_ref[k_rows, sl], kb_ref[g, k_rows, :]], axis=1)
                s_ref[slot, g] = lax.dot_general(q_ext[g], k_ext, NT_DIMS, preferred_element_type=F32)

        def absorb(slot, ki, band=None):
            k_rows = pl.ds(pl.multiple_of(ki * tk, tk), tk)
            for g, sl in enumerate(slices):
                s = s_ref[slot, g]
                if band is not None:
                    row = lax.broadcasted_iota(jnp.int32, (tq, tk), 0)
                    col = lax.broadcasted_iota(jnp.int32, (tq, tk), 1)
                    s = jnp.where(col + band * tk <= row, s, -jnp.inf)
                m = m_ref[g]
                m_new = jnp.maximum(m, jnp.max(s, axis=-1, keepdims=True))
                p = jnp.exp2(s - m_new).astype(BF16)
                v_ext = jnp.concatenate([v_ref[k_rows, sl], unit], axis=1)
                acc_ref[g] = jnp.exp2(m - m_new) * acc_ref[g] + jnp.dot(p, v_ext, preferred_element_type=F32)
                m_ref[g] = m_new

        m_ref[...] = jnp.full(m_ref.shape, -jnp.inf, F32)
        acc_ref[...] = jnp.zeros_like(acc_ref)

        logits(0, 0)

        @pl.loop(0, qi)
        def _(pair):
            ki = per * pair
            logits(1, ki + 1)
            absorb(0, ki)
            logits(0, ki + 2)
            absorb(1, ki + 1)

        logits(1, per * qi + 1)
        absorb(0, per * qi, band=0)
        absorb(1, per * qi + 1, band=1)

        for g, sl in enumerate(slices):
            acc = acc_ref[g]
            o_ref[q_rows, sl] = (acc[:, :HEAD_DIM] / acc[:, HEAD_DIM:HEAD_DIM + 1]).astype(o_ref.dtype)


def _fox_attention(proj, c, batch, seq, heads, cast_sources):
    tq = _tile(seq, FOX_TQ, LANES)
    tk = tq // 2
    group = FOX_GROUP
    assert heads % group == 0
    hb = heads // group
    gw = group * HEAD_DIM
    strides, cast_in, cast_out, cast_shapes, cast_ops = _cast_plan(cast_sources, (batch, hb))
    return pl.pallas_call(
        functools.partial(_fox_kernel, scale=HEAD_DIM ** -0.5, group=group, tq=tq, tk=tk, cast_strides=strides),
        grid=(batch, hb),
        in_specs=[
            pl.BlockSpec((seq, gw), lambda b, h: (b, h)),
            pl.BlockSpec((seq, gw), lambda b, h: (b, hb + h)),
            pl.BlockSpec((seq, gw), lambda b, h: (b, 2 * hb + h)),
            pl.BlockSpec((seq, LANES), lambda b, h: (b, 0)),
        ] + cast_in,
        out_specs=[pl.BlockSpec((seq, gw), lambda b, h: (b, h))] + cast_out,
        out_shape=[jax.ShapeDtypeStruct((batch * seq, heads * HEAD_DIM), BF16)] + cast_shapes,
        scratch_shapes=[pltpu.VMEM((group, seq, HEAD_DIM), BF16),
                        pltpu.VMEM((group, tq, 1), F32),
                        pltpu.VMEM((group, tq, 2 * HEAD_DIM), F32),
                        pltpu.VMEM((2, group, tq, tk), F32)],
        compiler_params=_params(("arbitrary", "arbitrary")),
        name="fox_attention",
    )(proj, proj, proj, c, *cast_ops)


def _rope(x, cos, sin_signed):
    xf = x.astype(F32)
    return xf * cos + pltpu.roll(xf, HEAD_DIM // 2, 1) * sin_signed


def _swa_kernel(sink_ref, q_ref, k_ref, v_ref, kp_ref, vp_ref, cos_ref, sin_ref, cosp_ref, sinp_ref, *rest,
                scale, group, cast_strides):
    nc = len(cast_strides)
    o_ref = rest[nc]
    _cast_step(rest[:nc], rest[nc + 1:], cast_strides)
    w = SWA_WINDOW
    first_visible = jnp.where(pl.program_id(1) == 0, w, 0)
    nblk = q_ref.shape[0] // w
    cos, sin = cos_ref[...], sin_ref[...]
    cos_q, sin_q = cos * scale, sin * scale
    cosp, sinp = cosp_ref[...], sinp_ref[...]
    row = lax.broadcasted_iota(jnp.int32, (w, 2 * w), 0)
    col = lax.broadcasted_iota(jnp.int32, (w, 2 * w), 1)
    valid = jnp.logical_and(col > row, col - w <= row)
    valid_first = jnp.logical_and(valid, col >= first_visible)
    for kv in range(SWA_KV_HEADS):
        ksl = slice(kv * HEAD_DIM, (kv + 1) * HEAD_DIM)
        k_all = jnp.concatenate([_rope(kp_ref[:, ksl], cosp, sinp), _rope(k_ref[:, ksl], cos, sin)],
                                axis=0).astype(BF16)
        v_all = jnp.concatenate([vp_ref[:, ksl], v_ref[:, ksl]], axis=0)
        for g in range(group):
            hq = kv * group + g
            qsl = slice(hq * HEAD_DIM, (hq + 1) * HEAD_DIM)
            q = _rope(q_ref[:, qsl], cos_q, sin_q).astype(BF16)
            sink = sink_ref[hq]
            for blk in range(nblk):
                qb = q[blk * w:(blk + 1) * w, :]
                kb = k_all[blk * w:(blk + 2) * w, :]
                vb = v_all[blk * w:(blk + 2) * w, :]
                s = lax.dot_general(qb, kb, NT_DIMS, preferred_element_type=F32)
                s = jnp.where(valid_first if blk == 0 else valid, s, -jnp.inf)
                m = jnp.maximum(jnp.max(s, axis=-1, keepdims=True), sink)
                p = jnp.exp(s - m)
                denom = jnp.sum(p, axis=-1, keepdims=True) + jnp.exp(sink - m)
                o = jnp.dot((p * (1.0 / denom)).astype(BF16), vb, preferred_element_type=F32)
                o_ref[blk * w:(blk + 1) * w, qsl] = o.astype(o_ref.dtype)


def _swa_attention(proj, sinks, cos, sin_signed, batch, seq, q_col, k_col, v_col, q_heads, cast_sources):
    t = _tile(seq, SWA_T, SWA_WINDOW)
    nt = seq // t
    per = t // SWA_WINDOW
    qw = q_heads * HEAD_DIM
    kvw = SWA_KV_HEADS * HEAD_DIM
    assert q_col % qw == 0 and k_col % kvw == 0 and v_col % kvw == 0

    def prev(b, i):
        return jnp.maximum((b * nt + i) * per - 1, 0)

    def prev_pos(i):
        return jnp.maximum(i * per - 1, 0)

    strides, cast_in, cast_out, cast_shapes, cast_ops = _cast_plan(cast_sources, (batch, nt))
    return pl.pallas_call(
        functools.partial(_swa_kernel, scale=HEAD_DIM ** -0.5, group=q_heads // SWA_KV_HEADS, cast_strides=strides),
        grid=(batch, nt),
        in_specs=[
            pl.BlockSpec(memory_space=pltpu.SMEM),
            pl.BlockSpec((t, qw), lambda b, i: (b * nt + i, q_col // qw)),
            pl.BlockSpec((t, kvw), lambda b, i: (b * nt + i, k_col // kvw)),
            pl.BlockSpec((t, kvw), lambda b, i: (b * nt + i, v_col // kvw)),
            pl.BlockSpec((SWA_WINDOW, kvw), lambda b, i: (prev(b, i), k_col // kvw)),
            pl.BlockSpec((SWA_WINDOW, kvw), lambda b, i: (prev(b, i), v_col // kvw)),
            pl.BlockSpec((t, HEAD_DIM), lambda b, i: (i, 0)),
            pl.BlockSpec((t, HEAD_DIM), lambda b, i: (i, 0)),
            pl.BlockSpec((SWA_WINDOW, HEAD_DIM), lambda b, i: (prev_pos(i), 0)),
            pl.BlockSpec((SWA_WINDOW, HEAD_DIM), lambda b, i: (prev_pos(i), 0)),
        ] + cast_in,
        out_specs=[pl.BlockSpec((t, qw), lambda b, i: (b * nt + i, 0))] + cast_out,
        out_shape=[jax.ShapeDtypeStruct((batch * seq, qw), BF16)] + cast_shapes,
        compiler_params=_params(("arbitrary", "arbitrary")),
        name="swa_attention",
    )(sinks, proj, proj, proj, proj, proj, cos, sin_signed, cos, sin_signed, *cast_ops)


def _cummax_rows(x):
    n = x.shape[0]
    row = lax.broadcasted_iota(jnp.int32, x.shape, 0)
    shift = 1
    while shift < n:
        x = jnp.maximum(x, jnp.where(row >= shift, pltpu.roll(x, shift, 0), -jnp.inf))
        shift *= 2
    return x


def _mlstm_kernel(q_ref, k_ref, v_ref, og_ref, g_ref, gt_ref, *rest, dk, dv, cast_strides):
    heads = MLSTM_HEADS
    nc = len(cast_strides)
    o_ref = rest[nc]
    state_ref, m_ref = rest[2 * nc + 1:]
    _cast_step(rest[:nc], rest[nc + 1:2 * nc + 1], cast_strides)

    @pl.when(pl.program_id(1) == 0)
    def _():
        state_ref[...] = jnp.zeros_like(state_ref)
        m_ref[...] = jnp.zeros_like(m_ref)

    n = q_ref.shape[0]
    g = g_ref[...]
    gt = gt_ref[...]
    row = lax.broadcasted_iota(jnp.int32, (n, n), 0)
    col = lax.broadcasted_iota(jnp.int32, (n, n), 1)
    causal = col <= row
    b_row_all = _cumsum_lanes(_log_sigmoid(gt), (row <= col).astype(BF16))
    b_all = pltpu.roll(_cumsum_rows(causal.astype(BF16), _log_sigmoid(g)), LANES - heads, 1)
    kc_all = g - b_all
    kcmax_all = _cummax_rows(kc_all)
    spread = (lax.broadcasted_iota(jnp.int32, (LANES, heads * LANES), 1) // LANES
              == lax.broadcasted_iota(jnp.int32, (LANES, heads * LANES), 0)).astype(BF16)
    b_wide, kc_wide, kcmax_wide = (
        sum(jnp.dot(term, spread, preferred_element_type=F32) for term in _split3(x))
        for x in (b_all, kc_all, kcmax_all))
    ones = jnp.ones((n, LANES), BF16)
    tiles = n // LANES

    def wide(x, reps):
        return jnp.concatenate([x] * reps, axis=1)

    for h in range(heads):
        hs = slice(h * LANES, (h + 1) * LANES)
        b_col, kc_col, kcmax_col = b_wide[:, hs], kc_wide[:, hs], kcmax_wide[:, hs]
        i_row = gt[h:h + 1, :]
        b_row = b_row_all[heads + h:heads + h + 1, :]
        b_last = b_row[:, n - 1:n]
        m_prev = m_ref[h]

        m_inter = b_col + m_prev
        m_t = jnp.maximum(m_inter, b_col + kcmax_col)
        inter = jnp.exp(m_inter - m_t)
        d = jnp.where(causal, (wide(b_col, tiles) - b_row) + i_row, -jnp.inf)
        wgt = jnp.exp(d - wide(m_t, tiles))

        qh = q_ref[:, h * dk:(h + 1) * dk]
        kf = k_ref[:, h * dk:(h + 1) * dk].astype(F32) * (dk ** -0.5)
        kh = kf.astype(BF16)
        v_ext = jnp.concatenate([v_ref[:, h * dv:(h + 1) * dv], ones], axis=1)
        state = state_ref[h]

        sm = wgt * lax.dot_general(qh, kh, NT_DIMS, preferred_element_type=F32)
        tot = wide(inter, dv // LANES + 1) * jnp.dot(qh, state.astype(BF16), preferred_element_type=F32)
        tot = tot + jnp.dot(sm.astype(BF16), v_ext, preferred_element_type=F32)
        den = jnp.maximum(jnp.abs(tot[:, dv:]), jnp.exp(-m_t))
        hid = tot[:, :dv] * wide(1.0 / den, dv // LANES)
        gate = _sigmoid(og_ref[:, h * dv:(h + 1) * dv].astype(F32))
        o_ref[:, h * dv:(h + 1) * dv] = (gate * hid).astype(o_ref.dtype)

        m_new = jnp.maximum(b_last + m_prev, jnp.max((b_last - b_row) + i_row, axis=-1, keepdims=True))
        decay = jnp.exp(b_last + m_prev - m_new)
        w_end = jnp.exp((b_last + kc_col) - m_new)
        kw = (kf * w_end).astype(BF16)
        state_ref[h] = decay * state + lax.dot_general(kw, v_ext, TN_DIMS, preferred_element_type=F32)
        m_ref[h] = m_new


def _mlstm(proj, g, gt, batch, seq, dk, dv, cast_sources):
    heads = MLSTM_HEADS
    n = _tile(seq, MLSTM_L, LANES)
    nc = seq // n
    qk, vw = heads * dk, heads * dv
    assert vw % qk == 0
    r = vw // qk
    strides, cast_in, cast_out, cast_shapes, cast_ops = _cast_plan(cast_sources, (batch, nc))
    return pl.pallas_call(
        functools.partial(_mlstm_kernel, dk=dk, dv=dv, cast_strides=strides),
        grid=(batch, nc),
        in_specs=[
            pl.BlockSpec((n, qk), lambda b, c: (b * nc + c, 0)),
            pl.BlockSpec((n, qk), lambda b, c: (b * nc + c, 1)),
            pl.BlockSpec((n, vw), lambda b, c: (b * nc + c, 2 // r)),
            pl.BlockSpec((n, vw), lambda b, c: (b * nc + c, 2 // r + 1)),
            pl.BlockSpec((n, LANES), lambda b, c: (b * nc + c, 0)),
            pl.BlockSpec((GATE_ROWS, n), lambda b, c: (0, b * nc + c)),
        ] + cast_in,
        out_specs=[pl.BlockSpec((n, vw), lambda b, c: (b * nc + c, 0))] + cast_out,
        out_shape=[jax.ShapeDtypeStruct((batch * seq, vw), BF16)] + cast_shapes,
        scratch_shapes=[pltpu.VMEM((heads, dk, dv + LANES), F32), pltpu.VMEM((heads, 1, 1), F32)],
        compiler_params=_params(("arbitrary", "arbitrary")),
        name="mlstm",
    )(proj, proj, proj, proj, g, gt, *cast_ops)


def _outproj_ln_kernel(*refs, alpha, n_in):
    h_refs, w_refs = refs[:n_in], refs[n_in:2 * n_in]
    x_ref, g_ref, b_ref, o_ref = refs[2 * n_in:]
    for r in range(o_ref.shape[0] // OUT_ROWS):
        rows = slice(r * OUT_ROWS, (r + 1) * OUT_ROWS)
        o_ref[rows, :] = sum(jnp.dot(h[rows, :], w[...], preferred_element_type=F32) for h, w in zip(h_refs, w_refs))
        o_ref[rows, :] = _layer_norm(alpha * x_ref[rows, :] + o_ref[rows, :], g_ref[...], b_ref[...])


def _out_projection_ln(hs, w, x, ln_g, ln_b, alpha):
    m, d = x.shape
    bm = _tile(m, OUT_BM, LANES)
    widths = [h.shape[1] for h in hs]
    assert all(wd == widths[0] for wd in widths) and w.shape[0] == sum(widths)
    in_specs = [pl.BlockSpec((bm, wd), lambda i: (i, 0)) for wd in widths]
    in_specs += [pl.BlockSpec((wd, d), lambda i, k=k: (k, 0)) for k, wd in enumerate(widths)]
    in_specs += [pl.BlockSpec((bm, d), lambda i: (i, 0)),
                 pl.BlockSpec((1, d), lambda i: (0, 0)),
                 pl.BlockSpec((1, d), lambda i: (0, 0))]
    return pl.pallas_call(
        functools.partial(_outproj_ln_kernel, alpha=alpha, n_in=len(hs)),
        grid=(m // bm,),
        in_specs=in_specs,
        out_specs=pl.BlockSpec((bm, d), lambda i: (i, 0)),
        out_shape=jax.ShapeDtypeStruct((m, d), F32),
        compiler_params=_params(("arbitrary",)),
        name="out_projection_ln",
    )(*hs, *([w] * len(hs)), x, ln_g, ln_b)


def _causal_conv(u, prev, p, r0):
    w0, w1, w2, b = p[r0:r0 + 1], p[r0 + 1:r0 + 2], p[r0 + 2:r0 + 3], p[r0 + 3:r0 + 4]
    body = b + w0 * pltpu.roll(u, 2, 0) + w1 * pltpu.roll(u, 1, 0) + w2 * u
    top = jnp.concatenate([prev, u[:HEAD_ROWS, :]], axis=0)
    head = b + w0 * pltpu.roll(top, 2, 0) + w1 * pltpu.roll(top, 1, 0) + w2 * top
    return body, head[SUBLANES:, :]


def _ffn_kernel(x_ref, wg_ref, wv_ref, cp_ref, wd_ref, lg_ref, lb_ref, o_ref,
                xb_ref, h0_ref, h1_ref, pg_ref, pv_ref, ug_ref, uv_ref, *, alpha, tiles_per_seq, nf):
    i = pl.program_id(0)
    j = pl.program_id(1)
    bm = x_ref.shape[0]
    h_refs = (h0_ref, h1_ref)

    def up(h_ref, first_step=False):
        kept_rows = jnp.where(i % tiles_per_seq != 0, SUBLANES, 0)
        keep = lax.broadcasted_iota(jnp.int32, (SUBLANES, FFN_CHUNK), 0) < kept_rows
        chunks = [slice(c * FFN_CHUNK, (c + 1) * FFN_CHUNK) for c in range(h_ref.shape[1] // FFN_CHUNK)]
        prev_g = [jnp.where(keep, pg_ref[j, :, cs], 0.0) for cs in chunks]
        prev_v = [jnp.where(keep, pv_ref[j, :, cs], 0.0) for cs in chunks]
        for r in range(bm // FFN_ROWS):
            rows = slice(r * FFN_ROWS, (r + 1) * FFN_ROWS)
            if first_step:
                xb_ref[rows, :] = x_ref[rows, :].astype(BF16)
            xr = xb_ref[rows, :]
            for c, cs in enumerate(chunks):
                slot = (r * len(chunks) + c) % FFN_SLOTS
                ug_ref[slot] = jnp.dot(xr, wg_ref[:, cs], preferred_element_type=F32)
                uv_ref[slot] = jnp.dot(xr, wv_ref[:, cs], preferred_element_type=F32)
                ug, uv = ug_ref[slot], uv_ref[slot]
                cp = cp_ref[j, :, cs]
                cg, cg_top = _causal_conv(ug, prev_g[c], cp, 0)
                cv, cv_top = _causal_conv(uv, prev_v[c], cp, CONV_WIDTH + 1)
                prev_g[c] = ug[FFN_ROWS - SUBLANES:, :]
                prev_v[c] = uv[FFN_ROWS - SUBLANES:, :]
                h_ref[r * FFN_ROWS:(r + 1) * FFN_ROWS, cs] = (cg * _sigmoid(cg) * cv).astype(BF16)
                h_ref[r * FFN_ROWS:r * FFN_ROWS + HEAD_ROWS, cs] = (cg_top * _sigmoid(cg_top) * cv_top).astype(BF16)
        for c, cs in enumerate(chunks):
            pg_ref[j, :, cs] = prev_g[c]
            pv_ref[j, :, cs] = prev_v[c]

    def down(h_ref):
        o_ref[...] += jnp.dot(h_ref[...], wd_ref[...], preferred_element_type=F32)

    @pl.when(j == 0)
    def _():
        o_ref[...] = jnp.zeros_like(o_ref)
        up(h_refs[0], first_step=True)

    for parity in range(2):
        @pl.when(jnp.logical_and(jnp.logical_and(j > 0, j < nf), j % 2 == parity))
        def _():
            up(h_refs[parity])
            down(h_refs[1 - parity])

    @pl.when(j == nf)
    def _():
        h_ref = h_refs[(nf - 1) % 2]
        for r in range(bm // FFN_ROWS):
            rows = slice(r * FFN_ROWS, (r + 1) * FFN_ROWS)
            o_ref[rows, :] += jnp.dot(h_ref[rows, :], wd_ref[...], preferred_element_type=F32)
            o_ref[rows, :] = _layer_norm(alpha * x_ref[rows, :] + o_ref[rows, :], lg_ref[...], lb_ref[...])


def _conv_ffn_ln(x, w_up, w_down, conv_w, conv_b, ln_g, ln_b, alpha, seq):
    m, d = x.shape
    f = w_down.shape[0]
    bm = _tile(seq, FFN_BM, 2 * SUBLANES)
    bf = _tile(f, FFN_BF, LANES)
    nf = f // bf

    conv_p = jnp.concatenate([conv_w[:, :f], conv_b[:, :f], conv_w[:, f:], conv_b[:, f:]], axis=0)
    conv_p = conv_p.reshape(2 * (CONV_WIDTH + 1), nf, bf).transpose(1, 0, 2)

    def up_blk(j):
        return jnp.minimum(j, nf - 1)

    def down_blk(j):
        return jnp.maximum(j - 1, 0)

    return pl.pallas_call(
        functools.partial(_ffn_kernel, alpha=alpha, tiles_per_seq=seq // bm, nf=nf),
        grid=(m // bm, nf + 1),
        in_specs=[
            pl.BlockSpec((bm, d), lambda i, j: (i, 0)),
            pl.BlockSpec((d, bf), lambda i, j: (0, up_blk(j))),
            pl.BlockSpec((d, bf), lambda i, j: (0, nf + up_blk(j))),
            pl.BlockSpec(conv_p.shape, lambda i, j: (0, 0, 0)),
            pl.BlockSpec((bf, d), lambda i, j: (down_blk(j), 0)),
            pl.BlockSpec((1, d), lambda i, j: (0, 0)),
            pl.BlockSpec((1, d), lambda i, j: (0, 0)),
        ],
        out_specs=pl.BlockSpec((bm, d), lambda i, j: (i, 0)),
        out_shape=jax.ShapeDtypeStruct((m, d), F32),
        scratch_shapes=[
            pltpu.VMEM((bm, d), BF16),
            pltpu.VMEM((bm, bf), BF16),
            pltpu.VMEM((bm, bf), BF16),
            pltpu.VMEM((nf, SUBLANES, bf), F32),
            pltpu.VMEM((nf, SUBLANES, bf), F32),
            pltpu.VMEM((FFN_SLOTS, FFN_ROWS, FFN_CHUNK), F32),
            pltpu.VMEM((FFN_SLOTS, FFN_ROWS, FFN_CHUNK), F32),
        ],
        compiler_params=_params(("arbitrary", "arbitrary")),
        name="conv_ffn_ln",
    )(x, w_up, w_up, conv_p, w_down, ln_g, ln_b)


def _gate_params(w, b):
    n, d = w.shape
    wg = jnp.zeros((LANES, d), BF16).at[:n, :].set(w.astype(BF16))
    bg = jnp.zeros((1, LANES), F32).at[0, :n].set(b)
    return wg, bg


def _rope_tables(seq):
    half = HEAD_DIM // 2
    inv_freq = jnp.power(ROPE_THETA, -jnp.arange(half, dtype=F32) * (2.0 / HEAD_DIM))
    ang = jnp.arange(seq, dtype=F32)[:, None] * inv_freq[None, :]
    cos, sin = jnp.cos(ang), jnp.sin(ang)
    return jnp.concatenate([cos, cos], axis=-1), jnp.concatenate([-sin, sin], axis=-1)


def kernel(x, attn_w_in, attn_b_in, attn_sinks, attn_w_out, mlstm_w_in, mlstm_b_in, mlstm_w_out, ffn_w_up,
           ffn_conv_w, ffn_conv_b, ffn_w_down, ln1_g, ln1_b, ln2_g, ln2_b):
    batch, seq, d = x.shape
    depth = ln1_g.shape[0]
    alpha = float((2 * depth) ** 0.25)
    fox_heads = d // (2 * HEAD_DIM)
    swa_heads = d // (2 * HEAD_DIM)
    fox_dim = fox_heads * HEAD_DIM
    fox_f_off = 3 * fox_dim
    dk, dv = d // (2 * MLSTM_HEADS), d // MLSTM_HEADS
    mlstm_main = 2 * MLSTM_HEADS * dk + 2 * MLSTM_HEADS * dv
    cos, sin_signed = _rope_tables(seq)

    gate_lo, gate_hi = fox_f_off, fox_f_off + fox_heads
    attn_wt, mlstm_wt = jnp.swapaxes(attn_w_in, 1, 2), jnp.swapaxes(mlstm_w_in, 1, 2)
    attn_w_gate = lax.optimization_barrier(attn_wt[:, gate_lo:gate_hi])
    mlstm_w_gate = lax.optimization_barrier(mlstm_wt[:, mlstm_main:])
    attn_w_main = jnp.concatenate([attn_wt[:, :gate_lo], attn_wt[:, gate_hi:]], axis=1).astype(BF16)
    attn_b_main = jnp.concatenate([attn_b_in[:, :gate_lo], attn_b_in[:, gate_hi:]], axis=1)
    mlstm_w_next = None

    h = x.reshape(batch * seq, d)
    for layer in range(depth):
        j = layer // 2
        w_out = attn_w_out if layer % 2 == 0 else mlstm_w_out
        ffn_w = [(ffn_w_up, layer, ffn_w_up.shape[1]), (ffn_w_down, layer, ffn_w_down.shape[1])]
        out_w = [(w_out, j, w_out.shape[1])]
        if layer % 2 == 0:
            gates = _gate_params(attn_w_gate[j], attn_b_in[j][gate_lo:gate_hi])
            proj, g, gt = _in_projection(h, attn_w_main, j, attn_b_main[j][None, :], *gates)
            fox_w = out_w + ([(mlstm_wt, j, mlstm_main)] if layer + 1 < depth else [])
            fox, w_out_b, *rest = _fox_attention(proj, _fox_gate_cumsum(g, batch, seq), batch, seq, fox_heads, fox_w)
            mlstm_w_next = rest[0] if rest else None
            swa_q = 3 * fox_dim
            swa_k = swa_q + swa_heads * HEAD_DIM
            swa_v = swa_k + SWA_KV_HEADS * HEAD_DIM
            swa, w_up_b, w_down_b = _swa_attention(proj, attn_sinks[j], cos, sin_signed, batch, seq, swa_q, swa_k,
                                                   swa_v, swa_heads, ffn_w)
            h = _out_projection_ln([fox, swa], w_out_b, h, ln1_g[layer][None, :], ln1_b[layer][None, :], alpha)
        else:
            gates = _gate_params(mlstm_w_gate[j], mlstm_b_in[j][mlstm_main:])
            proj, g, gt = _in_projection(h, mlstm_w_next, None, mlstm_b_in[j][None, :mlstm_main], *gates)
            mixed, w_up_b, w_down_b, w_out_b = _mlstm(proj, g, gt, batch, seq, dk, dv, ffn_w + out_w)
            h = _out_projection_ln([mixed], w_out_b, h, ln1_g[layer][None, :], ln1_b[layer][None, :], alpha)
        h = _conv_ffn_ln(h, w_up_b, w_down_b, ffn_conv_w[layer], ffn_conv_b[layer][None, :],
                         ln2_g[layer][None, :], ln2_b[layer][None, :], alpha, seq)
    return h.reshape(batch, seq, d)
```
